```python
import math, functools
import jax, jax.numpy as jnp
from jax import lax
import numpy as np

D_MODEL = 1024
BATCH = 8
SEQ = 2048
DEPTH = 4
DEC_BATCH = 128
DEC_SEQ = 8
PAST_LEN = 16384
PAGE_SIZE = 128

N_MIXERS = 3
N_GLA = (DEPTH + 2) // 3
N_GDN = (DEPTH + 1) // 3
N_S5 = DEPTH // 3
ALPHA_RES = (2 * DEPTH) ** 0.25
BETA_INIT = (8 * DEPTH) ** -0.25
LN_EPS = 1e-5
NORM_EPS = 1e-6
CHUNK = 64

GLA_HEADS = 4
GLA_KW = D_MODEL // 2
GLA_VW = D_MODEL
GLA_DK = GLA_KW // GLA_HEADS
GLA_DV = GLA_VW // GLA_HEADS
GLA_LOWRANK = 16
GLA_TAU = 16.0
GLA_IN = 2 * GLA_KW + 2 * GLA_VW + GLA_LOWRANK

GDN_HEADS = 8
GDN_DK = 128
GDN_DV = 128
GDN_KW = GDN_HEADS * GDN_DK
GDN_VW = GDN_HEADS * GDN_DV
GDN_CONV = 4
GDN_CONV_CH = 2 * GDN_KW + GDN_VW
GDN_IN = GDN_CONV_CH + GDN_VW + 2 * GDN_HEADS

S5_WIDTH = D_MODEL
S5_GROUP = 16
S5_GROUPS = S5_WIDTH // S5_GROUP
S5_STATE = 64
S5_IN = 2 * S5_WIDTH

F32 = jnp.float32

kernel_name = 'hybrid_gla_gdn_s5_deepnorm_step'


def layer_norm(x, g, b):
    xf = x.astype(F32)
    mu = jnp.mean(xf, -1, keepdims=True)
    var = jnp.mean(jnp.square(xf - mu), -1, keepdims=True)
    return ((xf - mu) * lax.rsqrt(var + LN_EPS) * g.astype(F32) + b.astype(F32)).astype(x.dtype)


def rms_head(o, g):
    return o * lax.rsqrt(jnp.mean(jnp.square(o), -1, keepdims=True) + NORM_EPS) * g.astype(F32)


def l2_normalize(t):
    return t * lax.rsqrt(jnp.sum(jnp.square(t), -1, keepdims=True) + NORM_EPS)


def to_chunks(t, c):
    bt, L = t.shape[:2]
    t = t.reshape((bt, L // c, c) + t.shape[2:])
    return t.transpose((1, 0, 3, 2) + tuple(range(4, t.ndim)))


def from_chunks(t):
    n, bt, h, c, d = t.shape
    return t.transpose(1, 0, 3, 2, 4).reshape(bt, n * c, h, d)


def gla_chunked(q, k, v, log_a, s0):
    L = q.shape[1]
    c = math.gcd(L, CHUNK)
    q, k, v, log_a = (to_chunks(t, c) for t in (q, k, v, log_a))
    b = jnp.cumsum(log_a, axis=-2)
    q_dec = q * jnp.exp(b)
    k_inv = k * jnp.exp(-b)
    k_end = k * jnp.exp(b[..., -1:, :] - b)
    incl = jnp.tril(jnp.ones((c, c), bool))
    att = jnp.where(incl, jnp.einsum('nbhid,nbhjd->nbhij', q_dec, k_inv), 0.0)
    o_intra = jnp.einsum('nbhij,nbhjv->nbhiv', att, v)

    def step(s, inp):
        qd, ke, vv, bl = inp
        o_inter = jnp.einsum('bhid,bhdv->bhiv', qd, s)
        s = s * jnp.exp(bl)[..., None] + jnp.einsum('bhjd,bhjv->bhdv', ke, vv)
        return s, o_inter

    s, o_inter = lax.scan(step, s0, (q_dec, k_end, v, b[..., -1, :]))
    return from_chunks(o_intra + o_inter), s


def gdn_chunked(q, k, v, g, beta, s0):
    L = q.shape[1]
    c = math.gcd(L, CHUNK)
    q, k, v = (to_chunks(t, c) for t in (q, k, v))
    g, beta = to_chunks(g, c), to_chunks(beta, c)
    gc = jnp.cumsum(g, axis=-1)
    incl = jnp.tril(jnp.ones((c, c), bool))
    strict = jnp.tril(jnp.ones((c, c), bool), -1)
    decay = jnp.exp(jnp.where(incl, gc[..., :, None] - gc[..., None, :], -jnp.inf))
    kb = k * beta[..., None]
    a_mat = jnp.where(strict, jnp.einsum('nbhid,nbhjd->nbhij', kb, k) * decay, 0.0)
    lhs = a_mat + jnp.eye(c, dtype=a_mat.dtype)
    solve = functools.partial(lax.linalg.triangular_solve, left_side=True, lower=True, unit_diagonal=True)
    u = solve(lhs, v * beta[..., None])
    w = solve(lhs, kb * jnp.exp(gc)[..., None])
    att = jnp.einsum('nbhid,nbhjd->nbhij', q, k) * decay
    q_dec = q * jnp.exp(gc)[..., None]
    k_end = k * jnp.exp(gc[..., -1:] - gc)[..., None]
    g_end = jnp.exp(gc[..., -1])

    def step(s, inp):
        qd, ke, uu, ww, aa, ge = inp
        v_new = uu - jnp.einsum('bhcd,bhdv->bhcv', ww, s)
        o = jnp.einsum('bhcd,bhdv->bhcv', qd, s) + jnp.einsum('bhij,bhjv->bhiv', aa, v_new)
        s = s * ge[..., None, None] + jnp.einsum('bhcd,bhcv->bhdv', ke, v_new)
        return s, o

    s, o = lax.scan(step, s0, (q_dec, k_end, u, w, att, g_end))
    return from_chunks(o), s


def complex_affine_combine(e1, e2):
    a1r, a1i, b1r, b1i = e1
    a2r, a2i, b2r, b2i = e2
    return (a2r * a1r - a2i * a1i, a2r * a1i + a2i * a1r,
            a2r * b1r - a2i * b1i + b2r, a2r * b1i + a2i * b1r + b2i)


def gla_mixer(x, w_in, w_a2, b_a, norm_g, w_out, s0):
    bt, L, _ = x.shape
    h = jnp.einsum('bld,de->ble', x, w_in).astype(F32)
    q, k, v, r, lr = jnp.split(h, [GLA_KW, 2 * GLA_KW, 2 * GLA_KW + GLA_VW, 2 * GLA_KW + 2 * GLA_VW], axis=-1)
    log_a = jax.nn.log_sigmoid(jnp.einsum('blr,rk->blk', lr, w_a2.astype(F32)) + b_a.astype(F32)) / GLA_TAU
    heads = lambda t, d: t.reshape(bt, L, GLA_HEADS, d)
    o, s = gla_chunked(heads(q, GLA_DK) * GLA_DK ** -0.5, heads(k, GLA_DK), heads(v, GLA_DV),
                       heads(log_a, GLA_DK), s0.astype(F32))
    o = rms_head(o, norm_g) * jax.nn.silu(heads(r, GLA_DV))
    out = jnp.einsum('ble,ed->bld', o.reshape(bt, L, GLA_VW), w_out.astype(F32))
    return out.astype(x.dtype), s


def gdn_mixer(x, w_in, w_conv, a_log, dt_bias, norm_g, w_out, conv0, s0):
    bt, L, _ = x.shape
    h = jnp.einsum('bld,de->ble', x, w_in).astype(F32)
    qkv, z, a, b = jnp.split(h, [GDN_CONV_CH, GDN_CONV_CH + GDN_VW, GDN_CONV_CH + GDN_VW + GDN_HEADS], axis=-1)
    ext = jnp.concatenate([conv0.astype(F32), qkv], axis=1)
    wc = w_conv.astype(F32)
    conv = ext[:, 0:L] * wc[0]
    for t in range(1, GDN_CONV):
        conv = conv + ext[:, t:t + L] * wc[t]
    conv_new = ext[:, L:]
    qkv = jax.nn.silu(conv)
    q, k, v = jnp.split(qkv, [GDN_KW, 2 * GDN_KW], axis=-1)
    q = l2_normalize(q.reshape(bt, L, GDN_HEADS, GDN_DK)) * GDN_DK ** -0.5
    k = l2_normalize(k.reshape(bt, L, GDN_HEADS, GDN_DK))
    v = v.reshape(bt, L, GDN_HEADS, GDN_DV)
    g = -jnp.exp(a_log.astype(F32)) * jax.nn.softplus(a + dt_bias.astype(F32))
    beta = jax.nn.sigmoid(b)
    o, s = gdn_chunked(q, k, v, g, beta, s0.astype(F32))
    o = rms_head(o, norm_g) * jax.nn.silu(z.reshape(bt, L, GDN_HEADS, GDN_DV))
    out = jnp.einsum('ble,ed->bld', o.reshape(bt, L, GDN_VW), w_out.astype(F32))
    return out.astype(x.dtype), conv_new, s


def s5_mixer(x, w_in, lam_re, lam_im, log_dt, b_re, b_im, c_re, c_im, d, w_glu, b_glu, w_out, h0_re, h0_im):
    bt, L, _ = x.shape
    h = jnp.einsum('bld,de->ble', x, w_in).astype(F32)
    u, z = jnp.split(h, [S5_WIDTH], axis=-1)
    ug = u.reshape(bt, L, S5_GROUPS, S5_GROUP)
    lam_re, lam_im = lam_re.astype(F32), lam_im.astype(F32)
    dt = jnp.exp(log_dt.astype(F32))[:, None]
    mag = jnp.exp(lam_re * dt)
    ab_re, ab_im = mag * jnp.cos(lam_im * dt), mag * jnp.sin(lam_im * dt)
    den = jnp.square(lam_re) + jnp.square(lam_im)
    num_re = ab_re - 1.0
    coef_re = (num_re * lam_re + ab_im * lam_im) / den
    coef_im = (ab_im * lam_re - num_re * lam_im) / den
    br, bi = b_re.astype(F32), b_im.astype(F32)
    bb_re = coef_re[..., None] * br - coef_im[..., None] * bi
    bb_im = coef_re[..., None] * bi + coef_im[..., None] * br
    bu_re = jnp.einsum('gpc,blgc->blgp', bb_re, ug)
    bu_im = jnp.einsum('gpc,blgc->blgp', bb_im, ug)
    h0r, h0i = h0_re.astype(F32), h0_im.astype(F32)
    bu_re = bu_re.at[:, 0].add(ab_re * h0r - ab_im * h0i)
    bu_im = bu_im.at[:, 0].add(ab_re * h0i + ab_im * h0r)
    a_re = jnp.broadcast_to(ab_re, bu_re.shape)
    a_im = jnp.broadcast_to(ab_im, bu_im.shape)
    _, _, hs_re, hs_im = lax.associative_scan(complex_affine_combine, (a_re, a_im, bu_re, bu_im), axis=1)
    y = (jnp.einsum('gcp,blgp->blgc', c_re.astype(F32), hs_re)
         - jnp.einsum('gcp,blgp->blgc', c_im.astype(F32), hs_im) + d.astype(F32) * ug)
    y = jax.nn.gelu(y.reshape(bt, L, S5_WIDTH))
    y1, y2 = jnp.split(jnp.einsum('blw,wv->blv', y, w_glu.astype(F32)) + b_glu.astype(F32), 2, axis=-1)
    y = y1 * jax.nn.sigmoid(y2) * jax.nn.silu(z)
    out = jnp.einsum('blw,wd->bld', y, w_out.astype(F32))
    return out.astype(x.dtype), hs_re[:, -1], hs_im[:, -1]


def trunk(x, st_gla, st_gdn, st_conv, st_re, st_im, wts):
    out_gla, out_gdn, out_conv, out_re, out_im = [], [], [], [], []
    for i in range(DEPTH):
        j = i // N_MIXERS
        kind = i % N_MIXERS
        if kind == 0:
            f, s = gla_mixer(x, wts['gla_w_in'][j], wts['gla_w_a2'][j], wts['gla_b_a'][j],
                             wts['gla_norm_g'][j], wts['gla_w_out'][j], st_gla[j])
            out_gla.append(s)
        elif kind == 1:
            f, cv, s = gdn_mixer(x, wts['gdn_w_in'][j], wts['gdn_w_conv'][j], wts['gdn_a_log'][j],
                                 wts['gdn_dt_bias'][j], wts['gdn_norm_g'][j], wts['gdn_w_out'][j],
                                 st_conv[j], st_gdn[j])
            out_gdn.append(s)
            out_conv.append(cv)
        else:
            f, hr, hi = s5_mixer(x, wts['s5_w_in'][j], wts['s5_lam_re'][j], wts['s5_lam_im'][j],
                                 wts['s5_log_dt'][j], wts['s5_b_re'][j], wts['s5_b_im'][j],
                                 wts['s5_c_re'][j], wts['s5_c_im'][j], wts['s5_d'][j],
                                 wts['s5_w_glu'][j], wts['s5_b_glu'][j], wts['s5_w_out'][j],
                                 st_re[j], st_im[j])
            out_re.append(hr)
            out_im.append(hi)
        x = layer_norm(ALPHA_RES * x + f, wts['ln_g'][i], wts['ln_b'][i])
    return (x, jnp.stack(out_gla), jnp.stack(out_gdn), jnp.stack(out_conv),
            jnp.stack(out_re), jnp.stack(out_im))


def setup_inputs(seed: int = 0) -> dict:
    key = jax.random.key(seed)
    ks = iter(jax.random.split(key, 48))
    nrm = lambda shape, scale: jax.random.normal(next(ks), shape, F32) * scale
    uni = lambda shape, lo, hi: jax.random.uniform(next(ks), shape, F32, lo, hi)
    x_prompt = nrm((BATCH, SEQ, D_MODEL), 1.0)
    x_sample = nrm((DEC_BATCH, DEC_SEQ, D_MODEL), 1.0)
    state_gla = nrm((N_GLA, DEC_BATCH, GLA_HEADS, GLA_DK, GLA_DV), 0.1)
    state_gdn = nrm((N_GDN, DEC_BATCH, GDN_HEADS, GDN_DK, GDN_DV), 0.1)
    state_gdn_conv = nrm((N_GDN, DEC_BATCH, GDN_CONV - 1, GDN_CONV_CH), 1.0)
    state_s5_re = nrm((N_S5, DEC_BATCH, S5_GROUPS, S5_STATE), 0.5)
    state_s5_im = nrm((N_S5, DEC_BATCH, S5_GROUPS, S5_STATE), 0.5)
    ln_g = 1.0 + nrm((DEPTH, D_MODEL), 0.01)
    ln_b = nrm((DEPTH, D_MODEL), 0.01)
    gla_w_in = nrm((N_GLA, D_MODEL, GLA_IN), D_MODEL ** -0.5)
    gla_w_a2 = nrm((N_GLA, GLA_LOWRANK, GLA_KW), GLA_LOWRANK ** -0.5)
    gla_b_a = nrm((N_GLA, GLA_KW), 0.1)
    gla_norm_g = 1.0 + nrm((N_GLA, GLA_DV), 0.01)
    gla_w_out = nrm((N_GLA, GLA_VW, D_MODEL), GLA_VW ** -0.5 * BETA_INIT)
    gdn_w_in = nrm((N_GDN, D_MODEL, GDN_IN), D_MODEL ** -0.5)
    gdn_w_conv = nrm((N_GDN, GDN_CONV, GDN_CONV_CH), GDN_CONV ** -0.5)
    gdn_a_log = jnp.log(uni((N_GDN, GDN_HEADS), 1.0, 16.0))
    gdn_dt = jnp.exp(uni((N_GDN, GDN_HEADS), math.log(1e-3), math.log(1e-1)))
    gdn_dt_bias = gdn_dt + jnp.log(-jnp.expm1(-gdn_dt))
    gdn_norm_g = 1.0 + nrm((N_GDN, GDN_DV), 0.01)
    gdn_w_out = nrm((N_GDN, GDN_VW, D_MODEL), GDN_VW ** -0.5 * BETA_INIT)
    s5_w_in = nrm((N_S5, D_MODEL, S5_IN), D_MODEL ** -0.5)
    s5_lam_re = -0.5 + nrm((N_S5, S5_GROUPS, S5_STATE), 0.01)
    s5_lam_im = math.pi * jnp.arange(S5_STATE, dtype=F32) + nrm((N_S5, S5_GROUPS, S5_STATE), 0.01)
    s5_log_dt = uni((N_S5, S5_GROUPS), math.log(1e-3), math.log(1e-1))
    s5_b_re = nrm((N_S5, S5_GROUPS, S5_STATE, S5_GROUP), (2 * S5_GROUP) ** -0.5)
    s5_b_im = nrm((N_S5, S5_GROUPS, S5_STATE, S5_GROUP), (2 * S5_GROUP) ** -0.5)
    s5_c_re = nrm((N_S5, S5_GROUPS, S5_GROUP, S5_STATE), (2 * S5_STATE) ** -0.5)
    s5_c_im = nrm((N_S5, S5_GROUPS, S5_GROUP, S5_STATE), (2 * S5_STATE) ** -0.5)
    s5_d = nrm((N_S5, S5_GROUPS, S5_GROUP), 1.0)
    s5_w_glu = nrm((N_S5, S5_WIDTH, 2 * S5_WIDTH), S5_WIDTH ** -0.5)
    s5_b_glu = nrm((N_S5, 2 * S5_WIDTH), 0.01)
    s5_w_out = nrm((N_S5, S5_WIDTH, D_MODEL), S5_WIDTH ** -0.5 * BETA_INIT)
    return {'x_prompt': x_prompt, 'x_sample': x_sample,
            'state_gla': state_gla, 'state_gdn': state_gdn, 'state_gdn_conv': state_gdn_conv,
            'state_s5_re': state_s5_re, 'state_s5_im': state_s5_im,
            'ln_g': ln_g, 'ln_b': ln_b,
            'gla_w_in': gla_w_in, 'gla_w_a2': gla_w_a2, 'gla_b_a': gla_b_a,
            'gla_norm_g': gla_norm_g, 'gla_w_out': gla_w_out,
            'gdn_w_in': gdn_w_in, 'gdn_w_conv': gdn_w_conv, 'gdn_a_log': gdn_a_log,
            'gdn_dt_bias': gdn_dt_bias, 'gdn_norm_g': gdn_norm_g, 'gdn_w_out': gdn_w_out,
            's5_w_in': s5_w_in, 's5_lam_re': s5_lam_re, 's5_lam_im': s5_lam_im, 's5_log_dt': s5_log_dt,
            's5_b_re': s5_b_re, 's5_b_im': s5_b_im, 's5_c_re': s5_c_re, 's5_c_im': s5_c_im,
            's5_d': s5_d, 's5_w_glu': s5_w_glu, 's5_b_glu': s5_b_glu, 's5_w_out': s5_w_out}


def reference(x_prompt, x_sample, state_gla, state_gdn, state_gdn_conv, state_s5_re, state_s5_im,
              ln_g, ln_b, gla_w_in, gla_w_a2, gla_b_a, gla_norm_g, gla_w_out,
              gdn_w_in, gdn_w_conv, gdn_a_log, gdn_dt_bias, gdn_norm_g, gdn_w_out,
              s5_w_in, s5_lam_re, s5_lam_im, s5_log_dt, s5_b_re, s5_b_im, s5_c_re, s5_c_im,
              s5_d, s5_w_glu, s5_b_glu, s5_w_out):
    wts = dict(ln_g=ln_g, ln_b=ln_b,
               gla_w_in=gla_w_in, gla_w_a2=gla_w_a2, gla_b_a=gla_b_a, gla_norm_g=gla_norm_g,
               gla_w_out=gla_w_out,
               gdn_w_in=gdn_w_in, gdn_w_conv=gdn_w_conv, gdn_a_log=gdn_a_log, gdn_dt_bias=gdn_dt_bias,
               gdn_norm_g=gdn_norm_g, gdn_w_out=gdn_w_out,
               s5_w_in=s5_w_in, s5_lam_re=s5_lam_re, s5_lam_im=s5_lam_im, s5_log_dt=s5_log_dt,
               s5_b_re=s5_b_re, s5_b_im=s5_b_im, s5_c_re=s5_c_re, s5_c_im=s5_c_im, s5_d=s5_d,
               s5_w_glu=s5_w_glu, s5_b_glu=s5_b_glu, s5_w_out=s5_w_out)
    bp = x_prompt.shape[0]
    z_gla = jnp.zeros((N_GLA, bp, GLA_HEADS, GLA_DK, GLA_DV), F32)
    z_gdn = jnp.zeros((N_GDN, bp, GDN_HEADS, GDN_DK, GDN_DV), F32)
    z_conv = jnp.zeros((N_GDN, bp, GDN_CONV - 1, GDN_CONV_CH), F32)
    z_s5 = jnp.zeros((N_S5, bp, S5_GROUPS, S5_STATE), F32)
    y_prompt, p_gla, p_gdn, p_gdn_conv, p_s5_re, p_s5_im = trunk(
        x_prompt, z_gla, z_gdn, z_conv, z_s5, z_s5, wts)
    y_sample, s_gla, s_gdn, s_gdn_conv, s_s5_re, s_s5_im = trunk(
        x_sample, state_gla, state_gdn, state_gdn_conv, state_s5_re, state_s5_im, wts)
    return (y_prompt, y_sample, p_gla, p_gdn, p_gdn_conv, p_s5_re, p_s5_im,
            s_gla, s_gdn, s_gdn_conv, s_s5_re, s_s5_im)
```

```python
import functools
import math

import jax
import jax.numpy as jnp
from jax import lax
from jax.experimental import pallas as pl
from jax.experimental.pallas import tpu as pltpu

F32 = jnp.float32
BF16 = jnp.bfloat16

D_MODEL = 1024
DEPTH = 4
ALPHA_RES = (2 * DEPTH) ** 0.25
LN_EPS = 1e-5
NORM_EPS = 1e-6
CHUNK = 64

GLA_HEADS = 4
GLA_KW = 512
GLA_VW = 1024
GLA_DK = 128
GLA_DV = 256
GLA_LOWRANK = 16
GLA_TAU = 16.0
GLA_MAIN = 2 * GLA_KW + 2 * GLA_VW

GDN_HEADS = 8
GDN_DK = 128
GDN_DV = 128
GDN_KW = 1024
GDN_VW = 1024
GDN_CONV = 4
GDN_CONV_CH = 3072
GDN_MAIN = GDN_CONV_CH + GDN_VW

S5_WIDTH = 1024
S5_GROUP = 16
S5_GROUPS = 64
S5_STATE = 64
S5_KT = 4
S5_KT_W = S5_WIDTH // S5_KT
S5_KT_STATES = (S5_GROUPS // S5_KT) * S5_STATE

LANES = 128
SUBLANES = 8
VMEM_LIMIT = 56 * 1024 * 1024


def _bf(x):
    return x.astype(BF16)


def _dot(a, b):
    return jnp.dot(a, b, preferred_element_type=F32)


def _dot_nt(a, b):
    return lax.dot_general(a, b, (((1,), (1,)), ((), ())), preferred_element_type=F32)


def _dot_tn(a, b):
    return lax.dot_general(a, b, (((0,), (0,)), ((), ())), preferred_element_type=F32)


def _split_hi_lo(a):
    hi = a.astype(BF16).astype(F32)
    return hi, a - hi


def _mm3(a, b):
    ah, al = _split_hi_lo(a)
    bh, bl = _split_hi_lo(b)
    return _dot(ah, bh) + (_dot(ah, bl) + _dot(al, bh))


def _sigmoid(x):
    return 1.0 / (1.0 + jnp.exp(-x))


def _silu(x):
    return x * _sigmoid(x)


def _softplus(x):
    return jnp.maximum(x, 0.0) + jnp.log(1.0 + jnp.exp(-jnp.abs(x)))


def _gelu_tanh(x):
    return 0.5 * x * (1.0 + jnp.tanh(math.sqrt(2.0 / math.pi) * (x + 0.044715 * (x * x * x))))


def _layer_norm(x, g, b):
    mu = jnp.mean(x, axis=-1, keepdims=True)
    xc = x - mu
    var = jnp.mean(xc * xc, axis=-1, keepdims=True)
    return xc * lax.rsqrt(var + LN_EPS) * g + b


def _cumsum_rows(x, n):
    row = lax.broadcasted_iota(jnp.int32, x.shape, 0)
    s = 1
    while s < n:
        x = x + jnp.where(row >= s, pltpu.roll(x, s, axis=0), 0.0)
        s *= 2
    return x


def _pad_rows(x, n):
    if x.shape[0] == n:
        return x
    return jnp.concatenate([x, jnp.zeros((n - x.shape[0], x.shape[1]), x.dtype)], axis=0)


def _inv_unit_lower(a, c):
    ri = lax.broadcasted_iota(jnp.int32, (c, c), 0)
    ci = lax.broadcasted_iota(jnp.int32, (c, c), 1)
    p = jnp.where(ri == ci, 1.0, 0.0) - a
    steps = int(math.log2(c)) - 1
    x = _mm3(a, a)
    for s in range(steps):
        p = p + _mm3(p, x)
        if s < steps - 1:
            x = _mm3(x, x)
    return p


def _const_spec(arr):
    nd = arr.ndim
    return pl.BlockSpec(arr.shape, lambda *_: (0,) * nd, pipeline_mode=pl.Buffered(1))


def _params(sem):
    return pltpu.CompilerParams(dimension_semantics=sem, vmem_limit_bytes=VMEM_LIMIT)


class _Rows:
    def __init__(self, arr, block, index_map):
        self.arr, self.block, self.index_map = arr, block, index_map

    def spec(self):
        return pl.BlockSpec(self.block, self.index_map)


def _flat(arr, tm):
    return _Rows(arr, (tm, arr.shape[1]), lambda i: (i, 0))


def _colblock(arr, tm, width):
    n_t = arr.shape[0] // tm
    return _Rows(arr, (tm, width), lambda i: (i % n_t, i // n_t))


def _rowwise_call(body, name, n_blocks, row_ins, consts, row_outs):
    in_specs = [r.spec() for r in row_ins] + [_const_spec(c) for c in consts]
    out_specs = [r.spec() for r in row_outs]
    out_shape = [jax.ShapeDtypeStruct(r.arr.shape, F32) for r in row_outs]
    res = pl.pallas_call(
        body, grid=(n_blocks,), in_specs=in_specs, out_specs=out_specs, out_shape=out_shape,
        compiler_params=_params(("parallel",)), name=name,
    )(*[r.arr for r in row_ins], *consts)
    return res


class _OutRows:
    def __init__(self, shape, block, index_map):
        self.arr = jax.ShapeDtypeStruct(shape, F32)
        self.block, self.index_map = block, index_map

    def spec(self):
        return pl.BlockSpec(self.block, self.index_map)


def _out_flat(n, width, tm):
    return _OutRows((n, width), (tm, width), lambda i: (i, 0))


def _out_colblock(rows, cols, tm, width):
    n_t = rows // tm
    return _OutRows((rows, cols * width), (tm, width), lambda i: (i % n_t, i // n_t))


def _gla_proj_body(x_ref, w_ref, wlr_ref, wa2_ref, ba_ref, h_ref, la_ref):
    xb = _bf(x_ref[...])
    h = _dot(xb, w_ref[...])
    h_ref[:, :GLA_KW] = h[:, :GLA_KW] * GLA_DK ** -0.5
    h_ref[:, GLA_KW:] = h[:, GLA_KW:]
    lr = _dot(xb, wlr_ref[...])
    pre = _dot(_bf(lr), wa2_ref[...]) + ba_ref[...]
    la_ref[...] = -_softplus(-pre) * (1.0 / GLA_TAU)


def _gdn_proj_body(x_ref, w_ref, wab_ref, h_ref, ab_ref):
    xb = _bf(x_ref[...])
    h_ref[...] = _dot(xb, w_ref[...])
    ab_ref[...] = _dot(xb, wab_ref[...])


def _out_ln_body(og_ref, x_ref, w_ref, g_ref, b_ref, y_ref):
    out = _dot(_bf(og_ref[...]), w_ref[...])
    y_ref[...] = _layer_norm(ALPHA_RES * x_ref[...] + out, g_ref[...], b_ref[...])


def _gla_recur_body(*refs, c, tc, has_init):
    if has_init:
        h_ref, la_ref, ng_ref, s0_ref, og_ref, sout_ref, s_scr = refs
    else:
        h_ref, la_ref, ng_ref, og_ref, sout_ref, s_scr = refs
    tb = pl.program_id(1)

    @pl.when(tb == 0)
    def _():
        if has_init:
            s_scr[...] = s0_ref[0]
        else:
            s_scr[...] = jnp.zeros_like(s_scr)

    ri = lax.broadcasted_iota(jnp.int32, (c, c), 0)
    ci = lax.broadcasted_iota(jnp.int32, (c, c), 1)
    incl = ri >= ci
    ng = ng_ref[...]

    def chunk(n, carry):
        r0 = pl.multiple_of(n * c, c)
        rows = pl.ds(r0, c)
        for hh in range(GLA_HEADS):
            kq = slice(hh * GLA_DK, (hh + 1) * GLA_DK)
            kk = slice(GLA_KW + hh * GLA_DK, GLA_KW + (hh + 1) * GLA_DK)
            kv = slice(2 * GLA_KW + hh * GLA_DV, 2 * GLA_KW + (hh + 1) * GLA_DV)
            kr = slice(2 * GLA_KW + GLA_VW + hh * GLA_DV, 2 * GLA_KW + GLA_VW + (hh + 1) * GLA_DV)
            q = h_ref[rows, kq]
            k = h_ref[rows, kk]
            v = h_ref[rows, kv]
            r = h_ref[rows, kr]
            b = _cumsum_rows(la_ref[rows, kq], c)
            bl = b[c - 1:c, :]
            q_dec = q * jnp.exp(b)
            k_inv = k * jnp.exp(-b)
            k_end = k * jnp.exp(bl - b)
            att = jnp.where(incl, _dot_nt(_bf(q_dec), _bf(k_inv)), 0.0)
            s = s_scr[hh]
            vb = _bf(v)
            o = _dot(_bf(att), vb) + _dot(_bf(q_dec), _bf(s))
            ecol = jnp.transpose(jnp.broadcast_to(jnp.exp(bl), (GLA_DK, GLA_DK)))
            ecol = jnp.concatenate([ecol] * (GLA_DV // GLA_DK), axis=1)
            s_scr[hh] = s * ecol + _dot_tn(_bf(k_end), vb)
            o = o * lax.rsqrt(jnp.mean(o * o, axis=-1, keepdims=True) + NORM_EPS) * ng
            og_ref[rows, hh * GLA_DV:(hh + 1) * GLA_DV] = o * _silu(r)
        return carry

    lax.fori_loop(0, tc // c, chunk, 0)

    @pl.when(tb == pl.num_programs(1) - 1)
    def _():
        sout_ref[0] = s_scr[...]


def _time_block_map(order, n_t):
    if order == "bt":
        return lambda b, t: (b * n_t + t, 0)
    return lambda b, t: (t, b)


def _as_order(arr2d, order, n_b, n_l):
    w = arr2d.shape[1]
    if order == "bt":
        return arr2d
    return arr2d.reshape(n_l, n_b * w)


def _gla_recur(h, la, norm_g, s0, n_b, n_l, tc, in_order, out_order):
    c = math.gcd(n_l, CHUNK)
    n_t = n_l // tc
    in_map = _time_block_map(in_order, n_t)
    out_map = _time_block_map(out_order, n_t)
    has_init = s0 is not None
    st_block = (1, GLA_HEADS, GLA_DK, GLA_DV)
    st_map = lambda b, t: (b, 0, 0, 0)
    ins = [_as_order(h, in_order, n_b, n_l), _as_order(la, in_order, n_b, n_l), norm_g]
    in_specs = [pl.BlockSpec((tc, GLA_MAIN), in_map), pl.BlockSpec((tc, GLA_KW), in_map),
                _const_spec(norm_g)]
    if has_init:
        ins.append(s0)
        in_specs.append(pl.BlockSpec(st_block, st_map))
    og_shape = (n_b * n_l, GLA_VW) if out_order == "bt" else (n_l, n_b * GLA_VW)
    og, s_out = pl.pallas_call(
        functools.partial(_gla_recur_body, c=c, tc=tc, has_init=has_init),
        grid=(n_b, n_t), in_specs=in_specs,
        out_specs=[pl.BlockSpec((tc, GLA_VW), out_map), pl.BlockSpec(st_block, st_map)],
        out_shape=[jax.ShapeDtypeStruct(og_shape, F32),
                   jax.ShapeDtypeStruct((n_b, GLA_HEADS, GLA_DK, GLA_DV), F32)],
        scratch_shapes=[pltpu.VMEM((GLA_HEADS, GLA_DK, GLA_DV), F32)],
        compiler_params=_params(("parallel", "arbitrary")), name="gla_recur",
    )(*ins)
    return og.reshape(n_b * n_l, GLA_VW), s_out


def _gdn_recur_body(*refs, c, tc, has_init):
    if has_init:
        (h_ref, ab_ref, wc_ref, alog_ref, dtb_ref, ng_ref, conv0_ref, s0_ref,
         og_ref, convn_ref, sout_ref, ext_scr, qkv_scr, g_scr, beta_scr, s_scr) = refs
    else:
        (h_ref, ab_ref, wc_ref, alog_ref, dtb_ref, ng_ref,
         og_ref, convn_ref, sout_ref, ext_scr, qkv_scr, g_scr, beta_scr, s_scr) = refs
    tb = pl.program_id(1)
    pad = SUBLANES
    lo = pad - (GDN_CONV - 1)

    @pl.when(tb == 0)
    def _():
        ext_scr[0:pad, :] = jnp.zeros((pad, GDN_CONV_CH), F32)
        if has_init:
            ext_scr[lo:pad, :] = conv0_ref[0]
            s_scr[...] = s0_ref[0]
        else:
            s_scr[...] = jnp.zeros_like(s_scr)

    ext_scr[pad:pad + tc, :] = h_ref[:, 0:GDN_CONV_CH]
    conv = ext_scr[lo:lo + tc, :] * wc_ref[0:1, :]
    for j in range(1, GDN_CONV):
        conv = conv + ext_scr[lo + j:lo + j + tc, :] * wc_ref[j:j + 1, :]
    qkv_scr[...] = _silu(conv)

    @pl.when(tb == pl.num_programs(1) - 1)
    def _():
        convn_ref[0] = ext_scr[tc + lo:tc + pad, :]

    ext_scr[0:pad, :] = ext_scr[tc:tc + pad, :]

    g_scr[...] = -jnp.exp(alog_ref[...]) * _softplus(ab_ref[:, 0:LANES] + dtb_ref[...])
    beta_scr[...] = _sigmoid(ab_ref[:, LANES:2 * LANES])

    ri = lax.broadcasted_iota(jnp.int32, (c, c), 0)
    ci = lax.broadcasted_iota(jnp.int32, (c, c), 1)
    incl = ri >= ci
    strict = ri > ci
    ng = ng_ref[...]

    def chunk(n, carry):
        r0 = pl.multiple_of(n * c, c)
        rows = pl.ds(r0, c)
        gc = _cumsum_rows(g_scr[rows, :], c)
        gct = jnp.transpose(_pad_rows(gc, LANES))
        beta = beta_scr[rows, :]
        for hh in range(GDN_HEADS):
            kq = slice(hh * GDN_DK, (hh + 1) * GDN_DK)
            kk = slice(GDN_KW + hh * GDN_DK, GDN_KW + (hh + 1) * GDN_DK)
            kv = slice(2 * GDN_KW + hh * GDN_DV, 2 * GDN_KW + (hh + 1) * GDN_DV)
            kz = slice(GDN_CONV_CH + hh * GDN_DV, GDN_CONV_CH + (hh + 1) * GDN_DV)
            q = qkv_scr[rows, kq]
            k = qkv_scr[rows, kk]
            v = qkv_scr[rows, kv]
            q = q * lax.rsqrt(jnp.sum(q * q, axis=-1, keepdims=True) + NORM_EPS) * GDN_DK ** -0.5
            k = k * lax.rsqrt(jnp.sum(k * k, axis=-1, keepdims=True) + NORM_EPS)
            gcol = gc[:, hh:hh + 1]
            grow = gct[hh:hh + 1, 0:c]
            bcol = beta[:, hh:hh + 1]
            glast = gc[c - 1:c, hh:hh + 1]
            decay = jnp.where(incl, jnp.exp(jnp.where(incl, gcol - grow, 0.0)), 0.0)
            kb = k * bcol
            kbf = _bf(k)
            a_mat = jnp.where(strict, _dot_nt(_bf(kb), kbf) * decay, 0.0)
            t_inv = _inv_unit_lower(a_mat, c)
            egc = jnp.exp(gcol)
            uw = _mm3(t_inv, jnp.concatenate([v * bcol, kb * egc], axis=1))
            u = uw[:, 0:GDN_DV]
            w = uw[:, GDN_DV:]
            att = _dot_nt(_bf(q), kbf) * decay
            q_dec = q * egc
            k_end = k * jnp.exp(glast - gcol)
            s = s_scr[hh]
            sb = _bf(s)
            v_new = u - _dot(_bf(w), sb)
            vnb = _bf(v_new)
            o = _dot(_bf(q_dec), sb) + _dot(_bf(att), vnb)
            s_scr[hh] = s * jnp.exp(glast) + _dot_tn(_bf(k_end), vnb)
            o = o * lax.rsqrt(jnp.mean(o * o, axis=-1, keepdims=True) + NORM_EPS) * ng
            og_ref[rows, hh * GDN_DV:(hh + 1) * GDN_DV] = o * _silu(h_ref[rows, kz])
        return carry

    lax.fori_loop(0, tc // c, chunk, 0)

    @pl.when(tb == pl.num_programs(1) - 1)
    def _():
        sout_ref[0] = s_scr[...]


def _gdn_recur(h, ab, w_conv, a_log, dt_bias, norm_g, conv0, s0, n_b, n_l, tc, in_order, out_order):
    c = math.gcd(n_l, CHUNK)
    n_t = n_l // tc
    in_map = _time_block_map(in_order, n_t)
    out_map = _time_block_map(out_order, n_t)
    has_init = s0 is not None
    st_block = (1, GDN_HEADS, GDN_DK, GDN_DV)
    st_map = lambda b, t: (b, 0, 0, 0)
    cv_block = (1, GDN_CONV - 1, GDN_CONV_CH)
    cv_map = lambda b, t: (b, 0, 0)
    consts = [w_conv, a_log, dt_bias, norm_g]
    ins = [_as_order(h, in_order, n_b, n_l), _as_order(ab, in_order, n_b, n_l)] + consts
    in_specs = [pl.BlockSpec((tc, GDN_MAIN), in_map), pl.BlockSpec((tc, 2 * LANES), in_map)]
    in_specs += [_const_spec(a) for a in consts]
    if has_init:
        ins += [conv0, s0]
        in_specs += [pl.BlockSpec(cv_block, cv_map), pl.BlockSpec(st_block, st_map)]
    og_shape = (n_b * n_l, GDN_VW) if out_order == "bt" else (n_l, n_b * GDN_VW)
    og, conv_new, s_out = pl.pallas_call(
        functools.partial(_gdn_recur_body, c=c, tc=tc, has_init=has_init),
        grid=(n_b, n_t), in_specs=in_specs,
        out_specs=[pl.BlockSpec((tc, GDN_VW), out_map), pl.BlockSpec(cv_block, cv_map),
                   pl.BlockSpec(st_block, st_map)],
        out_shape=[jax.ShapeDtypeStruct(og_shape, F32),
                   jax.ShapeDtypeStruct((n_b, GDN_CONV - 1, GDN_CONV_CH), F32),
                   jax.ShapeDtypeStruct((n_b, GDN_HEADS, GDN_DK, GDN_DV), F32)],
        scratch_shapes=[pltpu.VMEM((tc + SUBLANES, GDN_CONV_CH), F32),
                        pltpu.VMEM((tc, GDN_CONV_CH), F32),
                        pltpu.VMEM((tc, LANES), F32),
                        pltpu.VMEM((tc, LANES), F32),
                        pltpu.VMEM((GDN_HEADS, GDN_DK, GDN_DV), F32)],
        compiler_params=_params(("parallel", "arbitrary")), name="gdn_recur",
    )(*ins)
    return og.reshape(n_b * n_l, GDN_VW), conv_new, s_out


def _s5_body(*refs, tt, bs, has_init):
    if has_init:
        (x_ref, win_ref, wb_ref, are_ref, aim_ref, wc_ref, d_ref, wglu_ref, bglu_ref, wout_ref,
         lng_ref, lnb_ref, h0_ref, y_ref, hT_ref, u_scr, z_scr, bu_scr, y_scr, st_scr) = refs
    else:
        (x_ref, win_ref, wb_ref, are_ref, aim_ref, wc_ref, d_ref, wglu_ref, bglu_ref, wout_ref,
         lng_ref, lnb_ref, y_ref, hT_ref, u_scr, z_scr, bu_scr, y_scr, st_scr) = refs
    tb = pl.program_id(0)
    ns = S5_KT_STATES

    @pl.when(tb == 0)
    def _():
        if has_init:
            st_scr[...] = h0_ref[...]
        else:
            st_scr[...] = jnp.zeros_like(st_scr)

    h = _dot(_bf(x_ref[...]), win_ref[...])
    u_scr[...] = h[:, 0:S5_WIDTH]
    z_scr[...] = h[:, S5_WIDTH:]

    for kt in range(S5_KT):
        cols = slice(kt * S5_KT_W, (kt + 1) * S5_KT_W)
        bu_scr[...] = _dot(_bf(u_scr[:, cols]), wb_ref[kt])
        a_re = jnp.broadcast_to(are_ref[kt], (SUBLANES, ns))
        a_im = jnp.broadcast_to(aim_ref[kt], (SUBLANES, ns))
        base = kt * 2 * ns

        def row_tile(rb, carry):
            r_off = pl.multiple_of(rb * SUBLANES, SUBLANES)
            st_rows = pl.ds(r_off, SUBLANES)
            h_re0 = st_scr[st_rows, base:base + ns]
            h_im0 = st_scr[st_rows, base + ns:base + 2 * ns]

            def step(t, hc):
                h_re, h_im = hc
                rows = pl.ds(pl.multiple_of(t * bs + r_off, SUBLANES), SUBLANES)
                n_re = a_re * h_re - a_im * h_im + bu_scr[rows, 0:ns]
                n_im = a_re * h_im + a_im * h_re + bu_scr[rows, ns:2 * ns]
                bu_scr[rows, 0:ns] = n_re
                bu_scr[rows, ns:2 * ns] = n_im
                return n_re, n_im

            h_re, h_im = lax.fori_loop(0, tt, step, (h_re0, h_im0))
            st_scr[st_rows, base:base + ns] = h_re
            st_scr[st_rows, base + ns:base + 2 * ns] = h_im
            return carry

        lax.fori_loop(0, bs // SUBLANES, row_tile, 0)
        y_scr[:, cols] = _dot(_bf(bu_scr[...]), wc_ref[kt]) + d_ref[:, cols] * u_scr[:, cols]

    y = _gelu_tanh(y_scr[...])
    yg = _dot(_bf(y), wglu_ref[...]) + bglu_ref[...]
    y = yg[:, 0:S5_WIDTH] * _sigmoid(yg[:, S5_WIDTH:]) * _silu(z_scr[...])
    out = _dot(_bf(y), wout_ref[...])
    y_ref[...] = _layer_norm(ALPHA_RES * x_ref[...] + out, lng_ref[...], lnb_ref[...])

    @pl.when(tb == pl.num_programs(0) - 1)
    def _():
        hT_ref[...] = st_scr[...]


def _s5_discretize(lam_re, lam_im, log_dt, b_re, b_im, c_re, c_im):
    dt = jnp.exp(log_dt)[:, None]
    mag = jnp.exp(lam_re * dt)
    ab_re, ab_im = mag * jnp.cos(lam_im * dt), mag * jnp.sin(lam_im * dt)
    den = jnp.square(lam_re) + jnp.square(lam_im)
    num_re = ab_re - 1.0
    coef_re = (num_re * lam_re + ab_im * lam_im) / den
    coef_im = (ab_im * lam_re - num_re * lam_im) / den
    bb_re = coef_re[..., None] * b_re - coef_im[..., None] * b_im
    bb_im = coef_re[..., None] * b_im + coef_im[..., None] * b_re
    gl = S5_GROUPS // S5_KT
    eye = jnp.eye(gl, dtype=F32)

    def block_b(bb):
        t = bb.reshape(S5_KT, gl, S5_STATE, S5_GROUP)
        return jnp.einsum("kgpc,gh->kgchp", t, eye).reshape(S5_KT, S5_KT_W, S5_KT_STATES)

    def block_c(cc):
        t = cc.reshape(S5_KT, gl, S5_GROUP, S5_STATE)
        return jnp.einsum("kgcp,gh->kgphc", t, eye).reshape(S5_KT, S5_KT_STATES, S5_KT_W)

    w_b = jnp.concatenate([block_b(bb_re), block_b(bb_im)], axis=2).astype(BF16)
    w_c = jnp.concatenate([block_c(c_re), -block_c(c_im)], axis=1).astype(BF16)
    a_re = ab_re.reshape(S5_KT, 1, S5_KT_STATES)
    a_im = ab_im.reshape(S5_KT, 1, S5_KT_STATES)
    return w_b, w_c, a_re, a_im


def _s5_state_in(h_re, h_im):
    n_b = h_re.shape[0]
    r = h_re.reshape(n_b, S5_KT, 1, S5_KT_STATES)
    i = h_im.reshape(n_b, S5_KT, 1, S5_KT_STATES)
    return jnp.concatenate([r, i], axis=2).reshape(n_b, S5_KT * 2 * S5_KT_STATES)


def _s5_state_out(st):
    n_b = st.shape[0]
    t = st.reshape(n_b, S5_KT, 2, S5_GROUPS // S5_KT, S5_STATE)
    return (t[:, :, 0].reshape(n_b, S5_GROUPS, S5_STATE), t[:, :, 1].reshape(n_b, S5_GROUPS, S5_STATE))


def _s5_layer(x, prep, w_in, d_vec, w_glu, b_glu, w_out, ln_g, ln_b, h0, n_b, n_l, tt):
    w_b, w_c, a_re, a_im = prep
    rows = tt * n_b
    has_init = h0 is not None
    consts = [w_in, w_b, a_re, a_im, w_c, d_vec, w_glu, b_glu, w_out, ln_g, ln_b]
    ins = [x] + consts
    in_specs = [pl.BlockSpec((rows, D_MODEL), lambda t: (t, 0))] + [_const_spec(a) for a in consts]
    if has_init:
        ins.append(h0)
        in_specs.append(_const_spec(h0))
    n_st = S5_KT * 2 * S5_KT_STATES
    y, h_t = pl.pallas_call(
        functools.partial(_s5_body, tt=tt, bs=n_b, has_init=has_init),
        grid=(n_l // tt,), in_specs=in_specs,
        out_specs=[pl.BlockSpec((rows, D_MODEL), lambda t: (t, 0)),
                   pl.BlockSpec((n_b, n_st), lambda t: (0, 0))],
        out_shape=[jax.ShapeDtypeStruct((n_l * n_b, D_MODEL), F32),
                   jax.ShapeDtypeStruct((n_b, n_st), F32)],
        scratch_shapes=[pltpu.VMEM((rows, S5_WIDTH), F32), pltpu.VMEM((rows, S5_WIDTH), F32),
                        pltpu.VMEM((rows, 2 * S5_KT_STATES), F32), pltpu.VMEM((rows, S5_WIDTH), F32),
                        pltpu.VMEM((n_b, n_st), F32)],
        compiler_params=_params(("arbitrary",)), name="s5_layer",
    )(*ins)
    return y, h_t


def _row2(v):
    return v.reshape(1, -1).astype(F32)


def _pad_cols(w, n):
    return jnp.concatenate([w, jnp.zeros((w.shape[0], n - w.shape[1]), w.dtype)], axis=1)


def _trunk(x_rows, x_out_rows, n_b, n_l, orders, tm, tc, tt, states, wts):
    n = n_b * n_l
    n_blocks = n // tm
    out_gla, out_gdn, out_conv, out_re, out_im = [], [], [], [], []
    x = None
    for i in range(DEPTH):
        j, kind = divmod(i, 3)
        in_order, out_order = orders[i]
        last = i == DEPTH - 1
        x_in = x_rows(tm) if i == 0 else _flat(x, tm)
        ln_g, ln_b = _row2(wts["ln_g"][i]), _row2(wts["ln_b"][i])

        def residual_view(recur_out_order):
            if i == 0:
                return x_rows(tm)
            if recur_out_order == in_order:
                return _flat(x, tm)
            return _colblock(x.reshape(n_l, n_b * D_MODEL), tm, D_MODEL)

        if kind == 0:
            w_in = wts["gla_w_in"][j]
            w_main = _bf(w_in[:, :GLA_MAIN])
            w_lr = _bf(_pad_cols(w_in[:, GLA_MAIN:], LANES))
            w_a2 = _bf(jnp.concatenate(
                [wts["gla_w_a2"][j], jnp.zeros((LANES - GLA_LOWRANK, GLA_KW), F32)], axis=0))
            h, la = _rowwise_call(
                _gla_proj_body, "gla_proj", n_blocks, [x_in],
                [w_main, w_lr, w_a2, _row2(wts["gla_b_a"][j])],
                [_out_flat(n, GLA_MAIN, tm), _out_flat(n, GLA_KW, tm)])
            s0 = None if states is None else states[0][j]
            recur_out = "bt" if (last and out_order == "bt") else in_order
            og, s_new = _gla_recur(h, la, _row2(wts["gla_norm_g"][j]), s0, n_b, n_l, tc,
                                   in_order, recur_out)
            out_gla.append(s_new)
            w_out = _bf(wts["gla_w_out"][j])
        elif kind == 1:
            w_in = wts["gdn_w_in"][j]
            w_main = _bf(w_in[:, :GDN_MAIN])
            w_ab = _bf(jnp.concatenate(
                [_pad_cols(w_in[:, GDN_MAIN:GDN_MAIN + GDN_HEADS], LANES),
                 _pad_cols(w_in[:, GDN_MAIN + GDN_HEADS:], LANES)], axis=1))
            h, ab = _rowwise_call(
                _gdn_proj_body, "gdn_proj", n_blocks, [x_in], [w_main, w_ab],
                [_out_flat(n, GDN_MAIN, tm), _out_flat(n, 2 * LANES, tm)])
            conv0 = None if states is None else states[2][j]
            s0 = None if states is None else states[1][j]
            recur_out = in_order
            og, conv_new, s_new = _gdn_recur(
                h, ab, wts["gdn_w_conv"][j].astype(F32),
                _pad_cols(_row2(wts["gdn_a_log"][j]), LANES), _pad_cols(_row2(wts["gdn_dt_bias"][j]), LANES),
                _row2(wts["gdn_norm_g"][j]), conv0, s0, n_b, n_l, tc, in_order, recur_out)
            out_gdn.append(s_new)
            out_conv.append(conv_new)
            w_out = _bf(wts["gdn_w_out"][j])
        else:
            prep = _s5_discretize(wts["s5_lam_re"][j].astype(F32), wts["s5_lam_im"][j].astype(F32),
                                  wts["s5_log_dt"][j].astype(F32), wts["s5_b_re"][j].astype(F32),
                                  wts["s5_b_im"][j].astype(F32), wts["s5_c_re"][j].astype(F32),
                                  wts["s5_c_im"][j].astype(F32))
            h0 = None if states is None else _s5_state_in(states[3][j], states[4][j])
            assert in_order == "tb" and out_order == "tb"
            x, h_t = _s5_layer(x, prep, _bf(wts["s5_w_in"][j]), _row2(wts["s5_d"][j]),
                               _bf(wts["s5_w_glu"][j]), _row2(wts["s5_b_glu"][j]),
                               _bf(wts["s5_w_out"][j]), ln_g, ln_b, h0, n_b, n_l, tt)
            h_re, h_im = _s5_state_out(h_t)
            out_re.append(h_re)
            out_im.append(h_im)
            continue

        x_res = residual_view(recur_out)
        if last:
            y_out = x_out_rows(tm)
        elif out_order == recur_out:
            y_out = _out_flat(n, D_MODEL, tm)
        else:
            y_out = _out_colblock(n_l, n_b, tm, D_MODEL)
        (x,) = _rowwise_call(_out_ln_body, "out_ln", n_blocks, [_flat(og, tm), x_res],
                             [w_out, ln_g, ln_b], [y_out])
        x = x.reshape(n, D_MODEL)
    return (x, jnp.stack(out_gla), jnp.stack(out_gdn), jnp.stack(out_conv),
            jnp.stack(out_re), jnp.stack(out_im))


def kernel(x_prompt, x_sample, state_gla, state_gdn, state_gdn_conv, state_s5_re, state_s5_im,
           ln_g, ln_b, gla_w_in, gla_w_a2, gla_b_a, gla_norm_g, gla_w_out,
           gdn_w_in, gdn_w_conv, gdn_a_log, gdn_dt_bias, gdn_norm_g, gdn_w_out,
           s5_w_in, s5_lam_re, s5_lam_im, s5_log_dt, s5_b_re, s5_b_im, s5_c_re, s5_c_im,
           s5_d, s5_w_glu, s5_b_glu, s5_w_out):
    wts = dict(ln_g=ln_g, ln_b=ln_b,
               gla_w_in=gla_w_in, gla_w_a2=gla_w_a2, gla_b_a=gla_b_a, gla_norm_g=gla_norm_g,
               gla_w_out=gla_w_out,
               gdn_w_in=gdn_w_in, gdn_w_conv=gdn_w_conv, gdn_a_log=gdn_a_log, gdn_dt_bias=gdn_dt_bias,
               gdn_norm_g=gdn_norm_g, gdn_w_out=gdn_w_out,
               s5_w_in=s5_w_in, s5_lam_re=s5_lam_re, s5_lam_im=s5_lam_im, s5_log_dt=s5_log_dt,
               s5_b_re=s5_b_re, s5_b_im=s5_b_im, s5_c_re=s5_c_re, s5_c_im=s5_c_im, s5_d=s5_d,
               s5_w_glu=s5_w_glu, s5_b_glu=s5_b_glu, s5_w_out=s5_w_out)

    bp, lp, _ = x_prompt.shape
    xp = x_prompt.reshape(bp * lp, D_MODEL)
    p_orders = [("bt", "bt"), ("bt", "tb"), ("tb", "tb"), ("tb", "bt")]
    yp, p_gla, p_gdn, p_conv, p_re, p_im = _trunk(
        lambda tm: _flat(xp, tm), lambda tm: _out_flat(bp * lp, D_MODEL, tm),
        bp, lp, p_orders, tm=512, tc=256, tt=64, states=None, wts=wts)
    y_prompt = yp.reshape(bp, lp, D_MODEL)

    bs, ls, _ = x_sample.shape
    xs = x_sample.reshape(bs, ls * D_MODEL)
    s_orders = [("tb", "tb")] * DEPTH
    ys, s_gla, s_gdn, s_conv, s_re, s_im = _trunk(
        lambda tm: _colblock(xs, tm, D_MODEL), lambda tm: _out_colblock(bs, ls, tm, D_MODEL),
        bs, ls, s_orders, tm=bs, tc=ls, tt=2,
        states=(state_gla, state_gdn, state_gdn_conv, state_s5_re, state_s5_im), wts=wts)
    y_sample = ys.reshape(bs, ls, D_MODEL)

    return (y_prompt, y_sample, p_gla, p_gdn, p_conv, p_re, p_im,
            s_gla, s_gdn, s_conv, s_re, s_im)
```

```python
import functools
import math

import jax
import jax.numpy as jnp
from jax import lax
from jax.experimental import pallas as pl
from jax.experimental.pallas import tpu as pltpu

F32 = jnp.float32
BF16 = jnp.bfloat16

D_MODEL = 1024
DEPTH = 4
ALPHA_RES = (2 * DEPTH) ** 0.25
LN_EPS = 1e-5
NORM_EPS = 1e-6
CHUNK = 64

GLA_HEADS = 4
GLA_KW = 512
GLA_VW = 1024
GLA_DK = 128
GLA_DV = 256
GLA_LOWRANK = 16
GLA_TAU = 16.0
GLA_MAIN = 2 * GLA_KW + 2 * GLA_VW

GDN_HEADS = 8
GDN_DK = 128
GDN_DV = 128
GDN_KW = 1024
GDN_VW = 1024
GDN_CONV = 4
GDN_CONV_CH = 3072
GDN_MAIN = GDN_CONV_CH + GDN_VW

S5_WIDTH = 1024
S5_GROUP = 16
S5_GROUPS = 64
S5_STATE = 64
S5_KT = 4
S5_KT_W = S5_WIDTH // S5_KT
S5_KT_STATES = (S5_GROUPS // S5_KT) * S5_STATE

LANES = 128
SUBLANES = 8
VMEM_LIMIT = 56 * 1024 * 1024


def _bf(x):
    return x.astype(BF16)


def _dot(a, b):
    return jnp.dot(a, b, preferred_element_type=F32)


def _dot_nt(a, b):
    return lax.dot_general(a, b, (((1,), (1,)), ((), ())), preferred_element_type=F32)


def _dot_tn(a, b):
    return lax.dot_general(a, b, (((0,), (0,)), ((), ())), preferred_element_type=F32)


def _split_hi_lo(a):
    hi = a.astype(BF16).astype(F32)
    return hi, a - hi


def _mm3(a, b):
    ah, al = _split_hi_lo(a)
    bh, bl = _split_hi_lo(b)
    return _dot(ah, bh) + (_dot(ah, bl) + _dot(al, bh))


def _sigmoid(x):
    return 1.0 / (1.0 + jnp.exp(-x))


def _silu(x):
    return x * _sigmoid(x)


def _softplus(x):
    return jnp.maximum(x, 0.0) + jnp.log(1.0 + jnp.exp(-jnp.abs(x)))


def _gelu_tanh(x):
    return 0.5 * x * (1.0 + jnp.tanh(math.sqrt(2.0 / math.pi) * (x + 0.044715 * (x * x * x))))


def _layer_norm(x, g, b):
    mu = jnp.mean(x, axis=-1, keepdims=True)
    xc = x - mu
    var = jnp.mean(xc * xc, axis=-1, keepdims=True)
    return xc * lax.rsqrt(var + LN_EPS) * g + b


def _cumsum_rows(x, n):
    row = lax.broadcasted_iota(jnp.int32, x.shape, 0)
    s = 1
    while s < n:
        x = x + jnp.where(row >= s, pltpu.roll(x, s, axis=0), 0.0)
        s *= 2
    return x


def _pad_rows(x, n):
    if x.shape[0] == n:
        return x
    return jnp.concatenate([x, jnp.zeros((n - x.shape[0], x.shape[1]), x.dtype)], axis=0)


def _inv_unit_lower(a, c):
    ri = lax.broadcasted_iota(jnp.int32, (c, c), 0)
    ci = lax.broadcasted_iota(jnp.int32, (c, c), 1)
    p = jnp.where(ri == ci, 1.0, 0.0) - a
    steps = int(math.log2(c)) - 1
    x = _mm3(a, a)
    for s in range(steps):
        p = p + _mm3(p, x)
        if s < steps - 1:
            x = _mm3(x, x)
    return p


def _const_spec(arr):
    nd = arr.ndim
    return pl.BlockSpec(arr.shape, lambda *_: (0,) * nd, pipeline_mode=pl.Buffered(1))


def _params(sem):
    return pltpu.CompilerParams(dimension_semantics=sem, vmem_limit_bytes=VMEM_LIMIT)


class _Rows:
    def __init__(self, arr, block, index_map):
        self.arr, self.block, self.index_map = arr, block, index_map

    def spec(self):
        return pl.BlockSpec(self.block, self.index_map)


def _flat(arr, tm):
    return _Rows(arr, (tm, arr.shape[1]), lambda i: (i, 0))


def _colblock(arr, tm, width):
    n_t = arr.shape[0] // tm
    return _Rows(arr, (tm, width), lambda i: (i % n_t, i // n_t))


def _rowwise_call(body, name, n_blocks, row_ins, consts, row_outs):
    in_specs = [r.spec() for r in row_ins] + [_const_spec(c) for c in consts]
    out_specs = [r.spec() for r in row_outs]
    out_shape = [jax.ShapeDtypeStruct(r.arr.shape, F32) for r in row_outs]
    res = pl.pallas_call(
        body, grid=(n_blocks,), in_specs=in_specs, out_specs=out_specs, out_shape=out_shape,
        compiler_params=_params(("parallel",)), name=name,
    )(*[r.arr for r in row_ins], *consts)
    return res


class _OutRows:
    def __init__(self, shape, block, index_map):
        self.arr = jax.ShapeDtypeStruct(shape, F32)
        self.block, self.index_map = block, index_map

    def spec(self):
        return pl.BlockSpec(self.block, self.index_map)


def _out_flat(n, width, tm):
    return _OutRows((n, width), (tm, width), lambda i: (i, 0))


def _out_colblock(rows, cols, tm, width):
    n_t = rows // tm
    return _OutRows((rows, cols * width), (tm, width), lambda i: (i % n_t, i // n_t))


def _gla_proj_body(x_ref, w_ref, wlr_ref, wa2_ref, ba_ref, h_ref, la_ref):
    xb = _bf(x_ref[...])
    h = _dot(xb, w_ref[...])
    h_ref[:, :GLA_KW] = h[:, :GLA_KW] * GLA_DK ** -0.5
    h_ref[:, GLA_KW:] = h[:, GLA_KW:]
    lr = _dot(xb, wlr_ref[...])
    pre = _dot(_bf(lr), wa2_ref[...]) + ba_ref[...]
    la_ref[...] = -_softplus(-pre) * (1.0 / GLA_TAU)


def _gdn_proj_body(x_ref, w_ref, wab_ref, h_ref, ab_ref):
    xb = _bf(x_ref[...])
    h_ref[...] = _dot(xb, w_ref[...])
    ab_ref[...] = _dot(xb, wab_ref[...])


def _out_ln_body(og_ref, x_ref, w_ref, g_ref, b_ref, y_ref):
    out = _dot(_bf(og_ref[...]), w_ref[...])
    y_ref[...] = _layer_norm(ALPHA_RES * x_ref[...] + out, g_ref[...], b_ref[...])


def _gla_recur_body(*refs, c, tc, has_init):
    if has_init:
        h_ref, la_ref, ng_ref, s0_ref, og_ref, sout_ref, s_scr = refs
    else:
        h_ref, la_ref, ng_ref, og_ref, sout_ref, s_scr = refs
    tb = pl.program_id(1)

    @pl.when(tb == 0)
    def _():
        if has_init:
            s_scr[...] = s0_ref[0]
        else:
            s_scr[...] = jnp.zeros_like(s_scr)

    ri = lax.broadcasted_iota(jnp.int32, (c, c), 0)
    ci = lax.broadcasted_iota(jnp.int32, (c, c), 1)
    incl = ri >= ci
    ng = ng_ref[...]

    def chunk(n, carry):
        r0 = pl.multiple_of(n * c, c)
        rows = pl.ds(r0, c)
        for hh in range(GLA_HEADS):
            kq = slice(hh * GLA_DK, (hh + 1) * GLA_DK)
            kk = slice(GLA_KW + hh * GLA_DK, GLA_KW + (hh + 1) * GLA_DK)
            kv = slice(2 * GLA_KW + hh * GLA_DV, 2 * GLA_KW + (hh + 1) * GLA_DV)
            kr = slice(2 * GLA_KW + GLA_VW + hh * GLA_DV, 2 * GLA_KW + GLA_VW + (hh + 1) * GLA_DV)
            q = h_ref[rows, kq]
            k = h_ref[rows, kk]
            v = h_ref[rows, kv]
            r = h_ref[rows, kr]
            b = _cumsum_rows(la_ref[rows, kq], c)
            bl = b[c - 1:c, :]
            q_dec = q * jnp.exp(b)
            k_inv = k * jnp.exp(-b)
            k_end = k * jnp.exp(bl - b)
            att = jnp.where(incl, _dot_nt(_bf(q_dec), _bf(k_inv)), 0.0)
            s = s_scr[hh]
            vb = _bf(v)
            o = _dot(_bf(att), vb) + _dot(_bf(q_dec), _bf(s))
            ecol = jnp.transpose(jnp.broadcast_to(jnp.exp(bl), (GLA_DK, GLA_DK)))
            ecol = jnp.concatenate([ecol] * (GLA_DV // GLA_DK), axis=1)
            s_scr[hh] = s * ecol + _dot_tn(_bf(k_end), vb)
            o = o * lax.rsqrt(jnp.mean(o * o, axis=-1, keepdims=True) + NORM_EPS) * ng
            og_ref[rows, hh * GLA_DV:(hh + 1) * GLA_DV] = o * _silu(r)
        return carry

    lax.fori_loop(0, tc // c, chunk, 0)

    @pl.when(tb == pl.num_programs(1) - 1)
    def _():
        sout_ref[0] = s_scr[...]


def _time_block_map(order, n_t):
    if order == "bt":
        return lambda b, t: (b * n_t + t, 0)
    return lambda b, t: (t, b)


def _as_order(arr2d, order, n_b, n_l):
    w = arr2d.shape[1]
    if order == "bt":
        return arr2d
    return arr2d.reshape(n_l, n_b * w)


def _gla_recur(h, la, norm_g, s0, n_b, n_l, tc, in_order, out_order):
    c = math.gcd(n_l, CHUNK)
    n_t = n_l // tc
    in_map = _time_block_map(in_order, n_t)
    out_map = _time_block_map(out_order, n_t)
    has_init = s0 is not None
    st_block = (1, GLA_HEADS, GLA_DK, GLA_DV)
    st_map = lambda b, t: (b, 0, 0, 0)
    ins = [_as_order(h, in_order, n_b, n_l), _as_order(la, in_order, n_b, n_l), norm_g]
    in_specs = [pl.BlockSpec((tc, GLA_MAIN), in_map), pl.BlockSpec((tc, GLA_KW), in_map),
                _const_spec(norm_g)]
    if has_init:
        ins.append(s0)
        in_specs.append(pl.BlockSpec(st_block, st_map))
    og_shape = (n_b * n_l, GLA_VW) if out_order == "bt" else (n_l, n_b * GLA_VW)
    og, s_out = pl.pallas_call(
        functools.partial(_gla_recur_body, c=c, tc=tc, has_init=has_init),
        grid=(n_b, n_t), in_specs=in_specs,
        out_specs=[pl.BlockSpec((tc, GLA_VW), out_map), pl.BlockSpec(st_block, st_map)],
        out_shape=[jax.ShapeDtypeStruct(og_shape, F32),
                   jax.ShapeDtypeStruct((n_b, GLA_HEADS, GLA_DK, GLA_DV), F32)],
        scratch_shapes=[pltpu.VMEM((GLA_HEADS, GLA_DK, GLA_DV), F32)],
        compiler_params=_params(("parallel", "arbitrary")), name="gla_recur",
    )(*ins)
    return og.reshape(n_b * n_l, GLA_VW), s_out


def _gdn_recur_body(*refs, c, tc, has_init):
    if has_init:
        (h_ref, ab_ref, wc_ref, alog_ref, dtb_ref, ng_ref, conv0_ref, s0_ref,
         og_ref, convn_ref, sout_ref, ext_scr, qkv_scr, g_scr, beta_scr, s_scr) = refs
    else:
        (h_ref, ab_ref, wc_ref, alog_ref, dtb_ref, ng_ref,
         og_ref, convn_ref, sout_ref, ext_scr, qkv_scr, g_scr, beta_scr, s_scr) = refs
    tb = pl.program_id(1)
    pad = SUBLANES
    lo = pad - (GDN_CONV - 1)

    @pl.when(tb == 0)
    def _():
        ext_scr[0:pad, :] = jnp.zeros((pad, GDN_CONV_CH), F32)
        if has_init:
            ext_scr[lo:pad, :] = conv0_ref[0]
            s_scr[...] = s0_ref[0]
        else:
            s_scr[...] = jnp.zeros_like(s_scr)

    ext_scr[pad:pad + tc, :] = h_ref[:, 0:GDN_CONV_CH]
    conv = ext_scr[lo:lo + tc, :] * wc_ref[0:1, :]
    for j in range(1, GDN_CONV):
        conv = conv + ext_scr[lo + j:lo + j + tc, :] * wc_ref[j:j + 1, :]
    qkv_scr[...] = _silu(conv)

    @pl.when(tb == pl.num_programs(1) - 1)
    def _():
        convn_ref[0] = ext_scr[tc + lo:tc + pad, :]

    ext_scr[0:pad, :] = ext_scr[tc:tc + pad, :]

    g_scr[...] = -jnp.exp(alog_ref[...]) * _softplus(ab_ref[:, 0:LANES] + dtb_ref[...])
    beta_scr[...] = _sigmoid(ab_ref[:, LANES:2 * LANES])

    ri = lax.broadcasted_iota(jnp.int32, (c, c), 0)
    ci = lax.broadcasted_iota(jnp.int32, (c, c), 1)
    incl = ri >= ci
    strict = ri > ci
    ng = ng_ref[...]

    def chunk(n, carry):
        r0 = pl.multiple_of(n * c, c)
        rows = pl.ds(r0, c)
        gc = _cumsum_rows(g_scr[rows, :], c)
        gct = jnp.transpose(_pad_rows(gc, LANES))
        beta = beta_scr[rows, :]
        for hh in range(GDN_HEADS):
            kq = slice(hh * GDN_DK, (hh + 1) * GDN_DK)
            kk = slice(GDN_KW + hh * GDN_DK, GDN_KW + (hh + 1) * GDN_DK)
            kv = slice(2 * GDN_KW + hh * GDN_DV, 2 * GDN_KW + (hh + 1) * GDN_DV)
            kz = slice(GDN_CONV_CH + hh * GDN_DV, GDN_CONV_CH + (hh + 1) * GDN_DV)
            q = qkv_scr[rows, kq]
            k = qkv_scr[rows, kk]
            v = qkv_scr[rows, kv]
            q = q * lax.rsqrt(jnp.sum(q * q, axis=-1, keepdims=True) + NORM_EPS) * GDN_DK ** -0.5
            k = k * lax.rsqrt(jnp.sum(k * k, axis=-1, keepdims=True) + NORM_EPS)
            gcol = gc[:, hh:hh + 1]
            grow = gct[hh:hh + 1, 0:c]
            bcol = beta[:, hh:hh + 1]
            glast = gc[c - 1:c, hh:hh + 1]
            decay = jnp.where(incl, jnp.exp(jnp.where(incl, gcol - grow, 0.0)), 0.0)
            kb = k * bcol
            kbf = _bf(k)
            a_mat = jnp.where(strict, _dot_nt(_bf(kb), kbf) * decay, 0.0)
            t_inv = _inv_unit_lower(a_mat, c)
            egc = jnp.exp(gcol)
            uw = _mm3(t_inv, jnp.concatenate([v * bcol, kb * egc], axis=1))
            u = uw[:, 0:GDN_DV]
            w = uw[:, GDN_DV:]
            att = _dot_nt(_bf(q), kbf) * decay
            q_dec = q * egc
            k_end = k * jnp.exp(glast - gcol)
            s = s_scr[hh]
            sb = _bf(s)
            v_new = u - _dot(_bf(w), sb)
            vnb = _bf(v_new)
            o = _dot(_bf(q_dec), sb) + _dot(_bf(att), vnb)
            s_scr[hh] = s * jnp.exp(glast) + _dot_tn(_bf(k_end), vnb)
            o = o * lax.rsqrt(jnp.mean(o * o, axis=-1, keepdims=True) + NORM_EPS) * ng
            og_ref[rows, hh * GDN_DV:(hh + 1) * GDN_DV] = o * _silu(h_ref[rows, kz])
        return carry

    lax.fori_loop(0, tc // c, chunk, 0)

    @pl.when(tb == pl.num_programs(1) - 1)
    def _():
        sout_ref[0] = s_scr[...]


def _split3_bf(x):
    h = x.astype(BF16)
    r = x - h.astype(F32)
    m = r.astype(BF16)
    return h, m, (r - m.astype(F32)).astype(BF16)


def _split2_bf(x):
    h = x.astype(BF16)
    return h, (x - h.astype(F32)).astype(BF16)


def _block_rows(x, mask):
    return jnp.where(mask, jnp.concatenate([x, x], axis=0), 0.0)


def _mm3_split(lhs, rhs_hi, rhs_lo):
    lh, ll = _split2_bf(lhs)
    return _dot(lh, rhs_hi) + (_dot(lh, rhs_lo) + _dot(ll, rhs_hi))


def _mm3_pair(lhs, x_pair, mask):
    xh, xl = _split2_bf(x_pair)
    return _mm3_split(lhs, _bf(_block_rows(xh.astype(F32), mask)), _bf(_block_rows(xl.astype(F32), mask)))


def _inv_unit_lower_pair(a_pair, eye_pair, mask, c):
    p = eye_pair - a_pair
    x = _mm3_pair(a_pair, a_pair, mask)
    steps = int(math.log2(c)) - 1
    for s in range(steps):
        if s < steps - 1:
            r = _mm3_pair(jnp.concatenate([p, x], axis=0), x, mask)
            p = p + r[0:c]
            x = r[c:2 * c]
        else:
            p = p + _mm3_pair(p, x, mask)
    return p


def _gdn_expand_matrix():
    h = jnp.arange(LANES)[:, None]
    e64 = (jnp.arange(GDN_HEADS * CHUNK)[None, :] // CHUNK == h)
    e128 = (jnp.arange(GDN_HEADS * GDN_DK)[None, :] // GDN_DK == h)
    return jnp.concatenate([e64, e128], axis=1).astype(BF16)


def _gdn_wide_body(*refs, tc, has_init):
    c = CHUNK
    if has_init:
        (h_ref, ab_ref, wc_ref, alog_ref, dtb_ref, ng_ref, eall_ref, conv0_ref, s0_ref,
         og_ref, convn_ref, sout_ref,
         ext_scr, qkv_scr, g_scr, beta_scr, u_scr, w_scr, att_scr, egc_scr, kend_scr, s_scr) = refs
    else:
        (h_ref, ab_ref, wc_ref, alog_ref, dtb_ref, ng_ref, eall_ref,
         og_ref, convn_ref, sout_ref,
         ext_scr, qkv_scr, g_scr, beta_scr, u_scr, w_scr, att_scr, egc_scr, kend_scr, s_scr) = refs
    tb = pl.program_id(1)
    pad = SUBLANES
    lo = pad - (GDN_CONV - 1)
    n_pair = GDN_HEADS // 2
    pw = 2 * c
    hw = GDN_HEADS * c

    @pl.when(tb == 0)
    def _():
        ext_scr[0:pad, :] = jnp.zeros((pad, GDN_CONV_CH), F32)
        if has_init:
            ext_scr[lo:pad, :] = conv0_ref[0]
            s_scr[...] = s0_ref[0]
        else:
            s_scr[...] = jnp.zeros_like(s_scr)

    ext_scr[pad:pad + tc, :] = h_ref[:, 0:GDN_CONV_CH]
    conv = ext_scr[lo:lo + tc, :] * wc_ref[0:1, :]
    for j in range(1, GDN_CONV):
        conv = conv + ext_scr[lo + j:lo + j + tc, :] * wc_ref[j:j + 1, :]
    qkv_scr[...] = _silu(conv)

    @pl.when(tb == pl.num_programs(1) - 1)
    def _():
        convn_ref[0] = ext_scr[tc + lo:tc + pad, :]

    ext_scr[0:pad, :] = ext_scr[tc:tc + pad, :]

    g_scr[...] = -jnp.exp(alog_ref[...]) * _softplus(ab_ref[:, 0:LANES] + dtb_ref[...])
    beta_scr[...] = _sigmoid(ab_ref[:, LANES:2 * LANES])

    ri = lax.broadcasted_iota(jnp.int32, (c, hw), 0)
    li = lax.broadcasted_iota(jnp.int32, (c, hw), 1) & (c - 1)
    incl = ri >= li
    strict = ri > li
    delta = ri == li
    eye_pair = jnp.where(delta[:, 0:pw], 1.0, 0.0)
    r2 = lax.broadcasted_iota(jnp.int32, (2 * c, pw), 0) >> 6
    mask_cc = r2 == (lax.broadcasted_iota(jnp.int32, (2 * c, pw), 1) >> 6)
    r3 = lax.broadcasted_iota(jnp.int32, (2 * c, 2 * GDN_DK), 0) >> 6
    mask_cd = r3 == (lax.broadcasted_iota(jnp.int32, (2 * c, 2 * GDN_DK), 1) >> 7)
    ones_c = jnp.ones((c, c), BF16)
    ng = ng_ref[...]

    def local(n, carry):
        rows = pl.ds(pl.multiple_of(n * c, c), c)
        gc = _cumsum_rows(g_scr[rows, :], c)
        e_all = eall_ref[...]
        sh, sm, sl = _split3_bf(jnp.concatenate([gc, beta_scr[rows, :]], axis=0))
        ex = _dot(sh, e_all) + (_dot(sm, e_all) + _dot(sl, e_all))
        gcol, bcol, gwide = ex[0:c, 0:hw], ex[c:2 * c, 0:hw], ex[0:c, hw:]
        th, tm, tl = _split3_bf(jnp.concatenate(
            [jnp.where(delta, gcol, 0.0), jnp.where(delta, bcol, 0.0)], axis=1))
        rowf = _dot(ones_c, th) + (_dot(ones_c, tm) + _dot(ones_c, tl))
        grow, brow = rowf[:, 0:hw], rowf[:, hw:]
        dec = jnp.where(incl, jnp.exp(jnp.where(incl, gcol - grow, 0.0)), 0.0)
        egrow = jnp.exp(grow)
        egc_scr[rows, :] = jnp.exp(gwide)
        eend = jnp.exp(gwide[c - 1:c, :] - gwide)

        a_pairs = []
        for p in range(n_pair):
            qs, ks = [], []
            for hh in (2 * p, 2 * p + 1):
                kq = slice(hh * GDN_DK, (hh + 1) * GDN_DK)
                kk = slice(GDN_KW + hh * GDN_DK, GDN_KW + (hh + 1) * GDN_DK)
                q = qkv_scr[rows, kq]
                k = qkv_scr[rows, kk]
                q = q * lax.rsqrt(jnp.sum(q * q, axis=-1, keepdims=True) + NORM_EPS) * GDN_DK ** -0.5
                k = k * lax.rsqrt(jnp.sum(k * k, axis=-1, keepdims=True) + NORM_EPS)
                qkv_scr[rows, kq] = q
                qkv_scr[rows, kk] = k
                kend_scr[rows, kq] = k * eend[:, kq]
                qs.append(q)
                ks.append(k)
            kh = _bf(jnp.concatenate(ks, axis=1))
            kbd_h = _bf(_block_rows(kh.astype(F32), mask_cd))
            kkqk = _dot_nt(jnp.concatenate([kh, _bf(jnp.concatenate(qs, axis=1))], axis=0), kbd_h)
            cs = slice(p * pw, (p + 1) * pw)
            a_pairs.append(jnp.where(strict[:, cs], kkqk[0:c] * dec[:, cs] * bcol[:, cs], 0.0))
            att_scr[rows, cs] = kkqk[c:2 * c] * dec[:, cs]
        ps = [eye_pair - a for a in a_pairs]
        xs = [_mm3_pair(a, a, mask_cc) for a in a_pairs]
        steps = int(math.log2(c)) - 1
        for s in range(steps):
            if s < steps - 1:
                rs = [_mm3_pair(jnp.concatenate([pp, x], axis=0), x, mask_cc) for pp, x in zip(ps, xs)]
                ps = [pp + r[0:c] for pp, r in zip(ps, rs)]
                xs = [r[c:2 * c] for r in rs]
            else:
                ps = [pp + _mm3_pair(pp, x, mask_cc) for pp, x in zip(ps, xs)]
        for p in range(n_pair):
            cs = slice(p * pw, (p + 1) * pw)
            vs = slice(p * 2 * GDN_DV, (p + 1) * 2 * GDN_DV)
            t_b = ps[p] * brow[:, cs]
            vh, vl = _split2_bf(qkv_scr[rows, 2 * GDN_KW + p * 2 * GDN_DV:2 * GDN_KW + (p + 1) * 2 * GDN_DV])
            u_scr[rows, vs] = _mm3_split(t_b, _bf(_block_rows(vh.astype(F32), mask_cd)),
                                         _bf(_block_rows(vl.astype(F32), mask_cd)))
            kh, kl = _split2_bf(qkv_scr[rows, GDN_KW + p * 2 * GDN_DK:GDN_KW + (p + 1) * 2 * GDN_DK])
            w_scr[rows, vs] = _mm3_split(t_b * egrow[:, cs], _bf(_block_rows(kh.astype(F32), mask_cd)),
                                         _bf(_block_rows(kl.astype(F32), mask_cd)))
        return carry

    lax.fori_loop(0, tc // c, local, 0)

    def recur(n, carry):
        r0 = pl.multiple_of(n * c, c)
        rows = pl.ds(r0, c)
        tail = pl.ds(pl.multiple_of(r0 + c - SUBLANES, SUBLANES), SUBLANES)
        for p in range(n_pair):
            vns, qss = [], []
            for hh in (2 * p, 2 * p + 1):
                kq = slice(hh * GDN_DK, (hh + 1) * GDN_DK)
                s = s_scr[hh]
                m1 = _dot(_bf(jnp.concatenate([w_scr[rows, kq], qkv_scr[rows, kq]], axis=0)), _bf(s))
                vn = u_scr[rows, kq] - m1[0:c]
                e_last = egc_scr[tail, kq][SUBLANES - 1:SUBLANES, :]
                s_scr[hh] = s * e_last + _dot_tn(_bf(kend_scr[rows, kq]), _bf(vn))
                vns.append(vn)
                qss.append(m1[c:2 * c])
            vs = slice(p * 2 * GDN_DV, (p + 1) * 2 * GDN_DV)
            vbd = _bf(_block_rows(jnp.concatenate(vns, axis=1), mask_cd))
            o_pair = (_dot(_bf(att_scr[rows, p * pw:(p + 1) * pw]), vbd)
                      + egc_scr[rows, vs] * jnp.concatenate(qss, axis=1))
            for i, hh in enumerate((2 * p, 2 * p + 1)):
                o = o_pair[:, i * GDN_DV:(i + 1) * GDN_DV]
                o = o * lax.rsqrt(jnp.mean(o * o, axis=-1, keepdims=True) + NORM_EPS) * ng
                kz = slice(GDN_CONV_CH + hh * GDN_DV, GDN_CONV_CH + (hh + 1) * GDN_DV)
                og_ref[rows, hh * GDN_DV:(hh + 1) * GDN_DV] = o * _silu(h_ref[rows, kz])
        return carry

    lax.fori_loop(0, tc // c, recur, 0)

    @pl.when(tb == pl.num_programs(1) - 1)
    def _():
        sout_ref[0] = s_scr[...]


def _gdn_recur(h, ab, w_conv, a_log, dt_bias, norm_g, conv0, s0, n_b, n_l, tc, in_order, out_order):
    c = math.gcd(n_l, CHUNK)
    n_t = n_l // tc
    in_map = _time_block_map(in_order, n_t)
    out_map = _time_block_map(out_order, n_t)
    has_init = s0 is not None
    st_block = (1, GDN_HEADS, GDN_DK, GDN_DV)
    st_map = lambda b, t: (b, 0, 0, 0)
    cv_block = (1, GDN_CONV - 1, GDN_CONV_CH)
    cv_map = lambda b, t: (b, 0, 0)
    wide = c == CHUNK
    consts = [w_conv, a_log, dt_bias, norm_g]
    if wide:
        consts.append(_gdn_expand_matrix())
    ins = [_as_order(h, in_order, n_b, n_l), _as_order(ab, in_order, n_b, n_l)] + consts
    in_specs = [pl.BlockSpec((tc, GDN_MAIN), in_map), pl.BlockSpec((tc, 2 * LANES), in_map)]
    in_specs += [_const_spec(a) for a in consts]
    if has_init:
        ins += [conv0, s0]
        in_specs += [pl.BlockSpec(cv_block, cv_map), pl.BlockSpec(st_block, st_map)]
    og_shape = (n_b * n_l, GDN_VW) if out_order == "bt" else (n_l, n_b * GDN_VW)
    scratch = [pltpu.VMEM((tc + SUBLANES, GDN_CONV_CH), F32),
               pltpu.VMEM((tc, GDN_CONV_CH), F32),
               pltpu.VMEM((tc, LANES), F32),
               pltpu.VMEM((tc, LANES), F32)]
    if wide:
        body = functools.partial(_gdn_wide_body, tc=tc, has_init=has_init)
        scratch += [pltpu.VMEM((tc, GDN_VW), F32), pltpu.VMEM((tc, GDN_KW), F32),
                    pltpu.VMEM((tc, GDN_HEADS * CHUNK), F32), pltpu.VMEM((tc, GDN_KW), F32),
                    pltpu.VMEM((tc, GDN_KW), F32)]
    else:
        body = functools.partial(_gdn_recur_body, c=c, tc=tc, has_init=has_init)
    scratch.append(pltpu.VMEM((GDN_HEADS, GDN_DK, GDN_DV), F32))
    og, conv_new, s_out = pl.pallas_call(
        body,
        grid=(n_b, n_t), in_specs=in_specs,
        out_specs=[pl.BlockSpec((tc, GDN_VW), out_map), pl.BlockSpec(cv_block, cv_map),
                   pl.BlockSpec(st_block, st_map)],
        out_shape=[jax.ShapeDtypeStruct(og_shape, F32),
                   jax.ShapeDtypeStruct((n_b, GDN_CONV - 1, GDN_CONV_CH), F32),
                   jax.ShapeDtypeStruct((n_b, GDN_HEADS, GDN_DK, GDN_DV), F32)],
        scratch_shapes=scratch,
        compiler_params=_params(("parallel", "arbitrary")), name="gdn_recur",
    )(*ins)
    return og.reshape(n_b * n_l, GDN_VW), conv_new, s_out


def _s5_body(*refs, tt, bs, has_init):
    if has_init:
        (x_ref, win_ref, wb_ref, are_ref, aim_ref, wc_ref, d_ref, wglu_ref, bglu_ref, wout_ref,
         lng_ref, lnb_ref, h0_ref, y_ref, hT_ref, u_scr, z_scr, bu_scr, y_scr, st_scr) = refs
    else:
        (x_ref, win_ref, wb_ref, are_ref, aim_ref, wc_ref, d_ref, wglu_ref, bglu_ref, wout_ref,
         lng_ref, lnb_ref, y_ref, hT_ref, u_scr, z_scr, bu_scr, y_scr, st_scr) = refs
    tb = pl.program_id(0)
    ns = S5_KT_STATES

    @pl.when(tb == 0)
    def _():
        if has_init:
            st_scr[...] = h0_ref[...]
        else:
            st_scr[...] = jnp.zeros_like(st_scr)

    h = _dot(_bf(x_ref[...]), win_ref[...])
    u_scr[...] = h[:, 0:S5_WIDTH]
    z_scr[...] = h[:, S5_WIDTH:]

    for kt in range(S5_KT):
        cols = slice(kt * S5_KT_W, (kt + 1) * S5_KT_W)
        bu_scr[...] = _dot(_bf(u_scr[:, cols]), wb_ref[kt])
        a_re = jnp.broadcast_to(are_ref[kt], (SUBLANES, ns))
        a_im = jnp.broadcast_to(aim_ref[kt], (SUBLANES, ns))
        base = kt * 2 * ns

        def row_tile(rb, carry):
            r_off = pl.multiple_of(rb * SUBLANES, SUBLANES)
            st_rows = pl.ds(r_off, SUBLANES)
            h_re0 = st_scr[st_rows, base:base + ns]
            h_im0 = st_scr[st_rows, base + ns:base + 2 * ns]

            def step(t, hc):
                h_re, h_im = hc
                rows = pl.ds(pl.multiple_of(t * bs + r_off, SUBLANES), SUBLANES)
                n_re = a_re * h_re - a_im * h_im + bu_scr[rows, 0:ns]
                n_im = a_re * h_im + a_im * h_re + bu_scr[rows, ns:2 * ns]
                bu_scr[rows, 0:ns] = n_re
                bu_scr[rows, ns:2 * ns] = n_im
                return n_re, n_im

            h_re, h_im = lax.fori_loop(0, tt, step, (h_re0, h_im0))
            st_scr[st_rows, base:base + ns] = h_re
            st_scr[st_rows, base + ns:base + 2 * ns] = h_im
            return carry

        lax.fori_loop(0, bs // SUBLANES, row_tile, 0)
        y_scr[:, cols] = _dot(_bf(bu_scr[...]), wc_ref[kt]) + d_ref[:, cols] * u_scr[:, cols]

    y = _gelu_tanh(y_scr[...])
    yg = _dot(_bf(y), wglu_ref[...]) + bglu_ref[...]
    y = yg[:, 0:S5_WIDTH] * _sigmoid(yg[:, S5_WIDTH:]) * _silu(z_scr[...])
    out = _dot(_bf(y), wout_ref[...])
    y_ref[...] = _layer_norm(ALPHA_RES * x_ref[...] + out, lng_ref[...], lnb_ref[...])

    @pl.when(tb == pl.num_programs(0) - 1)
    def _():
        hT_ref[...] = st_scr[...]


def _s5_discretize(lam_re, lam_im, log_dt, b_re, b_im, c_re, c_im):
    dt = jnp.exp(log_dt)[:, None]
    mag = jnp.exp(lam_re * dt)
    ab_re, ab_im = mag * jnp.cos(lam_im * dt), mag * jnp.sin(lam_im * dt)
    den = jnp.square(lam_re) + jnp.square(lam_im)
    num_re = ab_re - 1.0
    coef_re = (num_re * lam_re + ab_im * lam_im) / den
    coef_im = (ab_im * lam_re - num_re * lam_im) / den
    bb_re = coef_re[..., None] * b_re - coef_im[..., None] * b_im
    bb_im = coef_re[..., None] * b_im + coef_im[..., None] * b_re
    gl = S5_GROUPS // S5_KT
    eye = jnp.eye(gl, dtype=F32)

    def block_b(bb):
        t = bb.reshape(S5_KT, gl, S5_STATE, S5_GROUP)
        return jnp.einsum("kgpc,gh->kgchp", t, eye).reshape(S5_KT, S5_KT_W, S5_KT_STATES)

    def block_c(cc):
        t = cc.reshape(S5_KT, gl, S5_GROUP, S5_STATE)
        return jnp.einsum("kgcp,gh->kgphc", t, eye).reshape(S5_KT, S5_KT_STATES, S5_KT_W)

    w_b = jnp.concatenate([block_b(bb_re), block_b(bb_im)], axis=2).astype(BF16)
    w_c = jnp.concatenate([block_c(c_re), -block_c(c_im)], axis=1).astype(BF16)
    a_re = ab_re.reshape(S5_KT, 1, S5_KT_STATES)
    a_im = ab_im.reshape(S5_KT, 1, S5_KT_STATES)
    return w_b, w_c, a_re, a_im


def _s5_state_in(h_re, h_im):
    n_b = h_re.shape[0]
    r = h_re.reshape(n_b, S5_KT, 1, S5_KT_STATES)
    i = h_im.reshape(n_b, S5_KT, 1, S5_KT_STATES)
    return jnp.concatenate([r, i], axis=2).reshape(n_b, S5_KT * 2 * S5_KT_STATES)


def _s5_state_out(st):
    n_b = st.shape[0]
    t = st.reshape(n_b, S5_KT, 2, S5_GROUPS // S5_KT, S5_STATE)
    return (t[:, :, 0].reshape(n_b, S5_GROUPS, S5_STATE), t[:, :, 1].reshape(n_b, S5_GROUPS, S5_STATE))


def _s5_layer(x, prep, w_in, d_vec, w_glu, b_glu, w_out, ln_g, ln_b, h0, n_b, n_l, tt):
    w_b, w_c, a_re, a_im = prep
    rows = tt * n_b
    has_init = h0 is not None
    consts = [w_in, w_b, a_re, a_im, w_c, d_vec, w_glu, b_glu, w_out, ln_g, ln_b]
    ins = [x] + consts
    in_specs = [pl.BlockSpec((rows, D_MODEL), lambda t: (t, 0))] + [_const_spec(a) for a in consts]
    if has_init:
        ins.append(h0)
        in_specs.append(_const_spec(h0))
    n_st = S5_KT * 2 * S5_KT_STATES
    y, h_t = pl.pallas_call(
        functools.partial(_s5_body, tt=tt, bs=n_b, has_init=has_init),
        grid=(n_l // tt,), in_specs=in_specs,
        out_specs=[pl.BlockSpec((rows, D_MODEL), lambda t: (t, 0)),
                   pl.BlockSpec((n_b, n_st), lambda t: (0, 0))],
        out_shape=[jax.ShapeDtypeStruct((n_l * n_b, D_MODEL), F32),
                   jax.ShapeDtypeStruct((n_b, n_st), F32)],
        scratch_shapes=[pltpu.VMEM((rows, S5_WIDTH), F32), pltpu.VMEM((rows, S5_WIDTH), F32),
                        pltpu.VMEM((rows, 2 * S5_KT_STATES), F32), pltpu.VMEM((rows, S5_WIDTH), F32),
                        pltpu.VMEM((n_b, n_st), F32)],
        compiler_params=_params(("arbitrary",)), name="s5_layer",
    )(*ins)
    return y, h_t


def _row2(v):
    return v.reshape(1, -1).astype(F32)


def _pad_cols(w, n):
    return jnp.concatenate([w, jnp.zeros((w.shape[0], n - w.shape[1]), w.dtype)], axis=1)


def _trunk(x_rows, x_out_rows, n_b, n_l, orders, tm, tc, tt, states, wts):
    n = n_b * n_l
    n_blocks = n // tm
    out_gla, out_gdn, out_conv, out_re, out_im = [], [], [], [], []
    x = None
    for i in range(DEPTH):
        j, kind = divmod(i, 3)
        in_order, out_order = orders[i]
        last = i == DEPTH - 1
        x_in = x_rows(tm) if i == 0 else _flat(x, tm)
        ln_g, ln_b = _row2(wts["ln_g"][i]), _row2(wts["ln_b"][i])

        def residual_view(recur_out_order):
            if i == 0:
                return x_rows(tm)
            if recur_out_order == in_order:
                return _flat(x, tm)
            return _colblock(x.reshape(n_l, n_b * D_MODEL), tm, D_MODEL)

        if kind == 0:
            w_in = wts["gla_w_in"][j]
            w_main = _bf(w_in[:, :GLA_MAIN])
            w_lr = _bf(_pad_cols(w_in[:, GLA_MAIN:], LANES))
            w_a2 = _bf(jnp.concatenate(
                [wts["gla_w_a2"][j], jnp.zeros((LANES - GLA_LOWRANK, GLA_KW), F32)], axis=0))
            h, la = _rowwise_call(
                _gla_proj_body, "gla_proj", n_blocks, [x_in],
                [w_main, w_lr, w_a2, _row2(wts["gla_b_a"][j])],
                [_out_flat(n, GLA_MAIN, tm), _out_flat(n, GLA_KW, tm)])
            s0 = None if states is None else states[0][j]
            recur_out = "bt" if (last and out_order == "bt") else in_order
            og, s_new = _gla_recur(h, la, _row2(wts["gla_norm_g"][j]), s0, n_b, n_l, tc,
                                   in_order, recur_out)
            out_gla.append(s_new)
            w_out = _bf(wts["gla_w_out"][j])
        elif kind == 1:
            w_in = wts["gdn_w_in"][j]
            w_main = _bf(w_in[:, :GDN_MAIN])
            w_ab = _bf(jnp.concatenate(
                [_pad_cols(w_in[:, GDN_MAIN:GDN_MAIN + GDN_HEADS], LANES),
                 _pad_cols(w_in[:, GDN_MAIN + GDN_HEADS:], LANES)], axis=1))
            h, ab = _rowwise_call(
                _gdn_proj_body, "gdn_proj", n_blocks, [x_in], [w_main, w_ab],
                [_out_flat(n, GDN_MAIN, tm), _out_flat(n, 2 * LANES, tm)])
            conv0 = None if states is None else states[2][j]
            s0 = None if states is None else states[1][j]
            recur_out = in_order
            og, conv_new, s_new = _gdn_recur(
                h, ab, wts["gdn_w_conv"][j].astype(F32),
                _pad_cols(_row2(wts["gdn_a_log"][j]), LANES), _pad_cols(_row2(wts["gdn_dt_bias"][j]), LANES),
                _row2(wts["gdn_norm_g"][j]), conv0, s0, n_b, n_l, tc, in_order, recur_out)
            out_gdn.append(s_new)
            out_conv.append(conv_new)
            w_out = _bf(wts["gdn_w_out"][j])
        else:
            prep = _s5_discretize(wts["s5_lam_re"][j].astype(F32), wts["s5_lam_im"][j].astype(F32),
                                  wts["s5_log_dt"][j].astype(F32), wts["s5_b_re"][j].astype(F32),
                                  wts["s5_b_im"][j].astype(F32), wts["s5_c_re"][j].astype(F32),
                                  wts["s5_c_im"][j].astype(F32))
            h0 = None if states is None else _s5_state_in(states[3][j], states[4][j])
            assert in_order == "tb" and out_order == "tb"
            x, h_t = _s5_layer(x, prep, _bf(wts["s5_w_in"][j]), _row2(wts["s5_d"][j]),
                               _bf(wts["s5_w_glu"][j]), _row2(wts["s5_b_glu"][j]),
                               _bf(wts["s5_w_out"][j]), ln_g, ln_b, h0, n_b, n_l, tt)
            h_re, h_im = _s5_state_out(h_t)
            out_re.append(h_re)
            out_im.append(h_im)
            continue

        x_res = residual_view(recur_out)
        if last:
            y_out = x_out_rows(tm)
        elif out_order == recur_out:
            y_out = _out_flat(n, D_MODEL, tm)
        else:
            y_out = _out_colblock(n_l, n_b, tm, D_MODEL)
        (x,) = _rowwise_call(_out_ln_body, "out_ln", n_blocks, [_flat(og, tm), x_res],
                             [w_out, ln_g, ln_b], [y_out])
        x = x.reshape(n, D_MODEL)
    return (x, jnp.stack(out_gla), jnp.stack(out_gdn), jnp.stack(out_conv),
            jnp.stack(out_re), jnp.stack(out_im))


def kernel(x_prompt, x_sample, state_gla, state_gdn, state_gdn_conv, state_s5_re, state_s5_im,
           ln_g, ln_b, gla_w_in, gla_w_a2, gla_b_a, gla_norm_g, gla_w_out,
           gdn_w_in, gdn_w_conv, gdn_a_log, gdn_dt_bias, gdn_norm_g, gdn_w_out,
           s5_w_in, s5_lam_re, s5_lam_im, s5_log_dt, s5_b_re, s5_b_im, s5_c_re, s5_c_im,
           s5_d, s5_w_glu, s5_b_glu, s5_w_out):
    wts = dict(ln_g=ln_g, ln_b=ln_b,
               gla_w_in=gla_w_in, gla_w_a2=gla_w_a2, gla_b_a=gla_b_a, gla_norm_g=gla_norm_g,
               gla_w_out=gla_w_out,
               gdn_w_in=gdn_w_in, gdn_w_conv=gdn_w_conv, gdn_a_log=gdn_a_log, gdn_dt_bias=gdn_dt_bias,
               gdn_norm_g=gdn_norm_g, gdn_w_out=gdn_w_out,
               s5_w_in=s5_w_in, s5_lam_re=s5_lam_re, s5_lam_im=s5_lam_im, s5_log_dt=s5_log_dt,
               s5_b_re=s5_b_re, s5_b_im=s5_b_im, s5_c_re=s5_c_re, s5_c_im=s5_c_im, s5_d=s5_d,
               s5_w_glu=s5_w_glu, s5_b_glu=s5_b_glu, s5_w_out=s5_w_out)

    bp, lp, _ = x_prompt.shape
    xp = x_prompt.reshape(bp * lp, D_MODEL)
    p_orders = [("bt", "bt"), ("bt", "tb"), ("tb", "tb"), ("tb", "bt")]
    yp, p_gla, p_gdn, p_conv, p_re, p_im = _trunk(
        lambda tm: _flat(xp, tm), lambda tm: _out_flat(bp * lp, D_MODEL, tm),
        bp, lp, p_orders, tm=512, tc=256, tt=64, states=None, wts=wts)
    y_prompt = yp.reshape(bp, lp, D_MODEL)

    bs, ls, _ = x_sample.shape
    xs = x_sample.reshape(bs, ls * D_MODEL)
    s_orders = [("tb", "tb")] * DEPTH
    ys, s_gla, s_gdn, s_conv, s_re, s_im = _trunk(
        lambda tm: _colblock(xs, tm, D_MODEL), lambda tm: _out_colblock(bs, ls, tm, D_MODEL),
        bs, ls, s_orders, tm=bs, tc=ls, tt=2,
        states=(state_gla, state_gdn, state_gdn_conv, state_s5_re, state_s5_im), wts=wts)
    y_sample = ys.reshape(bs, ls, D_MODEL)

    return (y_prompt, y_sample, p_gla, p_gdn, p_conv, p_re, p_im,
            s_gla, s_gdn, s_conv, s_re, s_im)
```

```python
import functools
import math

import jax
import jax.numpy as jnp
from jax import lax
from jax.experimental import pallas as pl
from jax.experimental.pallas import tpu as pltpu

F32 = jnp.float32
BF16 = jnp.bfloat16

D_MODEL = 1024
DEPTH = 4
ALPHA_RES = (2 * DEPTH) ** 0.25
LN_EPS = 1e-5
NORM_EPS = 1e-6
CHUNK = 64

GLA_HEADS = 4
GLA_KW = 512
GLA_VW = 1024
GLA_DK = 128
GLA_DV = 256
GLA_LOWRANK = 16
GLA_TAU = 16.0
GLA_MAIN = 2 * GLA_KW + 2 * GLA_VW

GDN_HEADS = 8
GDN_DK = 128
GDN_DV = 128
GDN_KW = 1024
GDN_VW = 1024
GDN_CONV = 4
GDN_CONV_CH = 3072
GDN_MAIN = GDN_CONV_CH + GDN_VW

S5_WIDTH = 1024
S5_GROUP = 16
S5_GROUPS = 64
S5_STATE = 64
S5_KT = 4
S5_KT_W = S5_WIDTH // S5_KT
S5_KT_STATES = (S5_GROUPS // S5_KT) * S5_STATE
S5_NSTATE = S5_GROUPS * S5_STATE

LANES = 128
SUBLANES = 8
VMEM_LIMIT = 56 * 1024 * 1024
ROW_BLOCK = 512
LONG_TIME_BLOCK = 256
S5_LONG_STEPS = 64
S5_SHORT_SEQS = 64


def _bf(x):
    return x.astype(BF16)


def _dot(a, b):
    return jnp.dot(a, b, preferred_element_type=F32)


def _dot_nt(a, b):
    return lax.dot_general(a, b, (((1,), (1,)), ((), ())), preferred_element_type=F32)


def _dot_tn(a, b):
    return lax.dot_general(a, b, (((0,), (0,)), ((), ())), preferred_element_type=F32)


def _split2_bf(x):
    h = x.astype(BF16)
    return h, (x - h.astype(F32)).astype(BF16)


def _split3_bf(x):
    h = x.astype(BF16)
    r = x - h.astype(F32)
    m = r.astype(BF16)
    return h, m, (r - m.astype(F32)).astype(BF16)


def _sigmoid(x):
    return 1.0 / (1.0 + jnp.exp(-x))


def _silu(x):
    return x * _sigmoid(x)


def _softplus(x):
    return jnp.maximum(x, 0.0) + jnp.log(1.0 + jnp.exp(-jnp.abs(x)))


def _gelu_tanh(x):
    return 0.5 * x * (1.0 + jnp.tanh(math.sqrt(2.0 / math.pi) * (x + 0.044715 * (x * x * x))))


def _layer_norm(x, g, b):
    mu = jnp.mean(x, axis=-1, keepdims=True)
    xc = x - mu
    var = jnp.mean(xc * xc, axis=-1, keepdims=True)
    return xc * lax.rsqrt(var + LN_EPS) * g + b


def _cumsum_rows(x, seg):
    row = lax.broadcasted_iota(jnp.int32, x.shape, 0) & (seg - 1)
    s = 1
    while s < seg:
        x = x + jnp.where(row >= s, pltpu.roll(x, s, axis=0), 0.0)
        s *= 2
    return x


def _seg_last_rows(x, seg):
    c, w = x.shape
    if seg == c:
        return jnp.broadcast_to(x[c - 1:c, :], (c, w))
    x3 = x.reshape(c // seg, seg, w)
    return jnp.broadcast_to(x3[:, seg - 1:seg, :], x3.shape).reshape(c, w)


def _seg_masks(shape, seg):
    ri = lax.broadcasted_iota(jnp.int32, shape, 0)
    li = lax.broadcasted_iota(jnp.int32, shape, 1) & (CHUNK - 1)
    shift = int(math.log2(seg))
    same = (ri >> shift) == (li >> shift)
    return same & (ri >= li), same & (ri > li), ri == li


def _const_spec(arr):
    nd = arr.ndim
    return pl.BlockSpec(arr.shape, lambda *_: (0,) * nd, pipeline_mode=pl.Buffered(1))


def _params(sem):
    return pltpu.CompilerParams(dimension_semantics=sem, vmem_limit_bytes=VMEM_LIMIT)


def _rowwise_call(body, name, rows, consts, out_widths):
    n = rows[0].shape[0]
    tm = min(ROW_BLOCK, n)
    spec = lambda w: pl.BlockSpec((tm, w), lambda i: (i, 0))
    return pl.pallas_call(
        body, grid=(n // tm,),
        in_specs=[spec(r.shape[1]) for r in rows] + [_const_spec(c) for c in consts],
        out_specs=[spec(w) for w in out_widths],
        out_shape=[jax.ShapeDtypeStruct((n, w), F32) for w in out_widths],
        compiler_params=_params(("parallel",)), name=name,
    )(*rows, *consts)


def _gla_proj_body(x_ref, w_ref, wlr_ref, wa2_ref, ba_ref, h_ref, la_ref):
    xb = _bf(x_ref[...])
    h = _dot(xb, w_ref[...])
    h_ref[:, :GLA_KW] = h[:, :GLA_KW] * GLA_DK ** -0.5
    h_ref[:, GLA_KW:] = h[:, GLA_KW:]
    lr = _dot(xb, wlr_ref[...])
    pre = _dot(_bf(lr), wa2_ref[...]) + ba_ref[...]
    la_ref[...] = -_softplus(-pre) * (1.0 / GLA_TAU)


def _gdn_proj_body(x_ref, w_ref, wab_ref, h_ref, ab_ref):
    xb = _bf(x_ref[...])
    h_ref[...] = _dot(xb, w_ref[...])
    ab_ref[...] = _dot(xb, wab_ref[...])


def _out_ln_body(og_ref, x_ref, w_ref, g_ref, b_ref, y_ref):
    out = _dot(_bf(og_ref[...]), w_ref[...])
    y_ref[...] = _layer_norm(ALPHA_RES * x_ref[...] + out, g_ref[...], b_ref[...])


class _Geometry:
    def __init__(self, n_b, n_l):
        self.n_b, self.n_l = n_b, n_l
        self.long = n_l >= CHUNK
        if self.long:
            assert n_l % LONG_TIME_BLOCK == 0
            self.seg = CHUNK
            self.tc = LONG_TIME_BLOCK
            self.n_t = n_l // self.tc
            self.grid = (n_b, self.n_t)
            self.sem = ("parallel", "arbitrary")
            self.seqs = 1
            self.row_map = lambda b, t: (b * self.n_t + t, 0)
            self.seq_block = lambda b, t: b
        else:
            assert CHUNK % n_l == 0 and n_l % SUBLANES == 0 and (n_b * n_l) % CHUNK == 0
            self.seg = n_l
            self.tc = CHUNK
            self.grid = (n_b * n_l // self.tc,)
            self.sem = ("parallel",)
            self.seqs = self.tc // n_l
            self.row_map = lambda i: (i, 0)
            self.seq_block = lambda i: i

    def state_spec(self, layer, tail):
        zeros = (0,) * len(tail)
        return pl.BlockSpec((1, self.seqs) + tail, lambda *g: (layer, self.seq_block(*g)) + zeros)

    def last_time_block(self):
        return pl.program_id(1) == pl.num_programs(1) - 1 if self.long else None


def _state_access(geo, s0_ref, sout_ref, s_scr):
    if geo.long:
        return (lambda sq, hh: s_scr[hh]), (lambda sq, hh, v: s_scr.__setitem__(hh, v))
    return (lambda sq, hh: s0_ref[0, sq, hh]), (lambda sq, hh, v: sout_ref.__setitem__((0, sq, hh), v))


def _gla_recur_body(*refs, geo, has_alias):
    refs = list(refs)
    h_ref, la_ref, ng_ref = refs[:3]
    pos = 3
    s0_ref = None
    if not geo.long:
        s0_ref = refs[pos]
        pos += 1
    if has_alias:
        pos += 1
    og_ref, sout_ref = refs[pos:pos + 2]
    s_scr = refs[pos + 2] if geo.long else None
    c, seg, tc = CHUNK, geo.seg, geo.tc
    n_seg = c // seg
    get_state, put_state = _state_access(geo, s0_ref, sout_ref, s_scr)

    if geo.long:
        @pl.when(pl.program_id(1) == 0)
        def _():
            s_scr[...] = jnp.zeros_like(s_scr)

    incl, _, _ = _seg_masks((c, c), seg)
    ng = ng_ref[...]

    def chunk(n, carry):
        r0 = pl.multiple_of(n * c, c)
        rows = pl.ds(r0, c)
        for hh in range(GLA_HEADS):
            kq = slice(hh * GLA_DK, (hh + 1) * GLA_DK)
            kk = slice(GLA_KW + hh * GLA_DK, GLA_KW + (hh + 1) * GLA_DK)
            kv = slice(2 * GLA_KW + hh * GLA_DV, 2 * GLA_KW + (hh + 1) * GLA_DV)
            kr = slice(2 * GLA_KW + GLA_VW + hh * GLA_DV, 2 * GLA_KW + GLA_VW + (hh + 1) * GLA_DV)
            q = h_ref[rows, kq]
            k = h_ref[rows, kk]
            v = h_ref[rows, kv]
            b = _cumsum_rows(la_ref[rows, kq], seg)
            bl = _seg_last_rows(b, seg)
            q_dec = q * jnp.exp(b)
            k_inv = k * jnp.exp(-b)
            k_end = k * jnp.exp(bl - b)
            att = jnp.where(incl, _dot_nt(_bf(q_dec), _bf(k_inv)), 0.0)
            o = _dot(_bf(att), _bf(v))
            o_inter = []
            for s in range(n_seg):
                sq = n * n_seg + s
                sr = slice(s * seg, (s + 1) * seg)
                st = get_state(sq, hh)
                o_inter.append(_dot(_bf(q_dec[sr]), _bf(st)))
                e_row = jnp.exp(bl[s * seg:s * seg + 1, :])
                ecol = jnp.transpose(jnp.broadcast_to(e_row, (GLA_DK, GLA_DK)))
                ecol = jnp.concatenate([ecol] * (GLA_DV // GLA_DK), axis=1)
                put_state(sq, hh, st * ecol + _dot_tn(_bf(k_end[sr]), _bf(v[sr])))
            o = o + (o_inter[0] if n_seg == 1 else jnp.concatenate(o_inter, axis=0))
            o = o * lax.rsqrt(jnp.mean(o * o, axis=-1, keepdims=True) + NORM_EPS) * ng
            og_ref[rows, hh * GLA_DV:(hh + 1) * GLA_DV] = o * _silu(h_ref[rows, kr])
        return carry

    lax.fori_loop(0, tc // c, chunk, 0)

    if geo.long:
        @pl.when(geo.last_time_block())
        def _():
            sout_ref[0, 0] = s_scr[...]


def _gla_recur(h, la, norm_g, state_in, layer, n_layers, prev_out, geo):
    tail = (GLA_HEADS, GLA_DK, GLA_DV)
    ins = [h, la, norm_g]
    in_specs = [pl.BlockSpec((geo.tc, GLA_MAIN), geo.row_map), pl.BlockSpec((geo.tc, GLA_KW), geo.row_map),
                _const_spec(norm_g)]
    if not geo.long:
        ins.append(state_in)
        in_specs.append(geo.state_spec(layer, tail))
    aliases = {}
    if prev_out is not None:
        aliases = {len(ins): 1}
        ins.append(prev_out)
        in_specs.append(pl.BlockSpec(memory_space=pl.ANY))
    scratch = [pltpu.VMEM(tail, F32)] if geo.long else []
    og, s_out = pl.pallas_call(
        functools.partial(_gla_recur_body, geo=geo, has_alias=prev_out is not None),
        grid=geo.grid, in_specs=in_specs,
        out_specs=[pl.BlockSpec((geo.tc, GLA_VW), geo.row_map), geo.state_spec(layer, tail)],
        out_shape=[jax.ShapeDtypeStruct((geo.n_b * geo.n_l, GLA_VW), F32),
                   jax.ShapeDtypeStruct((n_layers, geo.n_b) + tail, F32)],
        scratch_shapes=scratch, input_output_aliases=aliases,
        compiler_params=_params(geo.sem), name="gla_recur",
    )(*ins)
    return og, s_out


def _block_rows(x, mask):
    return jnp.where(mask, jnp.concatenate([x, x], axis=0), 0.0)


def _mm3_split(lhs, rhs_hi, rhs_lo):
    lh, ll = _split2_bf(lhs)
    return _dot(lh, rhs_hi) + (_dot(lh, rhs_lo) + _dot(ll, rhs_hi))


def _mm3_pair(lhs, x_pair, mask):
    xh, xl = _split2_bf(x_pair)
    return _mm3_split(lhs, _bf(_block_rows(xh.astype(F32), mask)), _bf(_block_rows(xl.astype(F32), mask)))


def _gdn_expand_matrix():
    h = jnp.arange(LANES)[:, None]
    e64 = (jnp.arange(GDN_HEADS * CHUNK)[None, :] // CHUNK == h)
    e128 = (jnp.arange(GDN_HEADS * GDN_DK)[None, :] // GDN_DK == h)
    return jnp.concatenate([e64, e128], axis=1).astype(BF16)


def _gdn_recur_body(*refs, geo):
    refs = list(refs)
    h_ref, ab_ref, wc_ref, alog_ref, dtb_ref, ng_ref, eall_ref = refs[:7]
    pos = 7
    conv0_ref = s0_ref = None
    if not geo.long:
        conv0_ref, s0_ref = refs[pos:pos + 2]
        pos += 2
    og_ref, convn_ref, sout_ref = refs[pos:pos + 3]
    pos += 3
    ext_scr, qkv_scr, g_scr, beta_scr, u_scr, w_scr, att_scr, egc_scr, kend_scr = refs[pos:pos + 9]
    s_scr = refs[pos + 9] if geo.long else None
    c, seg, tc = CHUNK, geo.seg, geo.tc
    n_seg = c // seg
    get_state, put_state = _state_access(geo, s0_ref, sout_ref, s_scr)
    pad = SUBLANES
    lo = pad - (GDN_CONV - 1)
    n_pair = GDN_HEADS // 2
    pw = 2 * c
    hw = GDN_HEADS * c

    if geo.long:
        @pl.when(pl.program_id(1) == 0)
        def _():
            ext_scr[0:pad, :] = jnp.zeros((pad, GDN_CONV_CH), F32)
            s_scr[...] = jnp.zeros_like(s_scr)
    else:
        ext_scr[0:pad, :] = jnp.zeros((pad, GDN_CONV_CH), F32)

    ext_scr[pad:pad + tc, :] = h_ref[:, 0:GDN_CONV_CH]
    if geo.long:
        conv = ext_scr[lo:lo + tc, :] * wc_ref[0:1, :]
        for j in range(1, GDN_CONV):
            conv = conv + ext_scr[lo + j:lo + j + tc, :] * wc_ref[j:j + 1, :]
        qkv_scr[...] = _silu(conv)

        @pl.when(geo.last_time_block())
        def _():
            convn_ref[0, 0] = ext_scr[tc + lo:tc + pad, :]

        ext_scr[0:pad, :] = ext_scr[tc:tc + pad, :]
    else:
        t_in = lax.broadcasted_iota(jnp.int32, (tc, GDN_CONV_CH), 0) & (seg - 1)
        conv = ext_scr[pad:pad + tc, :] * wc_ref[GDN_CONV - 1:GDN_CONV, :]
        for j in range(GDN_CONV - 1):
            back = GDN_CONV - 1 - j
            conv = conv + jnp.where(t_in >= back, ext_scr[lo + j:lo + j + tc, :], 0.0) * wc_ref[j:j + 1, :]
        qkv_scr[...] = conv
        for sq in range(geo.seqs):
            c0 = conv0_ref[0, sq]
            head = []
            for t in range(GDN_CONV - 1):
                acc = c0[t:t + 1, :] * wc_ref[0:1, :]
                for j in range(1, GDN_CONV - 1 - t):
                    acc = acc + c0[t + j:t + j + 1, :] * wc_ref[j:j + 1, :]
                head.append(acc)
            head.append(jnp.zeros((SUBLANES - (GDN_CONV - 1), GDN_CONV_CH), F32))
            first = slice(sq * seg, sq * seg + SUBLANES)
            qkv_scr[first, :] = qkv_scr[first, :] + jnp.concatenate(head, axis=0)
            convn_ref[0, sq] = h_ref[(sq + 1) * seg - (GDN_CONV - 1):(sq + 1) * seg, 0:GDN_CONV_CH]
        qkv_scr[...] = _silu(qkv_scr[...])

    g_scr[...] = -jnp.exp(alog_ref[...]) * _softplus(ab_ref[:, 0:LANES] + dtb_ref[...])
    beta_scr[...] = _sigmoid(ab_ref[:, LANES:2 * LANES])

    incl, strict, delta = _seg_masks((c, hw), seg)
    eye_pair = jnp.where(delta[:, 0:pw], 1.0, 0.0)
    r2 = lax.broadcasted_iota(jnp.int32, (2 * c, pw), 0) >> 6
    mask_cc = r2 == (lax.broadcasted_iota(jnp.int32, (2 * c, pw), 1) >> 6)
    r3 = lax.broadcasted_iota(jnp.int32, (2 * c, 2 * GDN_DK), 0) >> 6
    mask_cd = r3 == (lax.broadcasted_iota(jnp.int32, (2 * c, 2 * GDN_DK), 1) >> 7)
    ones_c = jnp.ones((c, c), BF16)
    ng = ng_ref[...]

    def local(n, carry):
        rows = pl.ds(pl.multiple_of(n * c, c), c)
        gc = _cumsum_rows(g_scr[rows, :], seg)
        e_all = eall_ref[...]
        sh, sm, sl = _split3_bf(jnp.concatenate([gc, beta_scr[rows, :]], axis=0))
        ex = _dot(sh, e_all) + (_dot(sm, e_all) + _dot(sl, e_all))
        gcol, bcol, gwide = ex[0:c, 0:hw], ex[c:2 * c, 0:hw], ex[0:c, hw:]
        th, tm, tl = _split3_bf(jnp.concatenate(
            [jnp.where(delta, gcol, 0.0), jnp.where(delta, bcol, 0.0)], axis=1))
        rowf = _dot(ones_c, th) + (_dot(ones_c, tm) + _dot(ones_c, tl))
        grow, brow = rowf[:, 0:hw], rowf[:, hw:]
        dec = jnp.where(incl, jnp.exp(jnp.where(incl, gcol - grow, 0.0)), 0.0)
        egrow = jnp.exp(grow)
        egc_scr[rows, :] = jnp.exp(gwide)
        eend = jnp.exp(_seg_last_rows(gwide, seg) - gwide)

        a_pairs = []
        for p in range(n_pair):
            qs, ks = [], []
            for hh in (2 * p, 2 * p + 1):
                kq = slice(hh * GDN_DK, (hh + 1) * GDN_DK)
                kk = slice(GDN_KW + hh * GDN_DK, GDN_KW + (hh + 1) * GDN_DK)
                q = qkv_scr[rows, kq]
                k = qkv_scr[rows, kk]
                q = q * lax.rsqrt(jnp.sum(q * q, axis=-1, keepdims=True) + NORM_EPS) * GDN_DK ** -0.5
                k = k * lax.rsqrt(jnp.sum(k * k, axis=-1, keepdims=True) + NORM_EPS)
                qkv_scr[rows, kq] = q
                qkv_scr[rows, kk] = k
                kend_scr[rows, kq] = k * eend[:, kq]
                qs.append(q)
                ks.append(k)
            kh = _bf(jnp.concatenate(ks, axis=1))
            kbd_h = _bf(_block_rows(kh.astype(F32), mask_cd))
            kkqk = _dot_nt(jnp.concatenate([kh, _bf(jnp.concatenate(qs, axis=1))], axis=0), kbd_h)
            cs = slice(p * pw, (p + 1) * pw)
            a_pairs.append(jnp.where(strict[:, cs], kkqk[0:c] * dec[:, cs] * bcol[:, cs], 0.0))
            att_scr[rows, cs] = kkqk[c:2 * c] * dec[:, cs]
        ps = [eye_pair - a for a in a_pairs]
        xs = [_mm3_pair(a, a, mask_cc) for a in a_pairs]
        steps = int(math.log2(seg)) - 1
        for s in range(steps):
            if s < steps - 1:
                rs = [_mm3_pair(jnp.concatenate([pp, x], axis=0), x, mask_cc) for pp, x in zip(ps, xs)]
                ps = [pp + r[0:c] for pp, r in zip(ps, rs)]
                xs = [r[c:2 * c] for r in rs]
            else:
                ps = [pp + _mm3_pair(pp, x, mask_cc) for pp, x in zip(ps, xs)]
        for p in range(n_pair):
            cs = slice(p * pw, (p + 1) * pw)
            vs = slice(p * 2 * GDN_DV, (p + 1) * 2 * GDN_DV)
            t_b = ps[p] * brow[:, cs]
            vh, vl = _split2_bf(qkv_scr[rows, 2 * GDN_KW + p * 2 * GDN_DV:2 * GDN_KW + (p + 1) * 2 * GDN_DV])
            u_scr[rows, vs] = _mm3_split(t_b, _bf(_block_rows(vh.astype(F32), mask_cd)),
                                         _bf(_block_rows(vl.astype(F32), mask_cd)))
            kh, kl = _split2_bf(qkv_scr[rows, GDN_KW + p * 2 * GDN_DK:GDN_KW + (p + 1) * 2 * GDN_DK])
            w_scr[rows, vs] = _mm3_split(t_b * egrow[:, cs], _bf(_block_rows(kh.astype(F32), mask_cd)),
                                         _bf(_block_rows(kl.astype(F32), mask_cd)))
        return carry

    lax.fori_loop(0, tc // c, local, 0)

    def recur(n, carry):
        r0 = pl.multiple_of(n * c, c)
        rows = pl.ds(r0, c)
        for p in range(n_pair):
            vns, qss = [], []
            for hh in (2 * p, 2 * p + 1):
                kq = slice(hh * GDN_DK, (hh + 1) * GDN_DK)
                vn_seg, qs_seg = [], []
                for s in range(n_seg):
                    sq = n * n_seg + s
                    sr = pl.ds(pl.multiple_of(r0 + s * seg, SUBLANES), seg)
                    tail = pl.ds(pl.multiple_of(r0 + (s + 1) * seg - SUBLANES, SUBLANES), SUBLANES)
                    st = get_state(sq, hh)
                    m1 = _dot(_bf(jnp.concatenate([w_scr[sr, kq], qkv_scr[sr, kq]], axis=0)), _bf(st))
                    vn = u_scr[sr, kq] - m1[0:seg]
                    e_last = egc_scr[tail, kq][SUBLANES - 1:SUBLANES, :]
                    put_state(sq, hh, st * e_last + _dot_tn(_bf(kend_scr[sr, kq]), _bf(vn)))
                    vn_seg.append(vn)
                    qs_seg.append(m1[seg:2 * seg])
                vns.append(vn_seg[0] if n_seg == 1 else jnp.concatenate(vn_seg, axis=0))
                qss.append(qs_seg[0] if n_seg == 1 else jnp.concatenate(qs_seg, axis=0))
            vs = slice(p * 2 * GDN_DV, (p + 1) * 2 * GDN_DV)
            vbd = _bf(_block_rows(jnp.concatenate(vns, axis=1), mask_cd))
            o_pair = (_dot(_bf(att_scr[rows, p * pw:(p + 1) * pw]), vbd)
                      + egc_scr[rows, vs] * jnp.concatenate(qss, axis=1))
            for i, hh in enumerate((2 * p, 2 * p + 1)):
                o = o_pair[:, i * GDN_DV:(i + 1) * GDN_DV]
                o = o * lax.rsqrt(jnp.mean(o * o, axis=-1, keepdims=True) + NORM_EPS) * ng
                kz = slice(GDN_CONV_CH + hh * GDN_DV, GDN_CONV_CH + (hh + 1) * GDN_DV)
                og_ref[rows, hh * GDN_DV:(hh + 1) * GDN_DV] = o * _silu(h_ref[rows, kz])
        return carry

    lax.fori_loop(0, tc // c, recur, 0)

    if geo.long:
        @pl.when(geo.last_time_block())
        def _():
            sout_ref[0, 0] = s_scr[...]


def _gdn_recur(h, ab, w_conv, a_log, dt_bias, norm_g, conv_in, state_in, layer, n_layers, geo):
    tail = (GDN_HEADS, GDN_DK, GDN_DV)
    cv_tail = (GDN_CONV - 1, GDN_CONV_CH)
    consts = [w_conv, a_log, dt_bias, norm_g, _gdn_expand_matrix()]
    ins = [h, ab] + consts
    in_specs = [pl.BlockSpec((geo.tc, GDN_MAIN), geo.row_map), pl.BlockSpec((geo.tc, 2 * LANES), geo.row_map)]
    in_specs += [_const_spec(a) for a in consts]
    if not geo.long:
        ins += [conv_in, state_in]
        in_specs += [geo.state_spec(layer, cv_tail), geo.state_spec(layer, tail)]
    tc = geo.tc
    scratch = [pltpu.VMEM((tc + SUBLANES, GDN_CONV_CH), F32), pltpu.VMEM((tc, GDN_CONV_CH), F32),
               pltpu.VMEM((tc, LANES), F32), pltpu.VMEM((tc, LANES), F32),
               pltpu.VMEM((tc, GDN_VW), F32), pltpu.VMEM((tc, GDN_KW), F32),
               pltpu.VMEM((tc, GDN_HEADS * CHUNK), F32), pltpu.VMEM((tc, GDN_KW), F32),
               pltpu.VMEM((tc, GDN_KW), F32)]
    if geo.long:
        scratch.append(pltpu.VMEM(tail, F32))
    return pl.pallas_call(
        functools.partial(_gdn_recur_body, geo=geo),
        grid=geo.grid, in_specs=in_specs,
        out_specs=[pl.BlockSpec((tc, GDN_VW), geo.row_map), geo.state_spec(layer, cv_tail),
                   geo.state_spec(layer, tail)],
        out_shape=[jax.ShapeDtypeStruct((geo.n_b * geo.n_l, GDN_VW), F32),
                   jax.ShapeDtypeStruct((n_layers, geo.n_b) + cv_tail, F32),
                   jax.ShapeDtypeStruct((n_layers, geo.n_b) + tail, F32)],
        scratch_shapes=scratch,
        compiler_params=_params(geo.sem), name="gdn_recur",
    )(*ins)


def _s5_body(*refs, tt, nb, has_init):
    if has_init:
        (x_ref, win_ref, wb_ref, are_ref, aim_ref, wc_ref, d_ref, wglu_ref, bglu_ref, wout_ref,
         lng_ref, lnb_ref, h0re_ref, h0im_ref, y_ref, hre_ref, him_ref,
         x_scr, u_scr, z_scr, bu_scr, y_scr, st_scr) = refs
    else:
        (x_ref, win_ref, wb_ref, are_ref, aim_ref, wc_ref, d_ref, wglu_ref, bglu_ref, wout_ref,
         lng_ref, lnb_ref, y_ref, hre_ref, him_ref,
         x_scr, u_scr, z_scr, bu_scr, y_scr, st_scr) = refs
    tb = pl.program_id(1)
    ns = S5_KT_STATES

    @pl.when(tb == 0)
    def _():
        if not has_init:
            st_scr[...] = jnp.zeros_like(st_scr)
        else:
            for kt in range(S5_KT):
                st_scr[:, 2 * kt * ns:(2 * kt + 1) * ns] = h0re_ref[0, :, kt * ns:(kt + 1) * ns]
                st_scr[:, (2 * kt + 1) * ns:(2 * kt + 2) * ns] = h0im_ref[0, :, kt * ns:(kt + 1) * ns]

    x_scr[...] = jnp.swapaxes(x_ref[...], 0, 1).reshape(tt * nb, D_MODEL)
    h = _dot(_bf(x_scr[...]), win_ref[...])
    u_scr[...] = h[:, 0:S5_WIDTH]
    z_scr[...] = h[:, S5_WIDTH:]

    for kt in range(S5_KT):
        cols = slice(kt * S5_KT_W, (kt + 1) * S5_KT_W)
        bu_scr[...] = _dot(_bf(u_scr[:, cols]), wb_ref[kt])
        a_re = jnp.broadcast_to(are_ref[kt], (SUBLANES, ns))
        a_im = jnp.broadcast_to(aim_ref[kt], (SUBLANES, ns))
        base = kt * 2 * ns

        def row_tile(rb, carry):
            r_off = pl.multiple_of(rb * SUBLANES, SUBLANES)
            st_rows = pl.ds(r_off, SUBLANES)
            h_re0 = st_scr[st_rows, base:base + ns]
            h_im0 = st_scr[st_rows, base + ns:base + 2 * ns]

            def step(t, hc):
                h_re, h_im = hc
                rows = pl.ds(pl.multiple_of(t * nb + r_off, SUBLANES), SUBLANES)
                n_re = a_re * h_re - a_im * h_im + bu_scr[rows, 0:ns]
                n_im = a_re * h_im + a_im * h_re + bu_scr[rows, ns:2 * ns]
                bu_scr[rows, 0:ns] = n_re
                bu_scr[rows, ns:2 * ns] = n_im
                return n_re, n_im

            h_re, h_im = lax.fori_loop(0, tt, step, (h_re0, h_im0))
            st_scr[st_rows, base:base + ns] = h_re
            st_scr[st_rows, base + ns:base + 2 * ns] = h_im
            return carry

        lax.fori_loop(0, nb // SUBLANES, row_tile, 0)
        y_scr[:, cols] = _dot(_bf(bu_scr[...]), wc_ref[kt]) + d_ref[:, cols] * u_scr[:, cols]

    y = _gelu_tanh(y_scr[...])
    yg = _dot(_bf(y), wglu_ref[...]) + bglu_ref[...]
    y = yg[:, 0:S5_WIDTH] * _sigmoid(yg[:, S5_WIDTH:]) * _silu(z_scr[...])
    out = _dot(_bf(y), wout_ref[...])
    y = _layer_norm(ALPHA_RES * x_scr[...] + out, lng_ref[...], lnb_ref[...])
    y_ref[...] = jnp.swapaxes(y.reshape(tt, nb, D_MODEL), 0, 1)

    @pl.when(tb == pl.num_programs(1) - 1)
    def _():
        for kt in range(S5_KT):
            hre_ref[0, :, kt * ns:(kt + 1) * ns] = st_scr[:, 2 * kt * ns:(2 * kt + 1) * ns]
            him_ref[0, :, kt * ns:(kt + 1) * ns] = st_scr[:, (2 * kt + 1) * ns:(2 * kt + 2) * ns]


def _s5_discretize(lam_re, lam_im, log_dt, b_re, b_im, c_re, c_im):
    dt = jnp.exp(log_dt)[:, None]
    mag = jnp.exp(lam_re * dt)
    ab_re, ab_im = mag * jnp.cos(lam_im * dt), mag * jnp.sin(lam_im * dt)
    den = jnp.square(lam_re) + jnp.square(lam_im)
    num_re = ab_re - 1.0
    coef_re = (num_re * lam_re + ab_im * lam_im) / den
    coef_im = (ab_im * lam_re - num_re * lam_im) / den
    bb_re = coef_re[..., None] * b_re - coef_im[..., None] * b_im
    bb_im = coef_re[..., None] * b_im + coef_im[..., None] * b_re
    gl = S5_GROUPS // S5_KT
    eye = jnp.eye(gl, dtype=F32)

    def block_b(bb):
        t = bb.reshape(S5_KT, gl, S5_STATE, S5_GROUP)
        return jnp.einsum("kgpc,gh->kgchp", t, eye).reshape(S5_KT, S5_KT_W, S5_KT_STATES)

    def block_c(cc):
        t = cc.reshape(S5_KT, gl, S5_GROUP, S5_STATE)
        return jnp.einsum("kgcp,gh->kgphc", t, eye).reshape(S5_KT, S5_KT_STATES, S5_KT_W)

    w_b = jnp.concatenate([block_b(bb_re), block_b(bb_im)], axis=2).astype(BF16)
    w_c = jnp.concatenate([block_c(c_re), -block_c(c_im)], axis=1).astype(BF16)
    a_re = ab_re.reshape(S5_KT, 1, S5_KT_STATES)
    a_im = ab_im.reshape(S5_KT, 1, S5_KT_STATES)
    return w_b, w_c, a_re, a_im


def _s5_layer(x, prep, w_in, d_vec, w_glu, b_glu, w_out, ln_g, ln_b, h0, n_b, n_l):
    w_b, w_c, a_re, a_im = prep
    if n_l >= S5_LONG_STEPS:
        nb, tt = n_b, S5_LONG_STEPS
    else:
        nb, tt = min(S5_SHORT_SEQS, n_b), n_l
    assert nb % SUBLANES == 0 and tt % SUBLANES == 0
    rows = tt * nb
    has_init = h0 is not None
    consts = [w_in, w_b, a_re, a_im, w_c, d_vec, w_glu, b_glu, w_out, ln_g, ln_b]
    x_spec = pl.BlockSpec((nb, tt, D_MODEL), lambda b, t: (b, t, 0))
    st_spec = pl.BlockSpec((1, nb, S5_NSTATE), lambda b, t: (0, b, 0))
    ins = [x.reshape(n_b, n_l, D_MODEL)] + consts
    in_specs = [x_spec] + [_const_spec(a) for a in consts]
    if has_init:
        ins += list(h0)
        in_specs += [st_spec, st_spec]
    y, h_re, h_im = pl.pallas_call(
        functools.partial(_s5_body, tt=tt, nb=nb, has_init=has_init),
        grid=(n_b // nb, n_l // tt), in_specs=in_specs,
        out_specs=[x_spec, st_spec, st_spec],
        out_shape=[jax.ShapeDtypeStruct((n_b, n_l, D_MODEL), F32),
                   jax.ShapeDtypeStruct((1, n_b, S5_NSTATE), F32),
                   jax.ShapeDtypeStruct((1, n_b, S5_NSTATE), F32)],
        scratch_shapes=[pltpu.VMEM((rows, D_MODEL), F32),
                        pltpu.VMEM((rows, S5_WIDTH), F32), pltpu.VMEM((rows, S5_WIDTH), F32),
                        pltpu.VMEM((rows, 2 * S5_KT_STATES), F32), pltpu.VMEM((rows, S5_WIDTH), F32),
                        pltpu.VMEM((nb, 2 * S5_NSTATE), F32)],
        compiler_params=_params(("parallel", "arbitrary")), name="s5_layer",
    )(*ins)
    return y.reshape(n_b * n_l, D_MODEL), h_re, h_im


def _row2(v):
    return v.reshape(1, -1).astype(F32)


def _pad_cols(w, n):
    return jnp.concatenate([w, jnp.zeros((w.shape[0], n - w.shape[1]), w.dtype)], axis=1)


def _trunk(x, n_b, n_l, states, wts):
    geo = _Geometry(n_b, n_l)
    n_gla, n_gdn, n_s5 = (DEPTH + 2) // 3, (DEPTH + 1) // 3, DEPTH // 3
    assert n_gdn == 1 and n_s5 == 1
    s_gla = s_gdn = s_conv = s_re = s_im = None
    for i in range(DEPTH):
        j, kind = divmod(i, 3)
        ln_g, ln_b = _row2(wts["ln_g"][i]), _row2(wts["ln_b"][i])
        if kind == 0:
            w_in = wts["gla_w_in"][j]
            w_main = _bf(w_in[:, :GLA_MAIN])
            w_lr = _bf(_pad_cols(w_in[:, GLA_MAIN:], LANES))
            w_a2 = _bf(jnp.concatenate(
                [wts["gla_w_a2"][j], jnp.zeros((LANES - GLA_LOWRANK, GLA_KW), F32)], axis=0))
            h, la = _rowwise_call(_gla_proj_body, "gla_proj", [x],
                                  [w_main, w_lr, w_a2, _row2(wts["gla_b_a"][j])], [GLA_MAIN, GLA_KW])
            og, s_gla = _gla_recur(h, la, _row2(wts["gla_norm_g"][j]),
                                   None if states is None else states[0], j, n_gla, s_gla, geo)
            w_out = _bf(wts["gla_w_out"][j])
        elif kind == 1:
            w_in = wts["gdn_w_in"][j]
            w_main = _bf(w_in[:, :GDN_MAIN])
            w_ab = _bf(jnp.concatenate(
                [_pad_cols(w_in[:, GDN_MAIN:GDN_MAIN + GDN_HEADS], LANES),
                 _pad_cols(w_in[:, GDN_MAIN + GDN_HEADS:], LANES)], axis=1))
            h, ab = _rowwise_call(_gdn_proj_body, "gdn_proj", [x], [w_main, w_ab], [GDN_MAIN, 2 * LANES])
            og, s_conv, s_gdn = _gdn_recur(
                h, ab, wts["gdn_w_conv"][j].astype(F32),
                _pad_cols(_row2(wts["gdn_a_log"][j]), LANES), _pad_cols(_row2(wts["gdn_dt_bias"][j]), LANES),
                _row2(wts["gdn_norm_g"][j]),
                None if states is None else states[2], None if states is None else states[1],
                j, n_gdn, geo)
            w_out = _bf(wts["gdn_w_out"][j])
        else:
            prep = _s5_discretize(wts["s5_lam_re"][j].astype(F32), wts["s5_lam_im"][j].astype(F32),
                                  wts["s5_log_dt"][j].astype(F32), wts["s5_b_re"][j].astype(F32),
                                  wts["s5_b_im"][j].astype(F32), wts["s5_c_re"][j].astype(F32),
                                  wts["s5_c_im"][j].astype(F32))
            h0 = None if states is None else (states[3].reshape(n_s5, n_b, S5_NSTATE),
                                              states[4].reshape(n_s5, n_b, S5_NSTATE))
            x, h_re, h_im = _s5_layer(x, prep, _bf(wts["s5_w_in"][j]), _row2(wts["s5_d"][j]),
                                      _bf(wts["s5_w_glu"][j]), _row2(wts["s5_b_glu"][j]),
                                      _bf(wts["s5_w_out"][j]), ln_g, ln_b, h0, n_b, n_l)
            s_re = h_re.reshape(n_s5, n_b, S5_GROUPS, S5_STATE)
            s_im = h_im.reshape(n_s5, n_b, S5_GROUPS, S5_STATE)
            continue
        (x,) = _rowwise_call(_out_ln_body, "out_ln", [og, x], [w_out, ln_g, ln_b], [D_MODEL])
    return x, s_gla, s_gdn, s_conv, s_re, s_im


def kernel(x_prompt, x_sample, state_gla, state_gdn, state_gdn_conv, state_s5_re, state_s5_im,
           ln_g, ln_b, gla_w_in, gla_w_a2, gla_b_a, gla_norm_g, gla_w_out,
           gdn_w_in, gdn_w_conv, gdn_a_log, gdn_dt_bias, gdn_norm_g, gdn_w_out,
           s5_w_in, s5_lam_re, s5_lam_im, s5_log_dt, s5_b_re, s5_b_im, s5_c_re, s5_c_im,
           s5_d, s5_w_glu, s5_b_glu, s5_w_out):
    wts = dict(ln_g=ln_g, ln_b=ln_b,
               gla_w_in=gla_w_in, gla_w_a2=gla_w_a2, gla_b_a=gla_b_a, gla_norm_g=gla_norm_g,
               gla_w_out=gla_w_out,
               gdn_w_in=gdn_w_in, gdn_w_conv=gdn_w_conv, gdn_a_log=gdn_a_log, gdn_dt_bias=gdn_dt_bias,
               gdn_norm_g=gdn_norm_g, gdn_w_out=gdn_w_out,
               s5_w_in=s5_w_in, s5_lam_re=s5_lam_re, s5_lam_im=s5_lam_im, s5_log_dt=s5_log_dt,
               s5_b_re=s5_b_re, s5_b_im=s5_b_im, s5_c_re=s5_c_re, s5_c_im=s5_c_im, s5_d=s5_d,
               s5_w_glu=s5_w_glu, s5_b_glu=s5_b_glu, s5_w_out=s5_w_out)

    bp, lp, _ = x_prompt.shape
    yp, p_gla, p_gdn, p_conv, p_re, p_im = _trunk(
        x_prompt.reshape(bp * lp, D_MODEL), bp, lp, None, wts)
    bs, ls, _ = x_sample.shape
    ys, s_gla, s_gdn, s_conv, s_re, s_im = _trunk(
        x_sample.reshape(bs * ls, D_MODEL), bs, ls,
        (state_gla, state_gdn, state_gdn_conv, state_s5_re, state_s5_im), wts)
    return (yp.reshape(bp, lp, D_MODEL), ys.reshape(bs, ls, D_MODEL),
            p_gla, p_gdn, p_conv, p_re, p_im, s_gla, s_gdn, s_conv, s_re, s_im)
```

```python
import functools
import math

import jax
import jax.numpy as jnp
from jax import lax
from jax.experimental import pallas as pl
from jax.experimental.pallas import tpu as pltpu

F32 = jnp.float32
BF16 = jnp.bfloat16

D_MODEL = 1024
DEPTH = 4
ALPHA_RES = (2 * DEPTH) ** 0.25
LN_EPS = 1e-5
NORM_EPS = 1e-6
CHUNK = 64

GLA_HEADS = 4
GLA_KW = 512
GLA_VW = 1024
GLA_DK = 128
GLA_DV = 256
GLA_LOWRANK = 16
GLA_TAU = 16.0
GLA_MAIN = 2 * GLA_KW + 2 * GLA_VW

GDN_HEADS = 8
GDN_DK = 128
GDN_DV = 128
GDN_KW = 1024
GDN_VW = 1024
GDN_CONV = 4
GDN_CONV_CH = 3072
GDN_MAIN = GDN_CONV_CH + GDN_VW

S5_WIDTH = 1024
S5_GROUP = 16
S5_GROUPS = 64
S5_STATE = 64
S5_KT = 4
S5_KT_W = S5_WIDTH // S5_KT
S5_KT_STATES = (S5_GROUPS // S5_KT) * S5_STATE
S5_NSTATE = S5_GROUPS * S5_STATE

LANES = 128
SUBLANES = 8
VMEM_LIMIT = 56 * 1024 * 1024
ROW_BLOCK = 512
LONG_TIME_BLOCK = 256
S5_LONG_STEPS = 64
S5_SHORT_SEQS = 64


def _bf(x):
    return x.astype(BF16)


def _dot(a, b):
    return jnp.dot(a, b, preferred_element_type=F32)


def _dot_nt(a, b):
    return lax.dot_general(a, b, (((1,), (1,)), ((), ())), preferred_element_type=F32)


def _dot_tn(a, b):
    return lax.dot_general(a, b, (((0,), (0,)), ((), ())), preferred_element_type=F32)


def _split2_bf(x):
    h = x.astype(BF16)
    return h, (x - h.astype(F32)).astype(BF16)


def _split3_bf(x):
    h = x.astype(BF16)
    r = x - h.astype(F32)
    m = r.astype(BF16)
    return h, m, (r - m.astype(F32)).astype(BF16)


def _sigmoid(x):
    return 1.0 / (1.0 + jnp.exp(-x))


def _silu(x):
    return x * _sigmoid(x)


def _softplus(x):
    return jnp.maximum(x, 0.0) + jnp.log(1.0 + jnp.exp(-jnp.abs(x)))


def _gelu_tanh(x):
    return 0.5 * x * (1.0 + jnp.tanh(math.sqrt(2.0 / math.pi) * (x + 0.044715 * (x * x * x))))


def _layer_norm(x, g, b):
    mu = jnp.mean(x, axis=-1, keepdims=True)
    xc = x - mu
    var = jnp.mean(xc * xc, axis=-1, keepdims=True)
    return xc * lax.rsqrt(var + LN_EPS) * g + b


def _cumsum_rows(x, seg):
    row = lax.broadcasted_iota(jnp.int32, x.shape, 0) & (seg - 1)
    s = 1
    while s < seg:
        x = x + jnp.where(row >= s, pltpu.roll(x, s, axis=0), 0.0)
        s *= 2
    return x


def _seg_last_rows(x, seg):
    c, w = x.shape
    if seg == c:
        return jnp.broadcast_to(x[c - 1:c, :], (c, w))
    x3 = x.reshape(c // seg, seg, w)
    return jnp.broadcast_to(x3[:, seg - 1:seg, :], x3.shape).reshape(c, w)


def _seg_masks(shape, seg):
    ri = lax.broadcasted_iota(jnp.int32, shape, 0)
    li = lax.broadcasted_iota(jnp.int32, shape, 1) & (CHUNK - 1)
    shift = int(math.log2(seg))
    same = (ri >> shift) == (li >> shift)
    return same & (ri >= li), same & (ri > li), ri == li


def _const_spec(arr):
    nd = arr.ndim
    return pl.BlockSpec(arr.shape, lambda *_: (0,) * nd, pipeline_mode=pl.Buffered(1))


def _params(sem):
    return pltpu.CompilerParams(dimension_semantics=sem, vmem_limit_bytes=VMEM_LIMIT)


def _rowwise_call(body, name, rows, consts, out_widths):
    n = rows[0].shape[0]
    tm = min(ROW_BLOCK, n)
    spec = lambda w: pl.BlockSpec((tm, w), lambda i: (i, 0))
    return pl.pallas_call(
        body, grid=(n // tm,),
        in_specs=[spec(r.shape[1]) for r in rows] + [_const_spec(c) for c in consts],
        out_specs=[spec(w) for w in out_widths],
        out_shape=[jax.ShapeDtypeStruct((n, w), F32) for w in out_widths],
        compiler_params=_params(("parallel",)), name=name,
    )(*rows, *consts)


def _gla_proj_body(x_ref, w_ref, wlr_ref, wa2_ref, ba_ref, h_ref, la_ref):
    xb = _bf(x_ref[...])
    h = _dot(xb, w_ref[...])
    h_ref[:, :GLA_KW] = h[:, :GLA_KW] * GLA_DK ** -0.5
    h_ref[:, GLA_KW:] = h[:, GLA_KW:]
    lr = _dot(xb, wlr_ref[...])
    pre = _dot(_bf(lr), wa2_ref[...]) + ba_ref[...]
    la_ref[...] = -_softplus(-pre) * (1.0 / GLA_TAU)


def _gdn_proj_body(x_ref, w_ref, wab_ref, h_ref, ab_ref):
    xb = _bf(x_ref[...])
    h_ref[...] = _dot(xb, w_ref[...])
    ab_ref[...] = _dot(xb, wab_ref[...])


def _out_ln_body(og_ref, x_ref, w_ref, g_ref, b_ref, y_ref):
    out = _dot(_bf(og_ref[...]), w_ref[...])
    y_ref[...] = _layer_norm(ALPHA_RES * x_ref[...] + out, g_ref[...], b_ref[...])


class _Geometry:
    def __init__(self, n_b, n_l):
        self.n_b, self.n_l = n_b, n_l
        self.long = n_l >= CHUNK
        if self.long:
            assert n_l % LONG_TIME_BLOCK == 0
            self.seg = CHUNK
            self.tc = LONG_TIME_BLOCK
            self.n_t = n_l // self.tc
            self.grid = (n_b, self.n_t)
            self.sem = ("parallel", "arbitrary")
            self.seqs = 1
            self.row_map = lambda b, t: (b * self.n_t + t, 0)
            self.seq_block = lambda b, t: b
        else:
            assert CHUNK % n_l == 0 and n_l % SUBLANES == 0 and (n_b * n_l) % CHUNK == 0
            self.seg = n_l
            self.tc = CHUNK
            self.grid = (n_b * n_l // self.tc,)
            self.sem = ("parallel",)
            self.seqs = self.tc // n_l
            self.row_map = lambda i: (i, 0)
            self.seq_block = lambda i: i

    def state_spec(self, layer, tail, n_layers=1):
        zeros = (0,) * len(tail)
        first = layer if n_layers == 1 else 0
        return pl.BlockSpec((n_layers, self.seqs) + tail, lambda *g: (first, self.seq_block(*g)) + zeros)

    def last_time_block(self):
        return pl.program_id(1) == pl.num_programs(1) - 1 if self.long else None


def _state_access(geo, s0_ref, sout_ref, s_scr, out_layer=0):
    if geo.long:
        return (lambda sq, hh: s_scr[hh]), (lambda sq, hh, v: s_scr.__setitem__(hh, v))
    return ((lambda sq, hh: s0_ref[0, sq, hh]),
            (lambda sq, hh, v: sout_ref.__setitem__((out_layer, sq, hh), v)))


def _gla_recur_body(*refs, geo, has_alias, layer, out_layers):
    refs = list(refs)
    h_ref, la_ref, ng_ref = refs[:3]
    pos = 3
    s0_ref = None
    if not geo.long:
        s0_ref = refs[pos]
        pos += 1
    if has_alias:
        pos += 1
    og_ref, sout_ref = refs[pos:pos + 2]
    s_scr = refs[pos + 2] if geo.long else None
    c, seg, tc = CHUNK, geo.seg, geo.tc
    n_seg = c // seg
    out_layer = layer if out_layers > 1 else 0
    get_state, put_state = _state_access(geo, s0_ref, sout_ref, s_scr, out_layer)

    def zero_other_layers():
        for other in range(out_layers):
            if other != out_layer:
                sout_ref[other] = jnp.zeros(sout_ref.shape[1:], F32)

    if geo.long:
        @pl.when(pl.program_id(1) == 0)
        def _():
            s_scr[...] = jnp.zeros_like(s_scr)
    else:
        zero_other_layers()

    incl, _, _ = _seg_masks((c, c), seg)
    ng = ng_ref[...]

    def chunk(n, carry):
        r0 = pl.multiple_of(n * c, c)
        rows = pl.ds(r0, c)
        heads = range(GLA_HEADS)
        b = _cumsum_rows(la_ref[rows, :], seg)
        bl = _seg_last_rows(b, seg)
        k_all = h_ref[rows, GLA_KW:2 * GLA_KW]
        q_dec = h_ref[rows, 0:GLA_KW] * jnp.exp(b)
        k_inv = k_all * jnp.exp(-b)
        k_end = k_all * jnp.exp(bl - b)
        e_bl = jnp.exp(bl)
        kq = [slice(hh * GLA_DK, (hh + 1) * GLA_DK) for hh in heads]
        vs = [h_ref[rows, 2 * GLA_KW + hh * GLA_DV:2 * GLA_KW + (hh + 1) * GLA_DV] for hh in heads]
        atts = [jnp.where(incl, _dot_nt(_bf(q_dec[:, kq[hh]]), _bf(k_inv[:, kq[hh]])), 0.0) for hh in heads]
        os_ = [_dot(_bf(atts[hh]), _bf(vs[hh])) for hh in heads]
        inter = [[] for _ in heads]
        for s in range(n_seg):
            sq = n * n_seg + s
            sr = slice(s * seg, (s + 1) * seg)
            sts = [get_state(sq, hh) for hh in heads]
            for hh in heads:
                inter[hh].append(_dot(_bf(q_dec[sr, kq[hh]]), _bf(sts[hh])))
            upd = [_dot_tn(_bf(k_end[sr, kq[hh]]), _bf(vs[hh][sr])) for hh in heads]
            for hh in heads:
                ecol = jnp.transpose(jnp.broadcast_to(e_bl[s * seg:s * seg + 1, kq[hh]], (GLA_DK, GLA_DK)))
                ecol = jnp.concatenate([ecol] * (GLA_DV // GLA_DK), axis=1)
                put_state(sq, hh, sts[hh] * ecol + upd[hh])
        for hh in heads:
            o = os_[hh] + (inter[hh][0] if n_seg == 1 else jnp.concatenate(inter[hh], axis=0))
            o = o * lax.rsqrt(jnp.mean(o * o, axis=-1, keepdims=True) + NORM_EPS) * ng
            kr = slice(2 * GLA_KW + GLA_VW + hh * GLA_DV, 2 * GLA_KW + GLA_VW + (hh + 1) * GLA_DV)
            og_ref[rows, hh * GLA_DV:(hh + 1) * GLA_DV] = o * _silu(h_ref[rows, kr])
        return carry

    lax.fori_loop(0, tc // c, chunk, 0)

    if geo.long:
        @pl.when(geo.last_time_block())
        def _():
            sout_ref[out_layer, 0] = s_scr[...]
            zero_other_layers()


def _gla_recur(h, la, norm_g, state_in, layer, n_layers, prev_out, geo):
    tail = (GLA_HEADS, GLA_DK, GLA_DV)
    ins = [h, la, norm_g]
    in_specs = [pl.BlockSpec((geo.tc, GLA_MAIN), geo.row_map), pl.BlockSpec((geo.tc, GLA_KW), geo.row_map),
                _const_spec(norm_g)]
    if not geo.long:
        ins.append(state_in)
        in_specs.append(geo.state_spec(layer, tail))
    aliases = {}
    if prev_out is not None:
        aliases = {len(ins): 1}
        ins.append(prev_out)
        in_specs.append(pl.BlockSpec(memory_space=pl.ANY))
    scratch = [pltpu.VMEM(tail, F32)] if geo.long else []
    out_layers = n_layers if prev_out is None else 1
    og, s_out = pl.pallas_call(
        functools.partial(_gla_recur_body, geo=geo, has_alias=prev_out is not None,
                          layer=layer, out_layers=out_layers),
        grid=geo.grid, in_specs=in_specs,
        out_specs=[pl.BlockSpec((geo.tc, GLA_VW), geo.row_map), geo.state_spec(layer, tail, out_layers)],
        out_shape=[jax.ShapeDtypeStruct((geo.n_b * geo.n_l, GLA_VW), F32),
                   jax.ShapeDtypeStruct((n_layers, geo.n_b) + tail, F32)],
        scratch_shapes=scratch, input_output_aliases=aliases,
        compiler_params=_params(geo.sem), name="gla_recur",
    )(*ins)
    return og, s_out


def _block_rows(x, mask):
    return jnp.where(mask, jnp.concatenate([x, x], axis=0), 0.0)


def _mm3_split(lhs, rhs_hi, rhs_lo):
    lh, ll = _split2_bf(lhs)
    return _dot(lh, rhs_hi) + (_dot(lh, rhs_lo) + _dot(ll, rhs_hi))


def _mm3_pair(lhs, x_pair, mask):
    xh, xl = _split2_bf(x_pair)
    return _mm3_split(lhs, _bf(_block_rows(xh.astype(F32), mask)), _bf(_block_rows(xl.astype(F32), mask)))


def _gdn_expand_matrix():
    h = jnp.arange(LANES)[:, None]
    e64 = (jnp.arange(GDN_HEADS * CHUNK)[None, :] // CHUNK == h)
    e128 = (jnp.arange(GDN_HEADS * GDN_DK)[None, :] // GDN_DK == h)
    return jnp.concatenate([e64, e128], axis=1).astype(BF16)


def _gdn_recur_body(*refs, geo):
    refs = list(refs)
    h_ref, ab_ref, wc_ref, alog_ref, dtb_ref, ng_ref, eall_ref = refs[:7]
    pos = 7
    conv0_ref = s0_ref = None
    if not geo.long:
        conv0_ref, s0_ref = refs[pos:pos + 2]
        pos += 2
    og_ref, convn_ref, sout_ref = refs[pos:pos + 3]
    pos += 3
    ext_scr, qkv_scr, g_scr, beta_scr, u_scr, w_scr, att_scr, egc_scr, kend_scr = refs[pos:pos + 9]
    s_scr = refs[pos + 9] if geo.long else None
    c, seg, tc = CHUNK, geo.seg, geo.tc
    n_seg = c // seg
    get_state, put_state = _state_access(geo, s0_ref, sout_ref, s_scr)
    pad = SUBLANES
    lo = pad - (GDN_CONV - 1)
    n_pair = GDN_HEADS // 2
    pw = 2 * c
    hw = GDN_HEADS * c

    if geo.long:
        @pl.when(pl.program_id(1) == 0)
        def _():
            ext_scr[0:pad, :] = jnp.zeros((pad, GDN_CONV_CH), F32)
            s_scr[...] = jnp.zeros_like(s_scr)
    else:
        ext_scr[0:pad, :] = jnp.zeros((pad, GDN_CONV_CH), F32)

    ext_scr[pad:pad + tc, :] = h_ref[:, 0:GDN_CONV_CH]
    if geo.long:
        conv = ext_scr[lo:lo + tc, :] * wc_ref[0:1, :]
        for j in range(1, GDN_CONV):
            conv = conv + ext_scr[lo + j:lo + j + tc, :] * wc_ref[j:j + 1, :]
        qkv_scr[...] = _silu(conv)

        @pl.when(geo.last_time_block())
        def _():
            convn_ref[0, 0] = ext_scr[tc + lo:tc + pad, :]

        ext_scr[0:pad, :] = ext_scr[tc:tc + pad, :]
    else:
        t_in = lax.broadcasted_iota(jnp.int32, (tc, GDN_CONV_CH), 0) & (seg - 1)
        conv = ext_scr[pad:pad + tc, :] * wc_ref[GDN_CONV - 1:GDN_CONV, :]
        for j in range(GDN_CONV - 1):
            back = GDN_CONV - 1 - j
            conv = conv + jnp.where(t_in >= back, ext_scr[lo + j:lo + j + tc, :], 0.0) * wc_ref[j:j + 1, :]
        qkv_scr[...] = conv
        for sq in range(geo.seqs):
            c0 = conv0_ref[0, sq]
            head = []
            for t in range(GDN_CONV - 1):
                acc = c0[t:t + 1, :] * wc_ref[0:1, :]
                for j in range(1, GDN_CONV - 1 - t):
                    acc = acc + c0[t + j:t + j + 1, :] * wc_ref[j:j + 1, :]
                head.append(acc)
            head.append(jnp.zeros((SUBLANES - (GDN_CONV - 1), GDN_CONV_CH), F32))
            first = slice(sq * seg, sq * seg + SUBLANES)
            qkv_scr[first, :] = qkv_scr[first, :] + jnp.concatenate(head, axis=0)
            convn_ref[0, sq] = h_ref[(sq + 1) * seg - (GDN_CONV - 1):(sq + 1) * seg, 0:GDN_CONV_CH]
        qkv_scr[...] = _silu(qkv_scr[...])

    g_scr[...] = -jnp.exp(alog_ref[...]) * _softplus(ab_ref[:, 0:LANES] + dtb_ref[...])
    beta_scr[...] = _sigmoid(ab_ref[:, LANES:2 * LANES])

    incl, strict, delta = _seg_masks((c, hw), seg)
    eye_pair = jnp.where(delta[:, 0:pw], 1.0, 0.0)
    r2 = lax.broadcasted_iota(jnp.int32, (2 * c, pw), 0) >> 6
    mask_cc = r2 == (lax.broadcasted_iota(jnp.int32, (2 * c, pw), 1) >> 6)
    r3 = lax.broadcasted_iota(jnp.int32, (2 * c, 2 * GDN_DK), 0) >> 6
    mask_cd = r3 == (lax.broadcasted_iota(jnp.int32, (2 * c, 2 * GDN_DK), 1) >> 7)
    ones_c = jnp.ones((c, c), BF16)
    ng = ng_ref[...]

    def local(n, carry):
        rows = pl.ds(pl.multiple_of(n * c, c), c)
        gc = _cumsum_rows(g_scr[rows, :], seg)
        e_all = eall_ref[...]
        sh, sm, sl = _split3_bf(jnp.concatenate([gc, beta_scr[rows, :]], axis=0))
        ex = _dot(sh, e_all) + (_dot(sm, e_all) + _dot(sl, e_all))
        gcol, bcol, gwide = ex[0:c, 0:hw], ex[c:2 * c, 0:hw], ex[0:c, hw:]
        th, tm, tl = _split3_bf(jnp.concatenate(
            [jnp.where(delta, gcol, 0.0), jnp.where(delta, bcol, 0.0)], axis=1))
        rowf = _dot(ones_c, th) + (_dot(ones_c, tm) + _dot(ones_c, tl))
        grow, brow = rowf[:, 0:hw], rowf[:, hw:]
        dec = jnp.where(incl, jnp.exp(jnp.where(incl, gcol - grow, 0.0)), 0.0)
        egrow = jnp.exp(grow)
        egc_scr[rows, :] = jnp.exp(gwide)
        eend = jnp.exp(_seg_last_rows(gwide, seg) - gwide)

        a_pairs = []
        for p in range(n_pair):
            qs, ks = [], []
            for hh in (2 * p, 2 * p + 1):
                kq = slice(hh * GDN_DK, (hh + 1) * GDN_DK)
                kk = slice(GDN_KW + hh * GDN_DK, GDN_KW + (hh + 1) * GDN_DK)
                q = qkv_scr[rows, kq]
                k = qkv_scr[rows, kk]
                q = q * lax.rsqrt(jnp.sum(q * q, axis=-1, keepdims=True) + NORM_EPS) * GDN_DK ** -0.5
                k = k * lax.rsqrt(jnp.sum(k * k, axis=-1, keepdims=True) + NORM_EPS)
                qkv_scr[rows, kq] = q
                qkv_scr[rows, kk] = k
                kend_scr[rows, kq] = k * eend[:, kq]
                qs.append(q)
                ks.append(k)
            kh = _bf(jnp.concatenate(ks, axis=1))
            kbd_h = _bf(_block_rows(kh.astype(F32), mask_cd))
            kkqk = _dot_nt(jnp.concatenate([kh, _bf(jnp.concatenate(qs, axis=1))], axis=0), kbd_h)
            cs = slice(p * pw, (p + 1) * pw)
            a_pairs.append(jnp.where(strict[:, cs], kkqk[0:c] * dec[:, cs] * bcol[:, cs], 0.0))
            att_scr[rows, cs] = kkqk[c:2 * c] * dec[:, cs]
        ps = [eye_pair - a for a in a_pairs]
        xs = [_mm3_pair(a, a, mask_cc) for a in a_pairs]
        steps = int(math.log2(seg)) - 1
        for s in range(steps):
            if s < steps - 1:
                rs = [_mm3_pair(jnp.concatenate([pp, x], axis=0), x, mask_cc) for pp, x in zip(ps, xs)]
                ps = [pp + r[0:c] for pp, r in zip(ps, rs)]
                xs = [r[c:2 * c] for r in rs]
            else:
                ps = [pp + _mm3_pair(pp, x, mask_cc) for pp, x in zip(ps, xs)]
        for p in range(n_pair):
            cs = slice(p * pw, (p + 1) * pw)
            vs = slice(p * 2 * GDN_DV, (p + 1) * 2 * GDN_DV)
            t_b = ps[p] * brow[:, cs]
            vh, vl = _split2_bf(qkv_scr[rows, 2 * GDN_KW + p * 2 * GDN_DV:2 * GDN_KW + (p + 1) * 2 * GDN_DV])
            u_scr[rows, vs] = _mm3_split(t_b, _bf(_block_rows(vh.astype(F32), mask_cd)),
                                         _bf(_block_rows(vl.astype(F32), mask_cd)))
            kh, kl = _split2_bf(qkv_scr[rows, GDN_KW + p * 2 * GDN_DK:GDN_KW + (p + 1) * 2 * GDN_DK])
            w_scr[rows, vs] = _mm3_split(t_b * egrow[:, cs], _bf(_block_rows(kh.astype(F32), mask_cd)),
                                         _bf(_block_rows(kl.astype(F32), mask_cd)))
        return carry

    lax.fori_loop(0, tc // c, local, 0)

    def recur(n, carry):
        r0 = pl.multiple_of(n * c, c)
        rows = pl.ds(r0, c)
        heads = range(GDN_HEADS)
        kqs = [slice(hh * GDN_DK, (hh + 1) * GDN_DK) for hh in heads]
        vn_seg = [[] for _ in heads]
        qs_seg = [[] for _ in heads]
        for s in range(n_seg):
            sq = n * n_seg + s
            sr = pl.ds(pl.multiple_of(r0 + s * seg, SUBLANES), seg)
            tail = pl.ds(pl.multiple_of(r0 + (s + 1) * seg - SUBLANES, SUBLANES), SUBLANES)
            sts = [get_state(sq, hh) for hh in heads]
            m1s = [_dot(_bf(jnp.concatenate([w_scr[sr, kqs[hh]], qkv_scr[sr, kqs[hh]]], axis=0)), _bf(sts[hh]))
                   for hh in heads]
            vn = [u_scr[sr, kqs[hh]] - m1s[hh][0:seg] for hh in heads]
            upd = [_dot_tn(_bf(kend_scr[sr, kqs[hh]]), _bf(vn[hh])) for hh in heads]
            for hh in heads:
                e_last = egc_scr[tail, kqs[hh]][SUBLANES - 1:SUBLANES, :]
                put_state(sq, hh, sts[hh] * e_last + upd[hh])
                vn_seg[hh].append(vn[hh])
                qs_seg[hh].append(m1s[hh][seg:2 * seg])
        join = lambda parts: parts[0] if n_seg == 1 else jnp.concatenate(parts, axis=0)
        for p in range(n_pair):
            pair = (2 * p, 2 * p + 1)
            vs = slice(p * 2 * GDN_DV, (p + 1) * 2 * GDN_DV)
            vbd = _bf(_block_rows(jnp.concatenate([join(vn_seg[hh]) for hh in pair], axis=1), mask_cd))
            o_pair = (_dot(_bf(att_scr[rows, p * pw:(p + 1) * pw]), vbd)
                      + egc_scr[rows, vs] * jnp.concatenate([join(qs_seg[hh]) for hh in pair], axis=1))
            for i, hh in enumerate(pair):
                o = o_pair[:, i * GDN_DV:(i + 1) * GDN_DV]
                o = o * lax.rsqrt(jnp.mean(o * o, axis=-1, keepdims=True) + NORM_EPS) * ng
                kz = slice(GDN_CONV_CH + hh * GDN_DV, GDN_CONV_CH + (hh + 1) * GDN_DV)
                og_ref[rows, hh * GDN_DV:(hh + 1) * GDN_DV] = o * _silu(h_ref[rows, kz])
        return carry

    lax.fori_loop(0, tc // c, recur, 0)

    if geo.long:
        @pl.when(geo.last_time_block())
        def _():
            sout_ref[0, 0] = s_scr[...]


def _gdn_recur(h, ab, w_conv, a_log, dt_bias, norm_g, conv_in, state_in, layer, n_layers, geo):
    tail = (GDN_HEADS, GDN_DK, GDN_DV)
    cv_tail = (GDN_CONV - 1, GDN_CONV_CH)
    consts = [w_conv, a_log, dt_bias, norm_g, _gdn_expand_matrix()]
    ins = [h, ab] + consts
    in_specs = [pl.BlockSpec((geo.tc, GDN_MAIN), geo.row_map), pl.BlockSpec((geo.tc, 2 * LANES), geo.row_map)]
    in_specs += [_const_spec(a) for a in consts]
    if not geo.long:
        ins += [conv_in, state_in]
        in_specs += [geo.state_spec(layer, cv_tail), geo.state_spec(layer, tail)]
    tc = geo.tc
    scratch = [pltpu.VMEM((tc + SUBLANES, GDN_CONV_CH), F32), pltpu.VMEM((tc, GDN_CONV_CH), F32),
               pltpu.VMEM((tc, LANES), F32), pltpu.VMEM((tc, LANES), F32),
               pltpu.VMEM((tc, GDN_VW), F32), pltpu.VMEM((tc, GDN_KW), F32),
               pltpu.VMEM((tc, GDN_HEADS * CHUNK), F32), pltpu.VMEM((tc, GDN_KW), F32),
               pltpu.VMEM((tc, GDN_KW), F32)]
    if geo.long:
        scratch.append(pltpu.VMEM(tail, F32))
    return pl.pallas_call(
        functools.partial(_gdn_recur_body, geo=geo),
        grid=geo.grid, in_specs=in_specs,
        out_specs=[pl.BlockSpec((tc, GDN_VW), geo.row_map), geo.state_spec(layer, cv_tail),
                   geo.state_spec(layer, tail)],
        out_shape=[jax.ShapeDtypeStruct((geo.n_b * geo.n_l, GDN_VW), F32),
                   jax.ShapeDtypeStruct((n_layers, geo.n_b) + cv_tail, F32),
                   jax.ShapeDtypeStruct((n_layers, geo.n_b) + tail, F32)],
        scratch_shapes=scratch,
        compiler_params=_params(geo.sem), name="gdn_recur",
    )(*ins)


def _s5_body(*refs, tt, nb, has_init):
    if has_init:
        (x_ref, win_ref, wb_ref, are_ref, aim_ref, wc_ref, d_ref, wglu_ref, bglu_ref, wout_ref,
         lng_ref, lnb_ref, h0re_ref, h0im_ref, y_ref, hre_ref, him_ref,
         x_scr, u_scr, z_scr, bu_scr, y_scr, st_scr) = refs
    else:
        (x_ref, win_ref, wb_ref, are_ref, aim_ref, wc_ref, d_ref, wglu_ref, bglu_ref, wout_ref,
         lng_ref, lnb_ref, y_ref, hre_ref, him_ref,
         x_scr, u_scr, z_scr, bu_scr, y_scr, st_scr) = refs
    tb = pl.program_id(1)
    ns = S5_KT_STATES

    @pl.when(tb == 0)
    def _():
        if not has_init:
            st_scr[...] = jnp.zeros_like(st_scr)
        else:
            for kt in range(S5_KT):
                st_scr[:, 2 * kt * ns:(2 * kt + 1) * ns] = h0re_ref[0, :, kt * ns:(kt + 1) * ns]
                st_scr[:, (2 * kt + 1) * ns:(2 * kt + 2) * ns] = h0im_ref[0, :, kt * ns:(kt + 1) * ns]

    x_scr[...] = jnp.swapaxes(x_ref[...], 0, 1).reshape(tt * nb, D_MODEL)
    h = _dot(_bf(x_scr[...]), win_ref[...])
    u_scr[...] = h[:, 0:S5_WIDTH]
    z_scr[...] = h[:, S5_WIDTH:]

    for kt in range(S5_KT):
        cols = slice(kt * S5_KT_W, (kt + 1) * S5_KT_W)
        bu_scr[...] = _dot(_bf(u_scr[:, cols]), wb_ref[kt])
        a_re = jnp.broadcast_to(are_ref[kt], (SUBLANES, ns))
        a_im = jnp.broadcast_to(aim_ref[kt], (SUBLANES, ns))
        base = kt * 2 * ns

        def row_tile(rb, carry):
            r_off = pl.multiple_of(rb * SUBLANES, SUBLANES)
            st_rows = pl.ds(r_off, SUBLANES)
            h_re0 = st_scr[st_rows, base:base + ns]
            h_im0 = st_scr[st_rows, base + ns:base + 2 * ns]

            def step(t, hc):
                h_re, h_im = hc
                rows = pl.ds(pl.multiple_of(t * nb + r_off, SUBLANES), SUBLANES)
                n_re = a_re * h_re - a_im * h_im + bu_scr[rows, 0:ns]
                n_im = a_re * h_im + a_im * h_re + bu_scr[rows, ns:2 * ns]
                bu_scr[rows, 0:ns] = n_re
                bu_scr[rows, ns:2 * ns] = n_im
                return n_re, n_im

            h_re, h_im = lax.fori_loop(0, tt, step, (h_re0, h_im0))
            st_scr[st_rows, base:base + ns] = h_re
            st_scr[st_rows, base + ns:base + 2 * ns] = h_im
            return carry

        lax.fori_loop(0, nb // SUBLANES, row_tile, 0)
        y_scr[:, cols] = _dot(_bf(bu_scr[...]), wc_ref[kt]) + d_ref[:, cols] * u_scr[:, cols]

    y = _gelu_tanh(y_scr[...])
    yg = _dot(_bf(y), wglu_ref[...]) + bglu_ref[...]
    y = yg[:, 0:S5_WIDTH] * _sigmoid(yg[:, S5_WIDTH:]) * _silu(z_scr[...])
    out = _dot(_bf(y), wout_ref[...])
    y = _layer_norm(ALPHA_RES * x_scr[...] + out, lng_ref[...], lnb_ref[...])
    y_ref[...] = jnp.swapaxes(y.reshape(tt, nb, D_MODEL), 0, 1)

    @pl.when(tb == pl.num_programs(1) - 1)
    def _():
        for kt in range(S5_KT):
            hre_ref[0, :, kt * ns:(kt + 1) * ns] = st_scr[:, 2 * kt * ns:(2 * kt + 1) * ns]
            him_ref[0, :, kt * ns:(kt + 1) * ns] = st_scr[:, (2 * kt + 1) * ns:(2 * kt + 2) * ns]


def _s5_discretize(lam_re, lam_im, log_dt, b_re, b_im, c_re, c_im):
    dt = jnp.exp(log_dt)[:, None]
    mag = jnp.exp(lam_re * dt)
    ab_re, ab_im = mag * jnp.cos(lam_im * dt), mag * jnp.sin(lam_im * dt)
    den = jnp.square(lam_re) + jnp.square(lam_im)
    num_re = ab_re - 1.0
    coef_re = (num_re * lam_re + ab_im * lam_im) / den
    coef_im = (ab_im * lam_re - num_re * lam_im) / den
    bb_re = coef_re[..., None] * b_re - coef_im[..., None] * b_im
    bb_im = coef_re[..., None] * b_im + coef_im[..., None] * b_re
    gl = S5_GROUPS // S5_KT
    eye = jnp.eye(gl, dtype=F32)

    def block_b(bb):
        t = bb.reshape(S5_KT, gl, S5_STATE, S5_GROUP)
        return jnp.einsum("kgpc,gh->kgchp", t, eye).reshape(S5_KT, S5_KT_W, S5_KT_STATES)

    def block_c(cc):
        t = cc.reshape(S5_KT, gl, S5_GROUP, S5_STATE)
        return jnp.einsum("kgcp,gh->kgphc", t, eye).reshape(S5_KT, S5_KT_STATES, S5_KT_W)

    w_b = jnp.concatenate([block_b(bb_re), block_b(bb_im)], axis=2).astype(BF16)
    w_c = jnp.concatenate([block_c(c_re), -block_c(c_im)], axis=1).astype(BF16)
    a_re = ab_re.reshape(S5_KT, 1, S5_KT_STATES)
    a_im = ab_im.reshape(S5_KT, 1, S5_KT_STATES)
    return w_b, w_c, a_re, a_im


def _s5_layer(x, prep, w_in, d_vec, w_glu, b_glu, w_out, ln_g, ln_b, h0, n_b, n_l):
    w_b, w_c, a_re, a_im = prep
    if n_l >= S5_LONG_STEPS:
        nb, tt = n_b, S5_LONG_STEPS
    else:
        nb, tt = min(S5_SHORT_SEQS, n_b), n_l
    assert nb % SUBLANES == 0 and tt % SUBLANES == 0
    rows = tt * nb
    has_init = h0 is not None
    consts = [w_in, w_b, a_re, a_im, w_c, d_vec, w_glu, b_glu, w_out, ln_g, ln_b]
    x_spec = pl.BlockSpec((nb, tt, D_MODEL), lambda b, t: (b, t, 0))
    st_spec = pl.BlockSpec((1, nb, S5_NSTATE), lambda b, t: (0, b, 0))
    ins = [x.reshape(n_b, n_l, D_MODEL)] + consts
    in_specs = [x_spec] + [_const_spec(a) for a in consts]
    if has_init:
        ins += list(h0)
        in_specs += [st_spec, st_spec]
    y, h_re, h_im = pl.pallas_call(
        functools.partial(_s5_body, tt=tt, nb=nb, has_init=has_init),
        grid=(n_b // nb, n_l // tt), in_specs=in_specs,
        out_specs=[x_spec, st_spec, st_spec],
        out_shape=[jax.ShapeDtypeStruct((n_b, n_l, D_MODEL), F32),
                   jax.ShapeDtypeStruct((1, n_b, S5_NSTATE), F32),
                   jax.ShapeDtypeStruct((1, n_b, S5_NSTATE), F32)],
        scratch_shapes=[pltpu.VMEM((rows, D_MODEL), F32),
                        pltpu.VMEM((rows, S5_WIDTH), F32), pltpu.VMEM((rows, S5_WIDTH), F32),
                        pltpu.VMEM((rows, 2 * S5_KT_STATES), F32), pltpu.VMEM((rows, S5_WIDTH), F32),
                        pltpu.VMEM((nb, 2 * S5_NSTATE), F32)],
        compiler_params=_params(("parallel", "arbitrary")), name="s5_layer",
    )(*ins)
    return y.reshape(n_b * n_l, D_MODEL), h_re, h_im


def _row2(v):
    return v.reshape(1, -1).astype(F32)


def _pad_cols(w, n):
    return jnp.concatenate([w, jnp.zeros((w.shape[0], n - w.shape[1]), w.dtype)], axis=1)


def _trunk(x, n_b, n_l, states, wts):
    geo = _Geometry(n_b, n_l)
    n_gla, n_gdn, n_s5 = (DEPTH + 2) // 3, (DEPTH + 1) // 3, DEPTH // 3
    assert n_gdn == 1 and n_s5 == 1
    s_gla = s_gdn = s_conv = s_re = s_im = None
    for i in range(DEPTH):
        j, kind = divmod(i, 3)
        ln_g, ln_b = _row2(wts["ln_g"][i]), _row2(wts["ln_b"][i])
        if kind == 0:
            w_in = wts["gla_w_in"][j]
            w_main = _bf(w_in[:, :GLA_MAIN])
            w_lr = _bf(_pad_cols(w_in[:, GLA_MAIN:], LANES))
            w_a2 = _bf(jnp.concatenate(
                [wts["gla_w_a2"][j], jnp.zeros((LANES - GLA_LOWRANK, GLA_KW), F32)], axis=0))
            h, la = _rowwise_call(_gla_proj_body, "gla_proj", [x],
                                  [w_main, w_lr, w_a2, _row2(wts["gla_b_a"][j])], [GLA_MAIN, GLA_KW])
            og, s_gla = _gla_recur(h, la, _row2(wts["gla_norm_g"][j]),
                                   None if states is None else states[0], j, n_gla, s_gla, geo)
            w_out = _bf(wts["gla_w_out"][j])
        elif kind == 1:
            w_in = wts["gdn_w_in"][j]
            w_main = _bf(w_in[:, :GDN_MAIN])
            w_ab = _bf(jnp.concatenate(
                [_pad_cols(w_in[:, GDN_MAIN:GDN_MAIN + GDN_HEADS], LANES),
                 _pad_cols(w_in[:, GDN_MAIN + GDN_HEADS:], LANES)], axis=1))
            h, ab = _rowwise_call(_gdn_proj_body, "gdn_proj", [x], [w_main, w_ab], [GDN_MAIN, 2 * LANES])
            og, s_conv, s_gdn = _gdn_recur(
                h, ab, wts["gdn_w_conv"][j].astype(F32),
                _pad_cols(_row2(wts["gdn_a_log"][j]), LANES), _pad_cols(_row2(wts["gdn_dt_bias"][j]), LANES),
                _row2(wts["gdn_norm_g"][j]),
                None if states is None else states[2], None if states is None else states[1],
                j, n_gdn, geo)
            w_out = _bf(wts["gdn_w_out"][j])
        else:
            prep = _s5_discretize(wts["s5_lam_re"][j].astype(F32), wts["s5_lam_im"][j].astype(F32),
                                  wts["s5_log_dt"][j].astype(F32), wts["s5_b_re"][j].astype(F32),
                                  wts["s5_b_im"][j].astype(F32), wts["s5_c_re"][j].astype(F32),
                                  wts["s5_c_im"][j].astype(F32))
            h0 = None if states is None else (states[3].reshape(n_s5, n_b, S5_NSTATE),
                                              states[4].reshape(n_s5, n_b, S5_NSTATE))
            x, h_re, h_im = _s5_layer(x, prep, _bf(wts["s5_w_in"][j]), _row2(wts["s5_d"][j]),
                                      _bf(wts["s5_w_glu"][j]), _row2(wts["s5_b_glu"][j]),
                                      _bf(wts["s5_w_out"][j]), ln_g, ln_b, h0, n_b, n_l)
            s_re = h_re.reshape(n_s5, n_b, S5_GROUPS, S5_STATE)
            s_im = h_im.reshape(n_s5, n_b, S5_GROUPS, S5_STATE)
            continue
        (x,) = _rowwise_call(_out_ln_body, "out_ln", [og, x], [w_out, ln_g, ln_b], [D_MODEL])
    return x, s_gla, s_gdn, s_conv, s_re, s_im


def kernel(x_prompt, x_sample, state_gla, state_gdn, state_gdn_conv, state_s5_re, state_s5_im,
           ln_g, ln_b, gla_w_in, gla_w_a2, gla_b_a, gla_norm_g, gla_w_out,
           gdn_w_in, gdn_w_conv, gdn_a_log, gdn_dt_bias, gdn_norm_g, gdn_w_out,
           s5_w_in, s5_lam_re, s5_lam_im, s5_log_dt, s5_b_re, s5_b_im, s5_c_re, s5_c_im,
           s5_d, s5_w_glu, s5_b_glu, s5_w_out):
    wts = dict(ln_g=ln_g, ln_b=ln_b,
               gla_w_in=gla_w_in, gla_w_a2=gla_w_a2, gla_b_a=gla_b_a, gla_norm_g=gla_norm_g,
               gla_w_out=gla_w_out,
               gdn_w_in=gdn_w_in, gdn_w_conv=gdn_w_conv, gdn_a_log=gdn_a_log, gdn_dt_bias=gdn_dt_bias,
               gdn_norm_g=gdn_norm_g, gdn_w_out=gdn_w_out,
               s5_w_in=s5_w_in, s5_lam_re=s5_lam_re, s5_lam_im=s5_lam_im, s5_log_dt=s5_log_dt,
               s5_b_re=s5_b_re, s5_b_im=s5_b_im, s5_c_re=s5_c_re, s5_c_im=s5_c_im, s5_d=s5_d,
               s5_w_glu=s5_w_glu, s5_b_glu=s5_b_glu, s5_w_out=s5_w_out)

    bp, lp, _ = x_prompt.shape
    yp, p_gla, p_gdn, p_conv, p_re, p_im = _trunk(
        x_prompt.reshape(bp * lp, D_MODEL), bp, lp, None, wts)
    bs, ls, _ = x_sample.shape
    ys, s_gla, s_gdn, s_conv, s_re, s_im = _trunk(
        x_sample.reshape(bs * ls, D_MODEL), bs, ls,
        (state_gla, state_gdn, state_gdn_conv, state_s5_re, state_s5_im), wts)
    return (yp.reshape(bp, lp, D_MODEL), ys.reshape(bs, ls, D_MODEL),
            p_gla, p_gdn, p_conv, p_re, p_im, s_gla, s_gdn, s_conv, s_re, s_im)
```

```python
import functools
import math

import jax
import jax.numpy as jnp
from jax import lax
from jax.experimental import pallas as pl
from jax.experimental.pallas import tpu as pltpu

F32 = jnp.float32
BF16 = jnp.bfloat16

D_MODEL = 1024
DEPTH = 4
ALPHA_RES = (2 * DEPTH) ** 0.25
LN_EPS = 1e-5
NORM_EPS = 1e-6
CHUNK = 64

GLA_HEADS = 4
GLA_KW = 512
GLA_VW = 1024
GLA_DK = 128
GLA_DV = 256
GLA_LOWRANK = 16
GLA_TAU = 16.0
GLA_MAIN = 2 * GLA_KW + 2 * GLA_VW

GDN_HEADS = 8
GDN_DK = 128
GDN_DV = 128
GDN_KW = 1024
GDN_VW = 1024
GDN_CONV = 4
GDN_CONV_CH = 3072
GDN_MAIN = GDN_CONV_CH + GDN_VW

S5_WIDTH = 1024
S5_GROUP = 16
S5_GROUPS = 64
S5_STATE = 64
S5_KT = 4
S5_KT_W = S5_WIDTH // S5_KT
S5_KT_STATES = (S5_GROUPS // S5_KT) * S5_STATE
S5_NSTATE = S5_GROUPS * S5_STATE

LANES = 128
SUBLANES = 8
VMEM_LIMIT = 56 * 1024 * 1024
ROW_BLOCK = 512
LONG_TIME_BLOCK = 512
S5_LONG_STEPS = 64
S5_SHORT_SEQS = 64


def _bf(x):
    return x.astype(BF16)


def _dot(a, b):
    return jnp.dot(a, b, preferred_element_type=F32)


def _dot_nt(a, b):
    return lax.dot_general(a, b, (((1,), (1,)), ((), ())), preferred_element_type=F32)


def _dot_tn(a, b):
    return lax.dot_general(a, b, (((0,), (0,)), ((), ())), preferred_element_type=F32)


def _split2_bf(x):
    h = x.astype(BF16)
    return h, (x - h.astype(F32)).astype(BF16)


def _split3_bf(x):
    h = x.astype(BF16)
    r = x - h.astype(F32)
    m = r.astype(BF16)
    return h, m, (r - m.astype(F32)).astype(BF16)


def _sigmoid(x):
    return 1.0 / (1.0 + jnp.exp(-x))


def _silu(x):
    return x * _sigmoid(x)


def _softplus(x):
    return jnp.maximum(x, 0.0) + jnp.log(1.0 + jnp.exp(-jnp.abs(x)))


def _gelu_tanh(x):
    return 0.5 * x * (1.0 + jnp.tanh(math.sqrt(2.0 / math.pi) * (x + 0.044715 * (x * x * x))))


def _layer_norm(x, g, b):
    mu = jnp.mean(x, axis=-1, keepdims=True)
    xc = x - mu
    var = jnp.mean(xc * xc, axis=-1, keepdims=True)
    return xc * lax.rsqrt(var + LN_EPS) * g + b


def _cumsum_rows(x, seg):
    row = lax.broadcasted_iota(jnp.int32, x.shape, 0) & (seg - 1)
    s = 1
    while s < seg:
        x = x + jnp.where(row >= s, pltpu.roll(x, s, axis=0), 0.0)
        s *= 2
    return x


def _seg_last_rows(x, seg):
    c, w = x.shape
    if seg == c:
        return jnp.broadcast_to(x[c - 1:c, :], (c, w))
    x3 = x.reshape(c // seg, seg, w)
    return jnp.broadcast_to(x3[:, seg - 1:seg, :], x3.shape).reshape(c, w)


def _seg_masks(shape, seg):
    ri = lax.broadcasted_iota(jnp.int32, shape, 0)
    li = lax.broadcasted_iota(jnp.int32, shape, 1) & (CHUNK - 1)
    shift = int(math.log2(seg))
    same = (ri >> shift) == (li >> shift)
    return same & (ri >= li), same & (ri > li), ri == li


def _const_spec(arr):
    nd = arr.ndim
    return pl.BlockSpec(arr.shape, lambda *_: (0,) * nd, pipeline_mode=pl.Buffered(1))


def _params(sem):
    return pltpu.CompilerParams(dimension_semantics=sem, vmem_limit_bytes=VMEM_LIMIT)


def _rowwise_call(body, name, rows, consts, out_widths):
    n = rows[0].shape[0]
    tm = min(ROW_BLOCK, n)
    spec = lambda w: pl.BlockSpec((tm, w), lambda i: (i, 0))
    return pl.pallas_call(
        body, grid=(n // tm,),
        in_specs=[spec(r.shape[1]) for r in rows] + [_const_spec(c) for c in consts],
        out_specs=[spec(w) for w in out_widths],
        out_shape=[jax.ShapeDtypeStruct((n, w), F32) for w in out_widths],
        compiler_params=_params(("parallel",)), name=name,
    )(*rows, *consts)


def _gla_proj_body(x_ref, w_ref, wlr_ref, wa2_ref, ba_ref, h_ref, la_ref):
    xb = _bf(x_ref[...])
    h = _dot(xb, w_ref[...])
    h_ref[:, :GLA_KW] = h[:, :GLA_KW] * GLA_DK ** -0.5
    h_ref[:, GLA_KW:] = h[:, GLA_KW:]
    lr = _dot(xb, wlr_ref[...])
    pre = _dot(_bf(lr), wa2_ref[...]) + ba_ref[...]
    la_ref[...] = -_softplus(-pre) * (1.0 / GLA_TAU)


def _gdn_proj_body(x_ref, w_ref, wab_ref, h_ref, ab_ref):
    xb = _bf(x_ref[...])
    h_ref[...] = _dot(xb, w_ref[...])
    ab_ref[...] = _dot(xb, wab_ref[...])


def _out_ln_body(og_ref, x_ref, w_ref, g_ref, b_ref, y_ref):
    out = _dot(_bf(og_ref[...]), w_ref[...])
    y_ref[...] = _layer_norm(ALPHA_RES * x_ref[...] + out, g_ref[...], b_ref[...])


class _Geometry:
    def __init__(self, n_b, n_l):
        self.n_b, self.n_l = n_b, n_l
        self.long = n_l >= CHUNK
        if self.long:
            assert n_l % LONG_TIME_BLOCK == 0
            self.seg = CHUNK
            self.tc = LONG_TIME_BLOCK
            self.n_t = n_l // self.tc
            self.grid = (n_b, self.n_t)
            self.sem = ("parallel", "arbitrary")
            self.seqs = 1
            self.row_map = lambda b, t: (b * self.n_t + t, 0)
            self.seq_block = lambda b, t: b
        else:
            assert CHUNK % n_l == 0 and n_l % SUBLANES == 0 and (n_b * n_l) % CHUNK == 0
            self.seg = n_l
            self.tc = CHUNK
            self.grid = (n_b * n_l // self.tc,)
            self.sem = ("parallel",)
            self.seqs = self.tc // n_l
            self.row_map = lambda i: (i, 0)
            self.seq_block = lambda i: i

    def state_spec(self, layer, tail, n_layers=1):
        zeros = (0,) * len(tail)
        first = layer if n_layers == 1 else 0
        return pl.BlockSpec((n_layers, self.seqs) + tail, lambda *g: (first, self.seq_block(*g)) + zeros)

    def last_time_block(self):
        return pl.program_id(1) == pl.num_programs(1) - 1 if self.long else None


def _state_access(geo, s0_ref, sout_ref, s_scr, out_layer=0):
    if geo.long:
        return (lambda sq, hh: s_scr[hh]), (lambda sq, hh, v: s_scr.__setitem__(hh, v))
    return ((lambda sq, hh: s0_ref[0, sq, hh]),
            (lambda sq, hh, v: sout_ref.__setitem__((out_layer, sq, hh), v)))


def _gla_recur_body(*refs, geo, has_alias, layer, out_layers):
    refs = list(refs)
    h_ref, la_ref, ng_ref = refs[:3]
    pos = 3
    s0_ref = None
    if not geo.long:
        s0_ref = refs[pos]
        pos += 1
    if has_alias:
        pos += 1
    og_ref, sout_ref = refs[pos:pos + 2]
    s_scr = refs[pos + 2] if geo.long else None
    c, seg, tc = CHUNK, geo.seg, geo.tc
    n_seg = c // seg
    out_layer = layer if out_layers > 1 else 0
    get_state, put_state = _state_access(geo, s0_ref, sout_ref, s_scr, out_layer)

    def zero_other_layers():
        for other in range(out_layers):
            if other != out_layer:
                sout_ref[other] = jnp.zeros(sout_ref.shape[1:], F32)

    if geo.long:
        @pl.when(pl.program_id(1) == 0)
        def _():
            s_scr[...] = jnp.zeros_like(s_scr)
    else:
        zero_other_layers()

    incl, _, _ = _seg_masks((c, c), seg)
    ng = ng_ref[...]

    def chunk(n, carry):
        r0 = pl.multiple_of(n * c, c)
        rows = pl.ds(r0, c)
        heads = range(GLA_HEADS)
        b = _cumsum_rows(la_ref[rows, :], seg)
        bl = _seg_last_rows(b, seg)
        k_all = h_ref[rows, GLA_KW:2 * GLA_KW]
        q_dec = h_ref[rows, 0:GLA_KW] * jnp.exp(b)
        k_inv = k_all * jnp.exp(-b)
        k_end = k_all * jnp.exp(bl - b)
        e_bl = jnp.exp(bl)
        kq = [slice(hh * GLA_DK, (hh + 1) * GLA_DK) for hh in heads]
        vs = [h_ref[rows, 2 * GLA_KW + hh * GLA_DV:2 * GLA_KW + (hh + 1) * GLA_DV] for hh in heads]
        atts = [jnp.where(incl, _dot_nt(_bf(q_dec[:, kq[hh]]), _bf(k_inv[:, kq[hh]])), 0.0) for hh in heads]
        os_ = [_dot(_bf(atts[hh]), _bf(vs[hh])) for hh in heads]
        inter = [[] for _ in heads]
        for s in range(n_seg):
            sq = n * n_seg + s
            sr = slice(s * seg, (s + 1) * seg)
            sts = [get_state(sq, hh) for hh in heads]
            for hh in heads:
                inter[hh].append(_dot(_bf(q_dec[sr, kq[hh]]), _bf(sts[hh])))
            upd = [_dot_tn(_bf(k_end[sr, kq[hh]]), _bf(vs[hh][sr])) for hh in heads]
            for hh in heads:
                ecol = jnp.transpose(jnp.broadcast_to(e_bl[s * seg:s * seg + 1, kq[hh]], (GLA_DK, GLA_DK)))
                ecol = jnp.concatenate([ecol] * (GLA_DV // GLA_DK), axis=1)
                put_state(sq, hh, sts[hh] * ecol + upd[hh])
        for hh in heads:
            o = os_[hh] + (inter[hh][0] if n_seg == 1 else jnp.concatenate(inter[hh], axis=0))
            o = o * lax.rsqrt(jnp.mean(o * o, axis=-1, keepdims=True) + NORM_EPS) * ng
            kr = slice(2 * GLA_KW + GLA_VW + hh * GLA_DV, 2 * GLA_KW + GLA_VW + (hh + 1) * GLA_DV)
            og_ref[rows, hh * GLA_DV:(hh + 1) * GLA_DV] = o * _silu(h_ref[rows, kr])
        return carry

    lax.fori_loop(0, tc // c, chunk, 0)

    if geo.long:
        @pl.when(geo.last_time_block())
        def _():
            sout_ref[out_layer, 0] = s_scr[...]
            zero_other_layers()


def _gla_recur(h, la, norm_g, state_in, layer, n_layers, prev_out, geo):
    tail = (GLA_HEADS, GLA_DK, GLA_DV)
    ins = [h, la, norm_g]
    in_specs = [pl.BlockSpec((geo.tc, GLA_MAIN), geo.row_map), pl.BlockSpec((geo.tc, GLA_KW), geo.row_map),
                _const_spec(norm_g)]
    if not geo.long:
        ins.append(state_in)
        in_specs.append(geo.state_spec(layer, tail))
    aliases = {}
    if prev_out is not None:
        aliases = {len(ins): 1}
        ins.append(prev_out)
        in_specs.append(pl.BlockSpec(memory_space=pl.ANY))
    scratch = [pltpu.VMEM(tail, F32)] if geo.long else []
    out_layers = n_layers if prev_out is None else 1
    og, s_out = pl.pallas_call(
        functools.partial(_gla_recur_body, geo=geo, has_alias=prev_out is not None,
                          layer=layer, out_layers=out_layers),
        grid=geo.grid, in_specs=in_specs,
        out_specs=[pl.BlockSpec((geo.tc, GLA_VW), geo.row_map), geo.state_spec(layer, tail, out_layers)],
        out_shape=[jax.ShapeDtypeStruct((geo.n_b * geo.n_l, GLA_VW), F32),
                   jax.ShapeDtypeStruct((n_layers, geo.n_b) + tail, F32)],
        scratch_shapes=scratch, input_output_aliases=aliases,
        compiler_params=_params(geo.sem), name="gla_recur",
    )(*ins)
    return og, s_out


def _block_rows(x, mask):
    return jnp.where(mask, jnp.concatenate([x, x], axis=0), 0.0)


def _mm3_split(lhs, rhs_hi, rhs_lo):
    lh, ll = _split2_bf(lhs)
    return _dot(lh, rhs_hi) + (_dot(lh, rhs_lo) + _dot(ll, rhs_hi))


def _mm3_pair(lhs, x_pair, mask):
    xh, xl = _split2_bf(x_pair)
    return _mm3_split(lhs, _bf(_block_rows(xh.astype(F32), mask)), _bf(_block_rows(xl.astype(F32), mask)))


def _dot_pair(lhs, x_pair, mask):
    return _dot(_bf(lhs), _bf(_block_rows(x_pair, mask)))


def _gdn_expand_matrix():
    h = jnp.arange(LANES)[:, None]
    e64 = (jnp.arange(GDN_HEADS * CHUNK)[None, :] // CHUNK == h)
    e128 = (jnp.arange(GDN_HEADS * GDN_DK)[None, :] // GDN_DK == h)
    return jnp.concatenate([e64, e128], axis=1).astype(BF16)


def _gdn_recur_body(*refs, geo):
    refs = list(refs)
    h_ref, ab_ref, wc_ref, alog_ref, dtb_ref, ng_ref, eall_ref = refs[:7]
    pos = 7
    conv0_ref = s0_ref = None
    if not geo.long:
        conv0_ref, s0_ref = refs[pos:pos + 2]
        pos += 2
    og_ref, convn_ref, sout_ref = refs[pos:pos + 3]
    pos += 3
    (ext_scr, qkv_scr, g_scr, beta_scr, u_scr, w_scr, att_scr, egc_scr, kend_scr,
     qk_scr) = refs[pos:pos + 10]
    s_scr = refs[pos + 10] if geo.long else None
    c, seg, tc = CHUNK, geo.seg, geo.tc
    n_seg = c // seg
    get_state, put_state = _state_access(geo, s0_ref, sout_ref, s_scr)
    pad = SUBLANES
    lo = pad - (GDN_CONV - 1)
    n_pair = GDN_HEADS // 2
    pw = 2 * c
    hw = GDN_HEADS * c

    if geo.long:
        @pl.when(pl.program_id(1) == 0)
        def _():
            ext_scr[0:pad, :] = jnp.zeros((pad, GDN_CONV_CH), F32)
            s_scr[...] = jnp.zeros_like(s_scr)
    else:
        ext_scr[0:pad, :] = jnp.zeros((pad, GDN_CONV_CH), F32)

    ext_scr[pad:pad + tc, :] = h_ref[:, 0:GDN_CONV_CH]

    def conv_chunk(m):
        r0 = pl.multiple_of(m * c, c)
        for cb in range(GDN_CONV_CH // LANES):
            cols = slice(cb * LANES, (cb + 1) * LANES)
            e = ext_scr[pl.ds(r0, c + pad), cols]
            conv = e[lo:lo + c] * wc_ref[0:1, cols]
            for j in range(1, GDN_CONV):
                conv = conv + e[lo + j:lo + j + c] * wc_ref[j:j + 1, cols]
            qkv_scr[pl.ds(r0, c), cols] = _silu(conv)

    cpi = 2 if (tc // c) % 2 == 0 else 1
    if geo.long:
        for i in range(cpi):
            conv_chunk(i)

        @pl.when(geo.last_time_block())
        def _():
            convn_ref[0, 0] = ext_scr[tc + lo:tc + pad, :]
    else:
        t_in = lax.broadcasted_iota(jnp.int32, (tc, GDN_CONV_CH), 0) & (seg - 1)
        conv = ext_scr[pad:pad + tc, :] * wc_ref[GDN_CONV - 1:GDN_CONV, :]
        for j in range(GDN_CONV - 1):
            back = GDN_CONV - 1 - j
            conv = conv + jnp.where(t_in >= back, ext_scr[lo + j:lo + j + tc, :], 0.0) * wc_ref[j:j + 1, :]
        qkv_scr[...] = conv
        for sq in range(geo.seqs):
            c0 = conv0_ref[0, sq]
            head = []
            for t in range(GDN_CONV - 1):
                acc = c0[t:t + 1, :] * wc_ref[0:1, :]
                for j in range(1, GDN_CONV - 1 - t):
                    acc = acc + c0[t + j:t + j + 1, :] * wc_ref[j:j + 1, :]
                head.append(acc)
            head.append(jnp.zeros((SUBLANES - (GDN_CONV - 1), GDN_CONV_CH), F32))
            first = slice(sq * seg, sq * seg + SUBLANES)
            qkv_scr[first, :] = qkv_scr[first, :] + jnp.concatenate(head, axis=0)
            convn_ref[0, sq] = h_ref[(sq + 1) * seg - (GDN_CONV - 1):(sq + 1) * seg, 0:GDN_CONV_CH]
        qkv_scr[...] = _silu(qkv_scr[...])

    g_scr[...] = -jnp.exp(alog_ref[...]) * _softplus(ab_ref[:, 0:LANES] + dtb_ref[...])
    beta_scr[...] = _sigmoid(ab_ref[:, LANES:2 * LANES])

    incl, strict, delta = _seg_masks((c, hw), seg)
    eye_pair = jnp.where(delta[:, 0:pw], 1.0, 0.0)
    r2 = lax.broadcasted_iota(jnp.int32, (2 * c, pw), 0) >> 6
    mask_cc = r2 == (lax.broadcasted_iota(jnp.int32, (2 * c, pw), 1) >> 6)
    r3 = lax.broadcasted_iota(jnp.int32, (2 * c, 2 * GDN_DK), 0) >> 6
    mask_cd = r3 == (lax.broadcasted_iota(jnp.int32, (2 * c, 2 * GDN_DK), 1) >> 7)
    ones_c = jnp.ones((c, c), BF16)
    ng = ng_ref[...]

    def gate_forms(rows):
        gc = _cumsum_rows(g_scr[rows, :], seg)
        e_all = eall_ref[...]
        sh, sm, sl = _split3_bf(jnp.concatenate([gc, beta_scr[rows, :]], axis=0))
        ex = _dot(sh, e_all) + (_dot(sm, e_all) + _dot(sl, e_all))
        gcol, bcol, gwide = ex[0:c, 0:hw], ex[c:2 * c, 0:hw], ex[0:c, hw:]
        th, tm, tl = _split3_bf(jnp.concatenate(
            [jnp.where(delta, gcol, 0.0), jnp.where(delta, bcol, 0.0)], axis=1))
        rowf = _dot(ones_c, th) + (_dot(ones_c, tm) + _dot(ones_c, tl))
        grow, brow = rowf[:, 0:hw], rowf[:, hw:]
        dec = jnp.where(incl, jnp.exp(jnp.where(incl, gcol - grow, 0.0)), 0.0)
        egc_scr[rows, :] = jnp.exp(gwide)
        eend = jnp.exp(_seg_last_rows(gwide, seg) - gwide)
        return dec, bcol, brow, jnp.exp(grow), eend

    def local(n, carry):
        chunk_rows = [pl.ds(pl.multiple_of((n * cpi + i) * c, c), c) for i in range(cpi)]
        forms = [gate_forms(rows) for rows in chunk_rows]
        units = [(i, p) for i in range(cpi) for p in range(n_pair)]

        a_pairs = []
        for i, p in units:
            rows = chunk_rows[i]
            dec, bcol, _, _, eend = forms[i]
            qs, ks = [], []
            for hh in (2 * p, 2 * p + 1):
                kq = slice(hh * GDN_DK, (hh + 1) * GDN_DK)
                kk = slice(GDN_KW + hh * GDN_DK, GDN_KW + (hh + 1) * GDN_DK)
                q = qkv_scr[rows, kq]
                k = qkv_scr[rows, kk]
                q = q * lax.rsqrt(jnp.sum(q * q, axis=-1, keepdims=True) + NORM_EPS) * GDN_DK ** -0.5
                k = k * lax.rsqrt(jnp.sum(k * k, axis=-1, keepdims=True) + NORM_EPS)
                qk_scr[rows, kq] = q
                qk_scr[rows, kk] = k
                kend_scr[rows, kq] = k * eend[:, kq]
                qs.append(q)
                ks.append(k)
            kh = _bf(jnp.concatenate(ks, axis=1))
            kbd_h = _bf(_block_rows(kh.astype(F32), mask_cd))
            kkqk = _dot_nt(jnp.concatenate([kh, _bf(jnp.concatenate(qs, axis=1))], axis=0), kbd_h)
            cs = slice(p * pw, (p + 1) * pw)
            a_pairs.append(jnp.where(strict[:, cs], kkqk[0:c] * dec[:, cs] * bcol[:, cs], 0.0))
            att_scr[rows, cs] = kkqk[c:2 * c] * dec[:, cs]
        ps = [eye_pair - a for a in a_pairs]
        xs = [_mm3_pair(a, a, mask_cc) for a in a_pairs]
        steps = int(math.log2(seg)) - 1
        for s in range(steps):
            if s < steps - 1:
                rs = [_mm3_pair(jnp.concatenate([pp, x], axis=0), x, mask_cc) for pp, x in zip(ps, xs)]
                ps = [pp + r[0:c] for pp, r in zip(ps, rs)]
                xs = [r[c:2 * c] for r in rs]
            else:
                ps = [pp + _mm3_pair(pp, x, mask_cc) for pp, x in zip(ps, xs)]
        for t_inv, (i, p) in zip(ps, units):
            rows = chunk_rows[i]
            _, _, brow, egrow, _ = forms[i]
            cs = slice(p * pw, (p + 1) * pw)
            vs = slice(p * 2 * GDN_DV, (p + 1) * 2 * GDN_DV)
            t_b = t_inv * brow[:, cs]
            vh, vl = _split2_bf(qkv_scr[rows, 2 * GDN_KW + p * 2 * GDN_DV:2 * GDN_KW + (p + 1) * 2 * GDN_DV])
            u_scr[rows, vs] = _mm3_split(t_b, _bf(_block_rows(vh.astype(F32), mask_cd)),
                                         _bf(_block_rows(vl.astype(F32), mask_cd)))
            kh, kl = _split2_bf(qk_scr[rows, GDN_KW + p * 2 * GDN_DK:GDN_KW + (p + 1) * 2 * GDN_DK])
            w_scr[rows, vs] = _mm3_split(t_b * egrow[:, cs], _bf(_block_rows(kh.astype(F32), mask_cd)),
                                         _bf(_block_rows(kl.astype(F32), mask_cd)))
        if geo.long:
            for i in range(cpi):
                conv_chunk(jnp.minimum((n + 1) * cpi + i, tc // c - cpi + i))
        return carry

    lax.fori_loop(0, tc // (c * cpi), local, 0)
    if geo.long:
        ext_scr[0:pad, :] = ext_scr[tc:tc + pad, :]

    def recur(n, carry):
        r0 = pl.multiple_of(n * c, c)
        rows = pl.ds(r0, c)
        heads = range(GDN_HEADS)
        kqs = [slice(hh * GDN_DK, (hh + 1) * GDN_DK) for hh in heads]
        vn_seg = [[] for _ in heads]
        qs_seg = [[] for _ in heads]
        for s in range(n_seg):
            sq = n * n_seg + s
            sr = pl.ds(pl.multiple_of(r0 + s * seg, SUBLANES), seg)
            tail = pl.ds(pl.multiple_of(r0 + (s + 1) * seg - SUBLANES, SUBLANES), SUBLANES)
            sts = [get_state(sq, hh) for hh in heads]
            m1s = [_dot(_bf(jnp.concatenate([w_scr[sr, kqs[hh]], qk_scr[sr, kqs[hh]]], axis=0)), _bf(sts[hh]))
                   for hh in heads]
            vn = [u_scr[sr, kqs[hh]] - m1s[hh][0:seg] for hh in heads]
            upd = [_dot_tn(_bf(kend_scr[sr, kqs[hh]]), _bf(vn[hh])) for hh in heads]
            for hh in heads:
                e_last = egc_scr[tail, kqs[hh]][SUBLANES - 1:SUBLANES, :]
                put_state(sq, hh, sts[hh] * e_last + upd[hh])
                vn_seg[hh].append(vn[hh])
                qs_seg[hh].append(m1s[hh][seg:2 * seg])
        join = lambda parts: parts[0] if n_seg == 1 else jnp.concatenate(parts, axis=0)
        for p in range(n_pair):
            pair = (2 * p, 2 * p + 1)
            vs = slice(p * 2 * GDN_DV, (p + 1) * 2 * GDN_DV)
            vbd = _bf(_block_rows(jnp.concatenate([join(vn_seg[hh]) for hh in pair], axis=1), mask_cd))
            o_pair = (_dot(_bf(att_scr[rows, p * pw:(p + 1) * pw]), vbd)
                      + egc_scr[rows, vs] * jnp.concatenate([join(qs_seg[hh]) for hh in pair], axis=1))
            for i, hh in enumerate(pair):
                o = o_pair[:, i * GDN_DV:(i + 1) * GDN_DV]
                o = o * lax.rsqrt(jnp.mean(o * o, axis=-1, keepdims=True) + NORM_EPS) * ng
                kz = slice(GDN_CONV_CH + hh * GDN_DV, GDN_CONV_CH + (hh + 1) * GDN_DV)
                og_ref[rows, hh * GDN_DV:(hh + 1) * GDN_DV] = o * _silu(h_ref[rows, kz])
        return carry

    lax.fori_loop(0, tc // c, recur, 0)

    if geo.long:
        @pl.when(geo.last_time_block())
        def _():
            sout_ref[0, 0] = s_scr[...]


def _gdn_recur(h, ab, w_conv, a_log, dt_bias, norm_g, conv_in, state_in, layer, n_layers, geo):
    tail = (GDN_HEADS, GDN_DK, GDN_DV)
    cv_tail = (GDN_CONV - 1, GDN_CONV_CH)
    consts = [w_conv, a_log, dt_bias, norm_g, _gdn_expand_matrix()]
    ins = [h, ab] + consts
    in_specs = [pl.BlockSpec((geo.tc, GDN_MAIN), geo.row_map), pl.BlockSpec((geo.tc, 2 * LANES), geo.row_map)]
    in_specs += [_const_spec(a) for a in consts]
    if not geo.long:
        ins += [conv_in, state_in]
        in_specs += [geo.state_spec(layer, cv_tail), geo.state_spec(layer, tail)]
    tc = geo.tc
    scratch = [pltpu.VMEM((tc + SUBLANES, GDN_CONV_CH), F32), pltpu.VMEM((tc, GDN_CONV_CH), F32),
               pltpu.VMEM((tc, LANES), F32), pltpu.VMEM((tc, LANES), F32),
               pltpu.VMEM((tc, GDN_VW), F32), pltpu.VMEM((tc, GDN_KW), F32),
               pltpu.VMEM((tc, GDN_HEADS * CHUNK), F32), pltpu.VMEM((tc, GDN_KW), F32),
               pltpu.VMEM((tc, GDN_KW), F32), pltpu.VMEM((tc, 2 * GDN_KW), F32)]
    if geo.long:
        scratch.append(pltpu.VMEM(tail, F32))
    return pl.pallas_call(
        functools.partial(_gdn_recur_body, geo=geo),
        grid=geo.grid, in_specs=in_specs,
        out_specs=[pl.BlockSpec((tc, GDN_VW), geo.row_map), geo.state_spec(layer, cv_tail),
                   geo.state_spec(layer, tail)],
        out_shape=[jax.ShapeDtypeStruct((geo.n_b * geo.n_l, GDN_VW), F32),
                   jax.ShapeDtypeStruct((n_layers, geo.n_b) + cv_tail, F32),
                   jax.ShapeDtypeStruct((n_layers, geo.n_b) + tail, F32)],
        scratch_shapes=scratch,
        compiler_params=_params(geo.sem), name="gdn_recur",
    )(*ins)


def _s5_body(*refs, tt, nb, has_init):
    if has_init:
        (x_ref, win_ref, wb_ref, are_ref, aim_ref, wc_ref, d_ref, wglu_ref, bglu_ref, wout_ref,
         lng_ref, lnb_ref, h0re_ref, h0im_ref, y_ref, hre_ref, him_ref,
         x_scr, u_scr, z_scr, bu_scr, y_scr, st_scr) = refs
    else:
        (x_ref, win_ref, wb_ref, are_ref, aim_ref, wc_ref, d_ref, wglu_ref, bglu_ref, wout_ref,
         lng_ref, lnb_ref, y_ref, hre_ref, him_ref,
         x_scr, u_scr, z_scr, bu_scr, y_scr, st_scr) = refs
    tb = pl.program_id(1)
    ns = S5_KT_STATES

    @pl.when(tb == 0)
    def _():
        if not has_init:
            st_scr[...] = jnp.zeros_like(st_scr)
        else:
            for kt in range(S5_KT):
                st_scr[:, 2 * kt * ns:(2 * kt + 1) * ns] = h0re_ref[0, :, kt * ns:(kt + 1) * ns]
                st_scr[:, (2 * kt + 1) * ns:(2 * kt + 2) * ns] = h0im_ref[0, :, kt * ns:(kt + 1) * ns]

    x_scr[...] = jnp.swapaxes(x_ref[...], 0, 1).reshape(tt * nb, D_MODEL)
    h = _dot(_bf(x_scr[...]), win_ref[...])
    u_scr[...] = h[:, 0:S5_WIDTH]
    z_scr[...] = h[:, S5_WIDTH:]

    def input_map(kt):
        cols = slice(kt * S5_KT_W, (kt + 1) * S5_KT_W)
        bu_scr[kt % 2] = _dot(_bf(u_scr[:, cols]), wb_ref[kt])

    def scan(kt):
        bu = bu_scr.at[kt % 2]
        a_re = jnp.broadcast_to(are_ref[kt], (SUBLANES, ns))
        a_im = jnp.broadcast_to(aim_ref[kt], (SUBLANES, ns))
        base = kt * 2 * ns
        for rb in range(nb // SUBLANES):
            st_rows = slice(rb * SUBLANES, (rb + 1) * SUBLANES)
            h_re = st_scr[st_rows, base:base + ns]
            h_im = st_scr[st_rows, base + ns:base + 2 * ns]
            for t in range(tt):
                rows = slice(t * nb + rb * SUBLANES, t * nb + (rb + 1) * SUBLANES)
                h_re, h_im = (a_re * h_re - a_im * h_im + bu[rows, 0:ns],
                              a_re * h_im + a_im * h_re + bu[rows, ns:2 * ns])
                bu[rows, 0:ns] = h_re
                bu[rows, ns:2 * ns] = h_im
            st_scr[st_rows, base:base + ns] = h_re
            st_scr[st_rows, base + ns:base + 2 * ns] = h_im

    def output_map(kt):
        cols = slice(kt * S5_KT_W, (kt + 1) * S5_KT_W)
        y_scr[:, cols] = _dot(_bf(bu_scr[kt % 2]), wc_ref[kt]) + d_ref[:, cols] * u_scr[:, cols]

    input_map(0)
    for kt in range(S5_KT):
        if kt + 1 < S5_KT:
            input_map(kt + 1)
        scan(kt)
        output_map(kt)

    y = _gelu_tanh(y_scr[...])
    yg = _dot(_bf(y), wglu_ref[...]) + bglu_ref[...]
    y = yg[:, 0:S5_WIDTH] * _sigmoid(yg[:, S5_WIDTH:]) * _silu(z_scr[...])
    out = _dot(_bf(y), wout_ref[...])
    y = _layer_norm(ALPHA_RES * x_scr[...] + out, lng_ref[...], lnb_ref[...])
    y_ref[...] = jnp.swapaxes(y.reshape(tt, nb, D_MODEL), 0, 1)

    @pl.when(tb == pl.num_programs(1) - 1)
    def _():
        for kt in range(S5_KT):
            hre_ref[0, :, kt * ns:(kt + 1) * ns] = st_scr[:, 2 * kt * ns:(2 * kt + 1) * ns]
            him_ref[0, :, kt * ns:(kt + 1) * ns] = st_scr[:, (2 * kt + 1) * ns:(2 * kt + 2) * ns]


def _s5_discretize(lam_re, lam_im, log_dt, b_re, b_im, c_re, c_im):
    dt = jnp.exp(log_dt)[:, None]
    mag = jnp.exp(lam_re * dt)
    ab_re, ab_im = mag * jnp.cos(lam_im * dt), mag * jnp.sin(lam_im * dt)
    den = jnp.square(lam_re) + jnp.square(lam_im)
    num_re = ab_re - 1.0
    coef_re = (num_re * lam_re + ab_im * lam_im) / den
    coef_im = (ab_im * lam_re - num_re * lam_im) / den
    bb_re = coef_re[..., None] * b_re - coef_im[..., None] * b_im
    bb_im = coef_re[..., None] * b_im + coef_im[..., None] * b_re
    gl = S5_GROUPS // S5_KT
    eye = jnp.eye(gl, dtype=F32)

    def block_b(bb):
        t = bb.reshape(S5_KT, gl, S5_STATE, S5_GROUP)
        return jnp.einsum("kgpc,gh->kgchp", t, eye).reshape(S5_KT, S5_KT_W, S5_KT_STATES)

    def block_c(cc):
        t = cc.reshape(S5_KT, gl, S5_GROUP, S5_STATE)
        return jnp.einsum("kgcp,gh->kgphc", t, eye).reshape(S5_KT, S5_KT_STATES, S5_KT_W)

    w_b = jnp.concatenate([block_b(bb_re), block_b(bb_im)], axis=2).astype(BF16)
    w_c = jnp.concatenate([block_c(c_re), -block_c(c_im)], axis=1).astype(BF16)
    a_re = ab_re.reshape(S5_KT, 1, S5_KT_STATES)
    a_im = ab_im.reshape(S5_KT, 1, S5_KT_STATES)
    return w_b, w_c, a_re, a_im


def _s5_layer(x, prep, w_in, d_vec, w_glu, b_glu, w_out, ln_g, ln_b, h0, n_b, n_l):
    w_b, w_c, a_re, a_im = prep
    if n_l >= S5_LONG_STEPS:
        nb, tt = n_b, S5_LONG_STEPS
    else:
        nb, tt = min(S5_SHORT_SEQS, n_b), n_l
    assert nb % SUBLANES == 0 and tt % SUBLANES == 0
    rows = tt * nb
    has_init = h0 is not None
    consts = [w_in, w_b, a_re, a_im, w_c, d_vec, w_glu, b_glu, w_out, ln_g, ln_b]
    x_spec = pl.BlockSpec((nb, tt, D_MODEL), lambda b, t: (b, t, 0))
    st_spec = pl.BlockSpec((1, nb, S5_NSTATE), lambda b, t: (0, b, 0))
    ins = [x.reshape(n_b, n_l, D_MODEL)] + consts
    in_specs = [x_spec] + [_const_spec(a) for a in consts]
    if has_init:
        ins += list(h0)
        in_specs += [st_spec, st_spec]
    y, h_re, h_im = pl.pallas_call(
        functools.partial(_s5_body, tt=tt, nb=nb, has_init=has_init),
        grid=(n_b // nb, n_l // tt), in_specs=in_specs,
        out_specs=[x_spec, st_spec, st_spec],
        out_shape=[jax.ShapeDtypeStruct((n_b, n_l, D_MODEL), F32),
                   jax.ShapeDtypeStruct((1, n_b, S5_NSTATE), F32),
                   jax.ShapeDtypeStruct((1, n_b, S5_NSTATE), F32)],
        scratch_shapes=[pltpu.VMEM((rows, D_MODEL), F32),
                        pltpu.VMEM((rows, S5_WIDTH), F32), pltpu.VMEM((rows, S5_WIDTH), F32),
                        pltpu.VMEM((2, rows, 2 * S5_KT_STATES), F32), pltpu.VMEM((rows, S5_WIDTH), F32),
                        pltpu.VMEM((nb, 2 * S5_NSTATE), F32)],
        compiler_params=_params(("parallel", "arbitrary")), name="s5_layer",
    )(*ins)
    return y.reshape(n_b * n_l, D_MODEL), h_re, h_im


def _row2(v):
    return v.reshape(1, -1).astype(F32)


def _pad_cols(w, n):
    return jnp.concatenate([w, jnp.zeros((w.shape[0], n - w.shape[1]), w.dtype)], axis=1)


def _trunk(x, n_b, n_l, states, wts):
    geo = _Geometry(n_b, n_l)
    n_gla, n_gdn, n_s5 = (DEPTH + 2) // 3, (DEPTH + 1) // 3, DEPTH // 3
    assert n_gdn == 1 and n_s5 == 1
    s_gla = s_gdn = s_conv = s_re = s_im = None
    for i in range(DEPTH):
        j, kind = divmod(i, 3)
        ln_g, ln_b = _row2(wts["ln_g"][i]), _row2(wts["ln_b"][i])
        if kind == 0:
            w_in = wts["gla_w_in"][j]
            w_main = _bf(w_in[:, :GLA_MAIN])
            w_lr = _bf(_pad_cols(w_in[:, GLA_MAIN:], LANES))
            w_a2 = _bf(jnp.concatenate(
                [wts["gla_w_a2"][j], jnp.zeros((LANES - GLA_LOWRANK, GLA_KW), F32)], axis=0))
            h, la = _rowwise_call(_gla_proj_body, "gla_proj", [x],
                                  [w_main, w_lr, w_a2, _row2(wts["gla_b_a"][j])], [GLA_MAIN, GLA_KW])
            og, s_gla = _gla_recur(h, la, _row2(wts["gla_norm_g"][j]),
                                   None if states is None else states[0], j, n_gla, s_gla, geo)
            w_out = _bf(wts["gla_w_out"][j])
        elif kind == 1:
            w_in = wts["gdn_w_in"][j]
            w_main = _bf(w_in[:, :GDN_MAIN])
            w_ab = _bf(jnp.concatenate(
                [_pad_cols(w_in[:, GDN_MAIN:GDN_MAIN + GDN_HEADS], LANES),
                 _pad_cols(w_in[:, GDN_MAIN + GDN_HEADS:], LANES)], axis=1))
            h, ab = _rowwise_call(_gdn_proj_body, "gdn_proj", [x], [w_main, w_ab], [GDN_MAIN, 2 * LANES])
            og, s_conv, s_gdn = _gdn_recur(
                h, ab, wts["gdn_w_conv"][j].astype(F32),
                _pad_cols(_row2(wts["gdn_a_log"][j]), LANES), _pad_cols(_row2(wts["gdn_dt_bias"][j]), LANES),
                _row2(wts["gdn_norm_g"][j]),
                None if states is None else states[2], None if states is None else states[1],
                j, n_gdn, geo)
            w_out = _bf(wts["gdn_w_out"][j])
        else:
            prep = _s5_discretize(wts["s5_lam_re"][j].astype(F32), wts["s5_lam_im"][j].astype(F32),
                                  wts["s5_log_dt"][j].astype(F32), wts["s5_b_re"][j].astype(F32),
                                  wts["s5_b_im"][j].astype(F32), wts["s5_c_re"][j].astype(F32),
                                  wts["s5_c_im"][j].astype(F32))
            h0 = None if states is None else (states[3].reshape(n_s5, n_b, S5_NSTATE),
                                              states[4].reshape(n_s5, n_b, S5_NSTATE))
            x, h_re, h_im = _s5_layer(x, prep, _bf(wts["s5_w_in"][j]), _row2(wts["s5_d"][j]),
                                      _bf(wts["s5_w_glu"][j]), _row2(wts["s5_b_glu"][j]),
                                      _bf(wts["s5_w_out"][j]), ln_g, ln_b, h0, n_b, n_l)
            s_re = h_re.reshape(n_s5, n_b, S5_GROUPS, S5_STATE)
            s_im = h_im.reshape(n_s5, n_b, S5_GROUPS, S5_STATE)
            continue
        (x,) = _rowwise_call(_out_ln_body, "out_ln", [og, x], [w_out, ln_g, ln_b], [D_MODEL])
    return x, s_gla, s_gdn, s_conv, s_re, s_im


def kernel(x_prompt, x_sample, state_gla, state_gdn, state_gdn_conv, state_s5_re, state_s5_im,
           ln_g, ln_b, gla_w_in, gla_w_a2, gla_b_a, gla_norm_g, gla_w_out,
           gdn_w_in, gdn_w_conv, gdn_a_log, gdn_dt_bias, gdn_norm_g, gdn_w_out,
           s5_w_in, s5_lam_re, s5_lam_im, s5_log_dt, s5_b_re, s5_b_im, s5_c_re, s5_c_im,
           s5_d, s5_w_glu, s5_b_glu, s5_w_out):
    wts = dict(ln_g=ln_g, ln_b=ln_b,
               gla_w_in=gla_w_in, gla_w_a2=gla_w_a2, gla_b_a=gla_b_a, gla_norm_g=gla_norm_g,
               gla_w_out=gla_w_out,
               gdn_w_in=gdn_w_in, gdn_w_conv=gdn_w_conv, gdn_a_log=gdn_a_log, gdn_dt_bias=gdn_dt_bias,
               gdn_norm_g=gdn_norm_g, gdn_w_out=gdn_w_out,
               s5_w_in=s5_w_in, s5_lam_re=s5_lam_re, s5_lam_im=s5_lam_im, s5_log_dt=s5_log_dt,
               s5_b_re=s5_b_re, s5_b_im=s5_b_im, s5_c_re=s5_c_re, s5_c_im=s5_c_im, s5_d=s5_d,
               s5_w_glu=s5_w_glu, s5_b_glu=s5_b_glu, s5_w_out=s5_w_out)

    bp, lp, _ = x_prompt.shape
    yp, p_gla, p_gdn, p_conv, p_re, p_im = _trunk(
        x_prompt.reshape(bp * lp, D_MODEL), bp, lp, None, wts)
    bs, ls, _ = x_sample.shape
    ys, s_gla, s_gdn, s_conv, s_re, s_im = _trunk(
        x_sample.reshape(bs * ls, D_MODEL), bs, ls,
        (state_gla, state_gdn, state_gdn_conv, state_s5_re, state_s5_im), wts)
    return (yp.reshape(bp, lp, D_MODEL), ys.reshape(bs, ls, D_MODEL),
            p_gla, p_gdn, p_conv, p_re, p_im, s_gla, s_gdn, s_conv, s_re, s_im)
```

```python
import functools
import math

import jax
import jax.numpy as jnp
from jax import lax
from jax.experimental import pallas as pl
from jax.experimental.pallas import tpu as pltpu

F32 = jnp.float32
BF16 = jnp.bfloat16

D_MODEL = 1024
DEPTH = 4
ALPHA_RES = (2 * DEPTH) ** 0.25
LN_EPS = 1e-5
NORM_EPS = 1e-6
CHUNK = 64

GLA_HEADS = 4
GLA_KW = 512
GLA_VW = 1024
GLA_DK = 128
GLA_DV = 256
GLA_LOWRANK = 16
GLA_TAU = 16.0
GLA_MAIN = 2 * GLA_KW + 2 * GLA_VW

GDN_HEADS = 8
GDN_DK = 128
GDN_DV = 128
GDN_KW = 1024
GDN_VW = 1024
GDN_CONV = 4
GDN_CONV_CH = 3072
GDN_MAIN = GDN_CONV_CH + GDN_VW

S5_WIDTH = 1024
S5_GROUP = 16
S5_GROUPS = 64
S5_STATE = 64
S5_KT = 4
S5_KT_W = S5_WIDTH // S5_KT
S5_KT_STATES = (S5_GROUPS // S5_KT) * S5_STATE
S5_NSTATE = S5_GROUPS * S5_STATE

LANES = 128
SUBLANES = 8
VMEM_LIMIT = 56 * 1024 * 1024
ROW_BLOCK = 512
LONG_TIME_BLOCK = 512
S5_LONG_STEPS = 64
S5_SHORT_SEQS = 64


def _bf(x):
    return x.astype(BF16)


def _dot(a, b):
    return jnp.dot(a, b, preferred_element_type=F32)


def _dot_nt(a, b):
    return lax.dot_general(a, b, (((1,), (1,)), ((), ())), preferred_element_type=F32)


def _dot_tn(a, b):
    return lax.dot_general(a, b, (((0,), (0,)), ((), ())), preferred_element_type=F32)


def _split2_bf(x):
    h = x.astype(BF16)
    return h, (x - h.astype(F32)).astype(BF16)


def _split3_bf(x):
    h = x.astype(BF16)
    r = x - h.astype(F32)
    m = r.astype(BF16)
    return h, m, (r - m.astype(F32)).astype(BF16)


def _sigmoid(x):
    return 1.0 / (1.0 + jnp.exp(-x))


def _silu(x):
    return x * _sigmoid(x)


def _softplus(x):
    return jnp.maximum(x, 0.0) + jnp.log(1.0 + jnp.exp(-jnp.abs(x)))


def _gelu_tanh(x):
    return 0.5 * x * (1.0 + jnp.tanh(math.sqrt(2.0 / math.pi) * (x + 0.044715 * (x * x * x))))


def _layer_norm(x, g, b):
    mu = jnp.mean(x, axis=-1, keepdims=True)
    xc = x - mu
    var = jnp.mean(xc * xc, axis=-1, keepdims=True)
    return xc * lax.rsqrt(var + LN_EPS) * g + b


def _cumsum_rows(x, seg):
    row = lax.broadcasted_iota(jnp.int32, x.shape, 0) & (seg - 1)
    s = 1
    while s < seg:
        x = x + jnp.where(row >= s, pltpu.roll(x, s, axis=0), 0.0)
        s *= 2
    return x


def _seg_last_rows(x, seg):
    c, w = x.shape
    if seg == c:
        return jnp.broadcast_to(x[c - 1:c, :], (c, w))
    x3 = x.reshape(c // seg, seg, w)
    return jnp.broadcast_to(x3[:, seg - 1:seg, :], x3.shape).reshape(c, w)


def _seg_masks(shape, seg):
    ri = lax.broadcasted_iota(jnp.int32, shape, 0)
    li = lax.broadcasted_iota(jnp.int32, shape, 1) & (CHUNK - 1)
    shift = int(math.log2(seg))
    same = (ri >> shift) == (li >> shift)
    return same & (ri >= li), same & (ri > li), ri == li


def _aligned(v, m):
    return v if isinstance(v, int) else pl.multiple_of(v, m)


def _const_spec(arr):
    nd = arr.ndim
    return pl.BlockSpec(arr.shape, lambda *_: (0,) * nd, pipeline_mode=pl.Buffered(1))


def _params(sem):
    return pltpu.CompilerParams(dimension_semantics=sem, vmem_limit_bytes=VMEM_LIMIT)


def _rowwise_call(body, name, rows, consts, out_widths):
    n = rows[0].shape[0]
    tm = min(ROW_BLOCK, n)
    spec = lambda w: pl.BlockSpec((tm, w), lambda i: (i, 0))
    return pl.pallas_call(
        body, grid=(n // tm,),
        in_specs=[spec(r.shape[1]) for r in rows] + [_const_spec(c) for c in consts],
        out_specs=[spec(w) for w in out_widths],
        out_shape=[jax.ShapeDtypeStruct((n, w), F32) for w in out_widths],
        compiler_params=_params(("parallel",)), name=name,
    )(*rows, *consts)


def _gla_proj_body(x_ref, w_ref, wlr_ref, wa2_ref, ba_ref, h_ref, la_ref):
    xb = _bf(x_ref[...])
    h = _dot(xb, w_ref[...])
    h_ref[:, :GLA_KW] = h[:, :GLA_KW] * GLA_DK ** -0.5
    h_ref[:, GLA_KW:] = h[:, GLA_KW:]
    lr = _dot(xb, wlr_ref[...])
    pre = _dot(_bf(lr), wa2_ref[...]) + ba_ref[...]
    la_ref[...] = -_softplus(-pre) * (1.0 / GLA_TAU)


def _gdn_proj_body(x_ref, w_ref, wab_ref, h_ref, ab_ref):
    xb = _bf(x_ref[...])
    h_ref[...] = _dot(xb, w_ref[...])
    ab_ref[...] = _dot(xb, wab_ref[...])


def _out_ln_body(og_ref, x_ref, w_ref, g_ref, b_ref, y_ref):
    out = _dot(_bf(og_ref[...]), w_ref[...])
    y_ref[...] = _layer_norm(ALPHA_RES * x_ref[...] + out, g_ref[...], b_ref[...])


class _Geometry:
    def __init__(self, n_b, n_l):
        self.n_b, self.n_l = n_b, n_l
        self.long = n_l >= CHUNK
        if self.long:
            assert n_l % LONG_TIME_BLOCK == 0
            self.seg = CHUNK
            self.tc = LONG_TIME_BLOCK
            self.n_t = n_l // self.tc
            self.grid = (n_b, self.n_t)
            self.sem = ("parallel", "arbitrary")
            self.seqs = 1
            self.row_map = lambda b, t: (b * self.n_t + t, 0)
            self.seq_block = lambda b, t: b
        else:
            assert CHUNK % n_l == 0 and n_l % SUBLANES == 0 and (n_b * n_l) % CHUNK == 0
            self.seg = n_l
            self.tc = CHUNK
            self.grid = (n_b * n_l // self.tc,)
            self.sem = ("parallel",)
            self.seqs = self.tc // n_l
            self.row_map = lambda i: (i, 0)
            self.seq_block = lambda i: i

    def state_spec(self, layer, tail, n_layers=1):
        zeros = (0,) * len(tail)
        first = layer if n_layers == 1 else 0
        return pl.BlockSpec((n_layers, self.seqs) + tail, lambda *g: (first, self.seq_block(*g)) + zeros)

    def last_time_block(self):
        return pl.program_id(1) == pl.num_programs(1) - 1 if self.long else None


def _state_access(geo, s0_ref, sout_ref, s_scr, out_layer=0):
    if geo.long:
        return (lambda sq, hh: s_scr[hh]), (lambda sq, hh, v: s_scr.__setitem__(hh, v))
    return ((lambda sq, hh: s0_ref[0, sq, hh]),
            (lambda sq, hh, v: sout_ref.__setitem__((out_layer, sq, hh), v)))


def _gla_recur_body(*refs, geo, has_alias, layer, out_layers):
    refs = list(refs)
    h_ref, la_ref, ng_ref = refs[:3]
    pos = 3
    s0_ref = x_ref = wout_ref = lng_ref = lnb_ref = None
    if geo.long:
        x_ref, wout_ref, lng_ref, lnb_ref = refs[pos:pos + 4]
        pos += 4
    else:
        s0_ref = refs[pos]
        pos += 1
    if has_alias:
        pos += 1
    og_ref, sout_ref = refs[pos:pos + 2]
    og_scr, s_scr = refs[pos + 2:pos + 4] if geo.long else (og_ref, None)
    c, seg, tc = CHUNK, geo.seg, geo.tc
    n_seg = c // seg
    out_layer = layer if out_layers > 1 else 0
    get_state, put_state = _state_access(geo, s0_ref, sout_ref, s_scr, out_layer)

    def zero_other_layers():
        for other in range(out_layers):
            if other != out_layer:
                sout_ref[other] = jnp.zeros(sout_ref.shape[1:], F32)

    if geo.long:
        @pl.when(pl.program_id(1) == 0)
        def _():
            s_scr[...] = jnp.zeros_like(s_scr)
    else:
        zero_other_layers()

    incl, _, _ = _seg_masks((c, c), seg)
    ng = ng_ref[...]

    def chunk(n, carry):
        r0 = pl.multiple_of(n * c, c)
        rows = pl.ds(r0, c)
        heads = range(GLA_HEADS)
        b = _cumsum_rows(la_ref[rows, :], seg)
        bl = _seg_last_rows(b, seg)
        k_all = h_ref[rows, GLA_KW:2 * GLA_KW]
        q_dec = h_ref[rows, 0:GLA_KW] * jnp.exp(b)
        k_inv = k_all * jnp.exp(-b)
        k_end = k_all * jnp.exp(bl - b)
        e_bl = jnp.exp(bl)
        kq = [slice(hh * GLA_DK, (hh + 1) * GLA_DK) for hh in heads]
        vs = [h_ref[rows, 2 * GLA_KW + hh * GLA_DV:2 * GLA_KW + (hh + 1) * GLA_DV] for hh in heads]
        atts = [jnp.where(incl, _dot_nt(_bf(q_dec[:, kq[hh]]), _bf(k_inv[:, kq[hh]])), 0.0) for hh in heads]
        os_ = [_dot(_bf(atts[hh]), _bf(vs[hh])) for hh in heads]
        inter = [[] for _ in heads]
        for s in range(n_seg):
            sq = n * n_seg + s
            sr = slice(s * seg, (s + 1) * seg)
            sts = [get_state(sq, hh) for hh in heads]
            for hh in heads:
                inter[hh].append(_dot(_bf(q_dec[sr, kq[hh]]), _bf(sts[hh])))
            upd = [_dot_tn(_bf(k_end[sr, kq[hh]]), _bf(vs[hh][sr])) for hh in heads]
            for hh in heads:
                ecol = jnp.transpose(jnp.broadcast_to(e_bl[s * seg:s * seg + 1, kq[hh]], (GLA_DK, GLA_DK)))
                ecol = jnp.concatenate([ecol] * (GLA_DV // GLA_DK), axis=1)
                put_state(sq, hh, sts[hh] * ecol + upd[hh])
        for hh in heads:
            o = os_[hh] + (inter[hh][0] if n_seg == 1 else jnp.concatenate(inter[hh], axis=0))
            o = o * lax.rsqrt(jnp.mean(o * o, axis=-1, keepdims=True) + NORM_EPS) * ng
            kr = slice(2 * GLA_KW + GLA_VW + hh * GLA_DV, 2 * GLA_KW + GLA_VW + (hh + 1) * GLA_DV)
            og_scr[rows, hh * GLA_DV:(hh + 1) * GLA_DV] = o * _silu(h_ref[rows, kr])
        return carry

    lax.fori_loop(0, tc // c, chunk, 0)

    if geo.long:
        _out_ln_body(og_scr, x_ref, wout_ref, lng_ref, lnb_ref, og_ref)

        @pl.when(geo.last_time_block())
        def _():
            sout_ref[out_layer, 0] = s_scr[...]
            zero_other_layers()


def _gla_recur(h, la, norm_g, state_in, layer, n_layers, prev_out, geo, out_ln):
    tail = (GLA_HEADS, GLA_DK, GLA_DV)
    ins = [h, la, norm_g]
    in_specs = [pl.BlockSpec((geo.tc, GLA_MAIN), geo.row_map), pl.BlockSpec((geo.tc, GLA_KW), geo.row_map),
                _const_spec(norm_g)]
    if geo.long:
        ins += list(out_ln)
        in_specs += [pl.BlockSpec((geo.tc, D_MODEL), geo.row_map)] + [_const_spec(a) for a in out_ln[1:]]
    else:
        ins.append(state_in)
        in_specs.append(geo.state_spec(layer, tail))
    aliases = {}
    if prev_out is not None:
        aliases = {len(ins): 1}
        ins.append(prev_out)
        in_specs.append(pl.BlockSpec(memory_space=pl.ANY))
    scratch = [pltpu.VMEM((geo.tc, GLA_VW), F32), pltpu.VMEM(tail, F32)] if geo.long else []
    out_layers = n_layers if prev_out is None else 1
    og, s_out = pl.pallas_call(
        functools.partial(_gla_recur_body, geo=geo, has_alias=prev_out is not None,
                          layer=layer, out_layers=out_layers),
        grid=geo.grid, in_specs=in_specs,
        out_specs=[pl.BlockSpec((geo.tc, GLA_VW), geo.row_map), geo.state_spec(layer, tail, out_layers)],
        out_shape=[jax.ShapeDtypeStruct((geo.n_b * geo.n_l, GLA_VW), F32),
                   jax.ShapeDtypeStruct((n_layers, geo.n_b) + tail, F32)],
        scratch_shapes=scratch, input_output_aliases=aliases,
        compiler_params=_params(geo.sem), name="gla_recur",
    )(*ins)
    return og, s_out


def _block_rows(x, mask):
    return jnp.where(mask, jnp.concatenate([x, x], axis=0), 0.0)


def _mm3_split(lhs, rhs_hi, rhs_lo):
    lh, ll = _split2_bf(lhs)
    return _dot(lh, rhs_hi) + (_dot(lh, rhs_lo) + _dot(ll, rhs_hi))


def _mm3_pair(lhs, x_pair, mask):
    xh, xl = _split2_bf(x_pair)
    return _mm3_split(lhs, _bf(_block_rows(xh.astype(F32), mask)), _bf(_block_rows(xl.astype(F32), mask)))


def _gdn_expand_matrix():
    h = jnp.arange(LANES)[:, None]
    e64 = (jnp.arange(GDN_HEADS * CHUNK)[None, :] // CHUNK == h)
    e128 = (jnp.arange(GDN_HEADS * GDN_DK)[None, :] // GDN_DK == h)
    return jnp.concatenate([e64, e128], axis=1).astype(BF16)


def _gdn_recur_body(*refs, geo):
    refs = list(refs)
    h_ref, ab_ref, wc_ref, alog_ref, dtb_ref, ng_ref, eall_ref = refs[:7]
    pos = 7
    conv0_ref = s0_ref = x_ref = wout_ref = lng_ref = lnb_ref = None
    if geo.long:
        x_ref, wout_ref, lng_ref, lnb_ref = refs[pos:pos + 4]
        pos += 4
    else:
        conv0_ref, s0_ref = refs[pos:pos + 2]
        pos += 2
    og_ref, convn_ref, sout_ref = refs[pos:pos + 3]
    pos += 3
    qkv_scr, g_scr, beta_scr, u_scr, w_scr, att_scr, egc_scr, kend_scr, qk_scr = refs[pos:pos + 9]
    pos += 9
    carry_scr = s_scr = None
    og_scr = og_ref
    if geo.long:
        carry_scr, og_scr, s_scr = refs[pos:pos + 3]
    c, seg, tc = CHUNK, geo.seg, geo.tc
    n_seg = c // seg
    get_state, put_state = _state_access(geo, s0_ref, sout_ref, s_scr)
    pad = SUBLANES
    lo = pad - (GDN_CONV - 1)
    n_pair = GDN_HEADS // 2
    pw = 2 * c
    hw = GDN_HEADS * c
    cpi = 2 if (tc // c) % 2 == 0 else 1

    if geo.long:
        @pl.when(pl.program_id(1) == 0)
        def _():
            carry_scr[...] = jnp.zeros_like(carry_scr)
            s_scr[...] = jnp.zeros_like(s_scr)

    def window_conv(e, cols, t_in=None):
        conv = e[pad:pad + c] * wc_ref[GDN_CONV - 1:GDN_CONV, cols]
        for j in range(GDN_CONV - 1):
            tap = e[lo + j:lo + j + c]
            if t_in is not None:
                tap = jnp.where(t_in >= GDN_CONV - 1 - j, tap, 0.0)
            conv = conv + tap * wc_ref[j:j + 1, cols]
        return conv

    def conv_chunk(m):
        static = isinstance(m, int)
        r0 = m * c if static else pl.multiple_of(m * c, c)
        for cb in range(GDN_CONV_CH // LANES):
            cols = slice(cb * LANES, (cb + 1) * LANES)
            if static and m == 0:
                e = jnp.concatenate([carry_scr[:, cols], h_ref[0:c, cols]], axis=0)
            elif static:
                e = h_ref[r0 - pad:r0 + c, cols]
            else:
                e = h_ref[pl.ds(pl.multiple_of(m * c - pad, SUBLANES), c + pad), cols]
            qkv_scr[pl.ds(r0, c), cols] = _silu(window_conv(e, cols))

    if geo.long:
        for i in range(cpi):
            conv_chunk(i)

        @pl.when(geo.last_time_block())
        def _():
            convn_ref[0, 0] = h_ref[tc - (GDN_CONV - 1):tc, 0:GDN_CONV_CH]
    else:
        t_in = lax.broadcasted_iota(jnp.int32, (c, LANES), 0) & (seg - 1)
        for cb in range(GDN_CONV_CH // LANES):
            cols = slice(cb * LANES, (cb + 1) * LANES)
            e = jnp.concatenate([jnp.zeros((pad, LANES), F32), h_ref[:, cols]], axis=0)
            qkv_scr[:, cols] = window_conv(e, cols, t_in)
        for sq in range(geo.seqs):
            c0 = conv0_ref[0, sq]
            head = []
            for t in range(GDN_CONV - 1):
                acc = c0[t:t + 1, :] * wc_ref[0:1, :]
                for j in range(1, GDN_CONV - 1 - t):
                    acc = acc + c0[t + j:t + j + 1, :] * wc_ref[j:j + 1, :]
                head.append(acc)
            head.append(jnp.zeros((SUBLANES - (GDN_CONV - 1), GDN_CONV_CH), F32))
            first = slice(sq * seg, sq * seg + SUBLANES)
            qkv_scr[first, :] = qkv_scr[first, :] + jnp.concatenate(head, axis=0)
            convn_ref[0, sq] = h_ref[(sq + 1) * seg - (GDN_CONV - 1):(sq + 1) * seg, 0:GDN_CONV_CH]
        qkv_scr[...] = _silu(qkv_scr[...])

    g_scr[...] = -jnp.exp(alog_ref[...]) * _softplus(ab_ref[:, 0:LANES] + dtb_ref[...])
    beta_scr[...] = _sigmoid(ab_ref[:, LANES:2 * LANES])

    incl, strict, delta = _seg_masks((c, hw), seg)
    eye_pair = jnp.where(delta[:, 0:pw], 1.0, 0.0)
    r2 = lax.broadcasted_iota(jnp.int32, (2 * c, pw), 0) >> 6
    mask_cc = r2 == (lax.broadcasted_iota(jnp.int32, (2 * c, pw), 1) >> 6)
    r3 = lax.broadcasted_iota(jnp.int32, (2 * c, 2 * GDN_DK), 0) >> 6
    mask_cd = r3 == (lax.broadcasted_iota(jnp.int32, (2 * c, 2 * GDN_DK), 1) >> 7)
    ones_c = jnp.ones((c, c), BF16)
    ng = ng_ref[...]

    def gate_forms(rows):
        gc = _cumsum_rows(g_scr[rows, :], seg)
        e_all = eall_ref[...]
        sh, sm, sl = _split3_bf(jnp.concatenate([gc, beta_scr[rows, :]], axis=0))
        ex = _dot(sh, e_all) + (_dot(sm, e_all) + _dot(sl, e_all))
        gcol, bcol, gwide = ex[0:c, 0:hw], ex[c:2 * c, 0:hw], ex[0:c, hw:]
        th, tm, tl = _split3_bf(jnp.concatenate(
            [jnp.where(delta, gcol, 0.0), jnp.where(delta, bcol, 0.0)], axis=1))
        rowf = _dot(ones_c, th) + (_dot(ones_c, tm) + _dot(ones_c, tl))
        grow, brow = rowf[:, 0:hw], rowf[:, hw:]
        dec = jnp.where(incl, jnp.exp(jnp.where(incl, gcol - grow, 0.0)), 0.0)
        egc_scr[rows, :] = jnp.exp(gwide)
        eend = jnp.exp(_seg_last_rows(gwide, seg) - gwide)
        return dec, bcol, brow, jnp.exp(grow), eend

    def local(n, side=None):
        tick = (lambda: None) if side is None else (lambda: next(side, None))
        chunk_rows = [pl.ds(_aligned((n * cpi + i) * c, c), c) for i in range(cpi)]
        forms = [gate_forms(rows) for rows in chunk_rows]
        units = [(i, p) for i in range(cpi) for p in range(n_pair)]
        tick()

        a_pairs = []
        for i, p in units:
            rows = chunk_rows[i]
            dec, bcol, _, _, eend = forms[i]
            qs, ks = [], []
            for hh in (2 * p, 2 * p + 1):
                kq = slice(hh * GDN_DK, (hh + 1) * GDN_DK)
                kk = slice(GDN_KW + hh * GDN_DK, GDN_KW + (hh + 1) * GDN_DK)
                q = qkv_scr[rows, kq]
                k = qkv_scr[rows, kk]
                q = q * lax.rsqrt(jnp.sum(q * q, axis=-1, keepdims=True) + NORM_EPS) * GDN_DK ** -0.5
                k = k * lax.rsqrt(jnp.sum(k * k, axis=-1, keepdims=True) + NORM_EPS)
                qk_scr[rows, kq] = q
                qk_scr[rows, kk] = k
                kend_scr[rows, kq] = k * eend[:, kq]
                qs.append(q)
                ks.append(k)
            kh = _bf(jnp.concatenate(ks, axis=1))
            kbd_h = _bf(_block_rows(kh.astype(F32), mask_cd))
            kkqk = _dot_nt(jnp.concatenate([kh, _bf(jnp.concatenate(qs, axis=1))], axis=0), kbd_h)
            cs = slice(p * pw, (p + 1) * pw)
            a_pairs.append(jnp.where(strict[:, cs], kkqk[0:c] * dec[:, cs] * bcol[:, cs], 0.0))
            att_scr[rows, cs] = kkqk[c:2 * c] * dec[:, cs]
        tick()
        ps = [eye_pair - a for a in a_pairs]
        xs = [_mm3_pair(a, a, mask_cc) for a in a_pairs]
        tick()
        steps = int(math.log2(seg)) - 1
        for s in range(steps):
            if s < steps - 1:
                rs = [_mm3_pair(jnp.concatenate([pp, x], axis=0), x, mask_cc) for pp, x in zip(ps, xs)]
                ps = [pp + r[0:c] for pp, r in zip(ps, rs)]
                xs = [r[c:2 * c] for r in rs]
            else:
                ps = [pp + _mm3_pair(pp, x, mask_cc) for pp, x in zip(ps, xs)]
            tick()
        for t_inv, (i, p) in zip(ps, units):
            rows = chunk_rows[i]
            _, _, brow, egrow, _ = forms[i]
            cs = slice(p * pw, (p + 1) * pw)
            vs = slice(p * 2 * GDN_DV, (p + 1) * 2 * GDN_DV)
            t_b = t_inv * brow[:, cs]
            vh, vl = _split2_bf(qkv_scr[rows, 2 * GDN_KW + p * 2 * GDN_DV:2 * GDN_KW + (p + 1) * 2 * GDN_DV])
            u_scr[rows, vs] = _mm3_split(t_b, _bf(_block_rows(vh.astype(F32), mask_cd)),
                                         _bf(_block_rows(vl.astype(F32), mask_cd)))
            kh, kl = _split2_bf(qk_scr[rows, GDN_KW + p * 2 * GDN_DK:GDN_KW + (p + 1) * 2 * GDN_DK])
            w_scr[rows, vs] = _mm3_split(t_b * egrow[:, cs], _bf(_block_rows(kh.astype(F32), mask_cd)),
                                         _bf(_block_rows(kl.astype(F32), mask_cd)))
        if geo.long:
            for i in range(cpi):
                nxt, cap = (n + 1) * cpi + i, tc // c - cpi + i
                conv_chunk(min(nxt, cap) if isinstance(n, int) else jnp.minimum(nxt, cap))
        if side is not None:
            for _ in side:
                pass

    def recur(n):
        r0 = _aligned(n * c, c)
        rows = pl.ds(r0, c)
        heads = range(GDN_HEADS)
        kqs = [slice(hh * GDN_DK, (hh + 1) * GDN_DK) for hh in heads]
        vn_seg = [[] for _ in heads]
        qs_seg = [[] for _ in heads]
        for s in range(n_seg):
            sq = n * n_seg + s
            sr = pl.ds(_aligned(r0 + s * seg, SUBLANES), seg)
            tail = pl.ds(_aligned(r0 + (s + 1) * seg - SUBLANES, SUBLANES), SUBLANES)
            sts = [get_state(sq, hh) for hh in heads]
            m1s = [_dot(_bf(jnp.concatenate([w_scr[sr, kqs[hh]], qk_scr[sr, kqs[hh]]], axis=0)), _bf(sts[hh]))
                   for hh in heads]
            yield
            vn = [u_scr[sr, kqs[hh]] - m1s[hh][0:seg] for hh in heads]
            upd = [_dot_tn(_bf(kend_scr[sr, kqs[hh]]), _bf(vn[hh])) for hh in heads]
            yield
            for hh in heads:
                e_last = egc_scr[tail, kqs[hh]][SUBLANES - 1:SUBLANES, :]
                put_state(sq, hh, sts[hh] * e_last + upd[hh])
                vn_seg[hh].append(vn[hh])
                qs_seg[hh].append(m1s[hh][seg:2 * seg])
        join = lambda parts: parts[0] if n_seg == 1 else jnp.concatenate(parts, axis=0)
        o_pairs = []
        for p in range(n_pair):
            pair = (2 * p, 2 * p + 1)
            vs = slice(p * 2 * GDN_DV, (p + 1) * 2 * GDN_DV)
            vbd = _bf(_block_rows(jnp.concatenate([join(vn_seg[hh]) for hh in pair], axis=1), mask_cd))
            o_pairs.append(_dot(_bf(att_scr[rows, p * pw:(p + 1) * pw]), vbd)
                           + egc_scr[rows, vs] * jnp.concatenate([join(qs_seg[hh]) for hh in pair], axis=1))
        yield
        for p in range(n_pair):
            for i, hh in enumerate((2 * p, 2 * p + 1)):
                o = o_pairs[p][:, i * GDN_DV:(i + 1) * GDN_DV]
                o = o * lax.rsqrt(jnp.mean(o * o, axis=-1, keepdims=True) + NORM_EPS) * ng
                kz = slice(GDN_CONV_CH + hh * GDN_DV, GDN_CONV_CH + (hh + 1) * GDN_DV)
                og_scr[rows, hh * GDN_DV:(hh + 1) * GDN_DV] = o * _silu(h_ref[rows, kz])
        yield

    def recur_trip(n):
        for i in range(cpi):
            yield from recur(n * cpi + i)

    def drain(gen):
        for _ in gen:
            pass

    n_trip = tc // (c * cpi)
    local(0)

    def trip(n, carry):
        local(n, recur_trip(n - 1))
        return carry

    lax.fori_loop(1, n_trip, trip, 0)
    drain(recur_trip(n_trip - 1))

    if geo.long:
        carry_scr[...] = h_ref[tc - pad:tc, 0:GDN_CONV_CH]
        _out_ln_body(og_scr, x_ref, wout_ref, lng_ref, lnb_ref, og_ref)

        @pl.when(geo.last_time_block())
        def _():
            sout_ref[0, 0] = s_scr[...]


def _gdn_recur(h, ab, w_conv, a_log, dt_bias, norm_g, conv_in, state_in, layer, n_layers, geo, out_ln):
    tail = (GDN_HEADS, GDN_DK, GDN_DV)
    cv_tail = (GDN_CONV - 1, GDN_CONV_CH)
    consts = [w_conv, a_log, dt_bias, norm_g, _gdn_expand_matrix()]
    ins = [h, ab] + consts
    in_specs = [pl.BlockSpec((geo.tc, GDN_MAIN), geo.row_map), pl.BlockSpec((geo.tc, 2 * LANES), geo.row_map)]
    in_specs += [_const_spec(a) for a in consts]
    tc = geo.tc
    if geo.long:
        ins += list(out_ln)
        in_specs += [pl.BlockSpec((tc, D_MODEL), geo.row_map)] + [_const_spec(a) for a in out_ln[1:]]
    else:
        assert tc == CHUNK
        ins += [conv_in, state_in]
        in_specs += [geo.state_spec(layer, cv_tail), geo.state_spec(layer, tail)]
    scratch = [pltpu.VMEM((tc, GDN_CONV_CH), F32),
               pltpu.VMEM((tc, LANES), F32), pltpu.VMEM((tc, LANES), F32),
               pltpu.VMEM((tc, GDN_VW), F32), pltpu.VMEM((tc, GDN_KW), F32),
               pltpu.VMEM((tc, GDN_HEADS * CHUNK), F32), pltpu.VMEM((tc, GDN_KW), F32),
               pltpu.VMEM((tc, GDN_KW), F32), pltpu.VMEM((tc, 2 * GDN_KW), F32)]
    if geo.long:
        scratch += [pltpu.VMEM((SUBLANES, GDN_CONV_CH), F32), pltpu.VMEM((tc, GDN_VW), F32),
                    pltpu.VMEM(tail, F32)]
    return pl.pallas_call(
        functools.partial(_gdn_recur_body, geo=geo),
        grid=geo.grid, in_specs=in_specs,
        out_specs=[pl.BlockSpec((tc, GDN_VW), geo.row_map), geo.state_spec(layer, cv_tail),
                   geo.state_spec(layer, tail)],
        out_shape=[jax.ShapeDtypeStruct((geo.n_b * geo.n_l, GDN_VW), F32),
                   jax.ShapeDtypeStruct((n_layers, geo.n_b) + cv_tail, F32),
                   jax.ShapeDtypeStruct((n_layers, geo.n_b) + tail, F32)],
        scratch_shapes=scratch,
        compiler_params=_params(geo.sem), name="gdn_recur",
    )(*ins)


def _s5_body(*refs, tt, nb, has_init):
    if has_init:
        (x_ref, win_ref, wb_ref, are_ref, aim_ref, wc_ref, d_ref, wglu_ref, bglu_ref, wout_ref,
         lng_ref, lnb_ref, h0re_ref, h0im_ref, y_ref, hre_ref, him_ref,
         x_scr, u_scr, z_scr, bu_scr, y_scr, st_scr) = refs
    else:
        (x_ref, win_ref, wb_ref, are_ref, aim_ref, wc_ref, d_ref, wglu_ref, bglu_ref, wout_ref,
         lng_ref, lnb_ref, y_ref, hre_ref, him_ref,
         x_scr, u_scr, z_scr, bu_scr, y_scr, st_scr) = refs
    tb = pl.program_id(1)
    ns = S5_KT_STATES

    @pl.when(tb == 0)
    def _():
        if not has_init:
            st_scr[...] = jnp.zeros_like(st_scr)
        else:
            for kt in range(S5_KT):
                st_scr[:, 2 * kt * ns:(2 * kt + 1) * ns] = h0re_ref[0, :, kt * ns:(kt + 1) * ns]
                st_scr[:, (2 * kt + 1) * ns:(2 * kt + 2) * ns] = h0im_ref[0, :, kt * ns:(kt + 1) * ns]

    x_scr[...] = jnp.swapaxes(x_ref[...], 0, 1).reshape(tt * nb, D_MODEL)
    h = _dot(_bf(x_scr[...]), win_ref[...])
    u_scr[...] = h[:, 0:S5_WIDTH]
    z_scr[...] = h[:, S5_WIDTH:]

    def input_map(kt):
        cols = slice(kt * S5_KT_W, (kt + 1) * S5_KT_W)
        bu_scr[kt % 2] = _dot(_bf(u_scr[:, cols]), wb_ref[kt])

    def scan(kt):
        bu = bu_scr.at[kt % 2]
        a_re = jnp.broadcast_to(are_ref[kt], (SUBLANES, ns))
        a_im = jnp.broadcast_to(aim_ref[kt], (SUBLANES, ns))
        base = kt * 2 * ns
        for rb in range(nb // SUBLANES):
            st_rows = slice(rb * SUBLANES, (rb + 1) * SUBLANES)
            h_re = st_scr[st_rows, base:base + ns]
            h_im = st_scr[st_rows, base + ns:base + 2 * ns]
            for t in range(tt):
                rows = slice(t * nb + rb * SUBLANES, t * nb + (rb + 1) * SUBLANES)
                h_re, h_im = (a_re * h_re - a_im * h_im + bu[rows, 0:ns],
                              a_re * h_im + a_im * h_re + bu[rows, ns:2 * ns])
                bu[rows, 0:ns] = h_re
                bu[rows, ns:2 * ns] = h_im
            st_scr[st_rows, base:base + ns] = h_re
            st_scr[st_rows, base + ns:base + 2 * ns] = h_im

    def output_map(kt):
        cols = slice(kt * S5_KT_W, (kt + 1) * S5_KT_W)
        y_scr[:, cols] = _dot(_bf(bu_scr[kt % 2]), wc_ref[kt]) + d_ref[:, cols] * u_scr[:, cols]

    input_map(0)
    for kt in range(S5_KT):
        if kt + 1 < S5_KT:
            input_map(kt + 1)
        scan(kt)
        output_map(kt)

    y = _gelu_tanh(y_scr[...])
    yg = _dot(_bf(y), wglu_ref[...]) + bglu_ref[...]
    y = yg[:, 0:S5_WIDTH] * _sigmoid(yg[:, S5_WIDTH:]) * _silu(z_scr[...])
    out = _dot(_bf(y), wout_ref[...])
    y = _layer_norm(ALPHA_RES * x_scr[...] + out, lng_ref[...], lnb_ref[...])
    y_ref[...] = jnp.swapaxes(y.reshape(tt, nb, D_MODEL), 0, 1)

    @pl.when(tb == pl.num_programs(1) - 1)
    def _():
        for kt in range(S5_KT):
            hre_ref[0, :, kt * ns:(kt + 1) * ns] = st_scr[:, 2 * kt * ns:(2 * kt + 1) * ns]
            him_ref[0, :, kt * ns:(kt + 1) * ns] = st_scr[:, (2 * kt + 1) * ns:(2 * kt + 2) * ns]


def _s5_discretize(lam_re, lam_im, log_dt, b_re, b_im, c_re, c_im):
    dt = jnp.exp(log_dt)[:, None]
    mag = jnp.exp(lam_re * dt)
    ab_re, ab_im = mag * jnp.cos(lam_im * dt), mag * jnp.sin(lam_im * dt)
    den = jnp.square(lam_re) + jnp.square(lam_im)
    num_re = ab_re - 1.0
    coef_re = (num_re * lam_re + ab_im * lam_im) / den
    coef_im = (ab_im * lam_re - num_re * lam_im) / den
    bb_re = coef_re[..., None] * b_re - coef_im[..., None] * b_im
    bb_im = coef_re[..., None] * b_im + coef_im[..., None] * b_re
    gl = S5_GROUPS // S5_KT
    eye = jnp.eye(gl, dtype=F32)

    def block_b(bb):
        t = bb.reshape(S5_KT, gl, S5_STATE, S5_GROUP)
        return jnp.einsum("kgpc,gh->kgchp", t, eye).reshape(S5_KT, S5_KT_W, S5_KT_STATES)

    def block_c(cc):
        t = cc.reshape(S5_KT, gl, S5_GROUP, S5_STATE)
        return jnp.einsum("kgcp,gh->kgphc", t, eye).reshape(S5_KT, S5_KT_STATES, S5_KT_W)

    w_b = jnp.concatenate([block_b(bb_re), block_b(bb_im)], axis=2).astype(BF16)
    w_c = jnp.concatenate([block_c(c_re), -block_c(c_im)], axis=1).astype(BF16)
    a_re = ab_re.reshape(S5_KT, 1, S5_KT_STATES)
    a_im = ab_im.reshape(S5_KT, 1, S5_KT_STATES)
    return w_b, w_c, a_re, a_im


def _s5_layer(x, prep, w_in, d_vec, w_glu, b_glu, w_out, ln_g, ln_b, h0, n_b, n_l):
    w_b, w_c, a_re, a_im = prep
    if n_l >= S5_LONG_STEPS:
        nb, tt = n_b, S5_LONG_STEPS
    else:
        nb, tt = min(S5_SHORT_SEQS, n_b), n_l
    assert nb % SUBLANES == 0 and tt % SUBLANES == 0
    rows = tt * nb
    has_init = h0 is not None
    consts = [w_in, w_b, a_re, a_im, w_c, d_vec, w_glu, b_glu, w_out, ln_g, ln_b]
    x_spec = pl.BlockSpec((nb, tt, D_MODEL), lambda b, t: (b, t, 0))
    st_spec = pl.BlockSpec((1, nb, S5_NSTATE), lambda b, t: (0, b, 0))
    ins = [x.reshape(n_b, n_l, D_MODEL)] + consts
    in_specs = [x_spec] + [_const_spec(a) for a in consts]
    if has_init:
        ins += list(h0)
        in_specs += [st_spec, st_spec]
    y, h_re, h_im = pl.pallas_call(
        functools.partial(_s5_body, tt=tt, nb=nb, has_init=has_init),
        grid=(n_b // nb, n_l // tt), in_specs=in_specs,
        out_specs=[x_spec, st_spec, st_spec],
        out_shape=[jax.ShapeDtypeStruct((n_b, n_l, D_MODEL), F32),
                   jax.ShapeDtypeStruct((1, n_b, S5_NSTATE), F32),
                   jax.ShapeDtypeStruct((1, n_b, S5_NSTATE), F32)],
        scratch_shapes=[pltpu.VMEM((rows, D_MODEL), F32),
                        pltpu.VMEM((rows, S5_WIDTH), F32), pltpu.VMEM((rows, S5_WIDTH), F32),
                        pltpu.VMEM((2, rows, 2 * S5_KT_STATES), F32), pltpu.VMEM((rows, S5_WIDTH), F32),
                        pltpu.VMEM((nb, 2 * S5_NSTATE), F32)],
        compiler_params=_params(("parallel", "arbitrary")), name="s5_layer",
    )(*ins)
    return y.reshape(n_b * n_l, D_MODEL), h_re, h_im


def _row2(v):
    return v.reshape(1, -1).astype(F32)


def _pad_cols(w, n):
    return jnp.concatenate([w, jnp.zeros((w.shape[0], n - w.shape[1]), w.dtype)], axis=1)


def _trunk(x, n_b, n_l, states, wts):
    geo = _Geometry(n_b, n_l)
    n_gla, n_gdn, n_s5 = (DEPTH + 2) // 3, (DEPTH + 1) // 3, DEPTH // 3
    assert n_gdn == 1 and n_s5 == 1
    s_gla = s_gdn = s_conv = s_re = s_im = None
    for i in range(DEPTH):
        j, kind = divmod(i, 3)
        ln_g, ln_b = _row2(wts["ln_g"][i]), _row2(wts["ln_b"][i])
        if kind == 0:
            w_in = wts["gla_w_in"][j]
            w_main = _bf(w_in[:, :GLA_MAIN])
            w_lr = _bf(_pad_cols(w_in[:, GLA_MAIN:], LANES))
            w_a2 = _bf(jnp.concatenate(
                [wts["gla_w_a2"][j], jnp.zeros((LANES - GLA_LOWRANK, GLA_KW), F32)], axis=0))
            h, la = _rowwise_call(_gla_proj_body, "gla_proj", [x],
                                  [w_main, w_lr, w_a2, _row2(wts["gla_b_a"][j])], [GLA_MAIN, GLA_KW])
            w_out = _bf(wts["gla_w_out"][j])
            og, s_gla = _gla_recur(h, la, _row2(wts["gla_norm_g"][j]),
                                   None if states is None else states[0], j, n_gla, s_gla, geo,
                                   (x, w_out, ln_g, ln_b))
        elif kind == 1:
            w_in = wts["gdn_w_in"][j]
            w_main = _bf(w_in[:, :GDN_MAIN])
            w_ab = _bf(jnp.concatenate(
                [_pad_cols(w_in[:, GDN_MAIN:GDN_MAIN + GDN_HEADS], LANES),
                 _pad_cols(w_in[:, GDN_MAIN + GDN_HEADS:], LANES)], axis=1))
            h, ab = _rowwise_call(_gdn_proj_body, "gdn_proj", [x], [w_main, w_ab], [GDN_MAIN, 2 * LANES])
            w_out = _bf(wts["gdn_w_out"][j])
            og, s_conv, s_gdn = _gdn_recur(
                h, ab, wts["gdn_w_conv"][j].astype(F32),
                _pad_cols(_row2(wts["gdn_a_log"][j]), LANES), _pad_cols(_row2(wts["gdn_dt_bias"][j]), LANES),
                _row2(wts["gdn_norm_g"][j]),
                None if states is None else states[2], None if states is None else states[1],
                j, n_gdn, geo, (x, w_out, ln_g, ln_b))
        else:
            prep = _s5_discretize(wts["s5_lam_re"][j].astype(F32), wts["s5_lam_im"][j].astype(F32),
                                  wts["s5_log_dt"][j].astype(F32), wts["s5_b_re"][j].astype(F32),
                                  wts["s5_b_im"][j].astype(F32), wts["s5_c_re"][j].astype(F32),
                                  wts["s5_c_im"][j].astype(F32))
            h0 = None if states is None else (states[3].reshape(n_s5, n_b, S5_NSTATE),
                                              states[4].reshape(n_s5, n_b, S5_NSTATE))
            x, h_re, h_im = _s5_layer(x, prep, _bf(wts["s5_w_in"][j]), _row2(wts["s5_d"][j]),
                                      _bf(wts["s5_w_glu"][j]), _row2(wts["s5_b_glu"][j]),
                                      _bf(wts["s5_w_out"][j]), ln_g, ln_b, h0, n_b, n_l)
            s_re = h_re.reshape(n_s5, n_b, S5_GROUPS, S5_STATE)
            s_im = h_im.reshape(n_s5, n_b, S5_GROUPS, S5_STATE)
            continue
        if geo.long:
            x = og
        else:
            (x,) = _rowwise_call(_out_ln_body, "out_ln", [og, x], [w_out, ln_g, ln_b], [D_MODEL])
    return x, s_gla, s_gdn, s_conv, s_re, s_im


def kernel(x_prompt, x_sample, state_gla, state_gdn, state_gdn_conv, state_s5_re, state_s5_im,
           ln_g, ln_b, gla_w_in, gla_w_a2, gla_b_a, gla_norm_g, gla_w_out,
           gdn_w_in, gdn_w_conv, gdn_a_log, gdn_dt_bias, gdn_norm_g, gdn_w_out,
           s5_w_in, s5_lam_re, s5_lam_im, s5_log_dt, s5_b_re, s5_b_im, s5_c_re, s5_c_im,
           s5_d, s5_w_glu, s5_b_glu, s5_w_out):
    wts = dict(ln_g=ln_g, ln_b=ln_b,
               gla_w_in=gla_w_in, gla_w_a2=gla_w_a2, gla_b_a=gla_b_a, gla_norm_g=gla_norm_g,
               gla_w_out=gla_w_out,
               gdn_w_in=gdn_w_in, gdn_w_conv=gdn_w_conv, gdn_a_log=gdn_a_log, gdn_dt_bias=gdn_dt_bias,
               gdn_norm_g=gdn_norm_g, gdn_w_out=gdn_w_out,
               s5_w_in=s5_w_in, s5_lam_re=s5_lam_re, s5_lam_im=s5_lam_im, s5_log_dt=s5_log_dt,
               s5_b_re=s5_b_re, s5_b_im=s5_b_im, s5_c_re=s5_c_re, s5_c_im=s5_c_im, s5_d=s5_d,
               s5_w_glu=s5_w_glu, s5_b_glu=s5_b_glu, s5_w_out=s5_w_out)

    bp, lp, _ = x_prompt.shape
    yp, p_gla, p_gdn, p_conv, p_re, p_im = _trunk(
        x_prompt.reshape(bp * lp, D_MODEL), bp, lp, None, wts)
    bs, ls, _ = x_sample.shape
    ys, s_gla, s_gdn, s_conv, s_re, s_im = _trunk(
        x_sample.reshape(bs * ls, D_MODEL), bs, ls,
        (state_gla, state_gdn, state_gdn_conv, state_s5_re, state_s5_im), wts)
    return (yp.reshape(bp, lp, D_MODEL), ys.reshape(bs, ls, D_MODEL),
            p_gla, p_gdn, p_conv, p_re, p_im, s_gla, s_gdn, s_conv, s_re, s_im)
```

```python
import functools
import math

import jax
import jax.numpy as jnp
from jax import lax
from jax.experimental import pallas as pl
from jax.experimental.pallas import tpu as pltpu

F32 = jnp.float32
BF16 = jnp.bfloat16

D_MODEL = 1024
DEPTH = 4
ALPHA_RES = (2 * DEPTH) ** 0.25
LN_EPS = 1e-5
NORM_EPS = 1e-6
CHUNK = 64

GLA_HEADS = 4
GLA_KW = 512
GLA_VW = 1024
GLA_DK = 128
GLA_DV = 256
GLA_LOWRANK = 16
GLA_TAU = 16.0
GLA_MAIN = 2 * GLA_KW + 2 * GLA_VW

GDN_HEADS = 8
GDN_DK = 128
GDN_DV = 128
GDN_KW = 1024
GDN_VW = 1024
GDN_CONV = 4
GDN_CONV_CH = 3072
GDN_MAIN = GDN_CONV_CH + GDN_VW

S5_WIDTH = 1024
S5_GROUP = 16
S5_GROUPS = 64
S5_STATE = 64
S5_KT = 4
S5_KT_W = S5_WIDTH // S5_KT
S5_KT_STATES = (S5_GROUPS // S5_KT) * S5_STATE
S5_NSTATE = S5_GROUPS * S5_STATE

LANES = 128
SUBLANES = 8
VMEM_LIMIT = 56 * 1024 * 1024
ROW_BLOCK = 512
LONG_TIME_BLOCK = 512
S5_LONG_STEPS = 64
S5_SHORT_SEQS = 64


def _bf(x):
    return x.astype(BF16)


def _dot(a, b):
    return jnp.dot(a, b, preferred_element_type=F32)


def _dot_nt(a, b):
    return lax.dot_general(a, b, (((1,), (1,)), ((), ())), preferred_element_type=F32)


def _dot_tn(a, b):
    return lax.dot_general(a, b, (((0,), (0,)), ((), ())), preferred_element_type=F32)


def _split2_bf(x):
    h = x.astype(BF16)
    return h, (x - h.astype(F32)).astype(BF16)


def _split3_bf(x):
    h = x.astype(BF16)
    r = x - h.astype(F32)
    m = r.astype(BF16)
    return h, m, (r - m.astype(F32)).astype(BF16)


def _sigmoid(x):
    return 1.0 / (1.0 + jnp.exp(-x))


def _silu(x):
    return x * _sigmoid(x)


def _softplus(x):
    return jnp.maximum(x, 0.0) + jnp.log(1.0 + jnp.exp(-jnp.abs(x)))


def _gelu_tanh(x):
    return 0.5 * x * (1.0 + jnp.tanh(math.sqrt(2.0 / math.pi) * (x + 0.044715 * (x * x * x))))


def _layer_norm(x, g, b):
    mu = jnp.mean(x, axis=-1, keepdims=True)
    xc = x - mu
    var = jnp.mean(xc * xc, axis=-1, keepdims=True)
    return xc * lax.rsqrt(var + LN_EPS) * g + b


def _cumsum_rows(x, seg):
    row = lax.broadcasted_iota(jnp.int32, x.shape, 0) & (seg - 1)
    s = 1
    while s < seg:
        x = x + jnp.where(row >= s, pltpu.roll(x, s, axis=0), 0.0)
        s *= 2
    return x


def _seg_last_rows(x, seg):
    c, w = x.shape
    if seg == c:
        return jnp.broadcast_to(x[c - 1:c, :], (c, w))
    x3 = x.reshape(c // seg, seg, w)
    return jnp.broadcast_to(x3[:, seg - 1:seg, :], x3.shape).reshape(c, w)


def _seg_masks(shape, seg):
    ri = lax.broadcasted_iota(jnp.int32, shape, 0)
    li = lax.broadcasted_iota(jnp.int32, shape, 1) & (CHUNK - 1)
    shift = int(math.log2(seg))
    same = (ri >> shift) == (li >> shift)
    return same & (ri >= li), same & (ri > li), ri == li


def _aligned(v, m):
    return v if isinstance(v, int) else pl.multiple_of(v, m)


def _const_spec(arr):
    nd = arr.ndim
    return pl.BlockSpec(arr.shape, lambda *_: (0,) * nd, pipeline_mode=pl.Buffered(1))


def _params(sem):
    return pltpu.CompilerParams(dimension_semantics=sem, vmem_limit_bytes=VMEM_LIMIT)


def _rowwise_call(body, name, rows, consts, out_widths):
    n = rows[0].shape[0]
    tm = min(ROW_BLOCK, n)
    spec = lambda w: pl.BlockSpec((tm, w), lambda i: (i, 0))
    return pl.pallas_call(
        body, grid=(n // tm,),
        in_specs=[spec(r.shape[1]) for r in rows] + [_const_spec(c) for c in consts],
        out_specs=[spec(w) for w in out_widths],
        out_shape=[jax.ShapeDtypeStruct((n, w), F32) for w in out_widths],
        compiler_params=_params(("parallel",)), name=name,
    )(*rows, *consts)


def _gla_proj_body(x_ref, w_ref, wlr_ref, wa2_ref, ba_ref, h_ref, la_ref):
    xb = _bf(x_ref[...])
    h = _dot(xb, w_ref[:, 0:GLA_MAIN])
    h_ref[:, :GLA_KW] = h[:, :GLA_KW] * GLA_DK ** -0.5
    h_ref[:, GLA_KW:] = h[:, GLA_KW:]
    lr = _dot(xb, wlr_ref[...])
    pre = _dot(_bf(lr), wa2_ref[...]) + ba_ref[...]
    la_ref[...] = -_softplus(-pre) * (1.0 / GLA_TAU)


def _gdn_proj_body(x_ref, w_ref, wab_ref, h_ref, ab_ref):
    xb = _bf(x_ref[...])
    h_ref[...] = _dot(xb, w_ref[:, 0:GDN_MAIN])
    ab_ref[...] = _dot(xb, wab_ref[...])


def _out_ln_body(og_ref, x_ref, w_ref, g_ref, b_ref, y_ref):
    out = _dot(_bf(og_ref[...]), w_ref[...])
    y_ref[...] = _layer_norm(ALPHA_RES * x_ref[...] + out, g_ref[...], b_ref[...])


class _Geometry:
    def __init__(self, n_b, n_l):
        self.n_b, self.n_l = n_b, n_l
        self.long = n_l >= CHUNK
        if self.long:
            assert n_l % LONG_TIME_BLOCK == 0
            self.seg = CHUNK
            self.tc = LONG_TIME_BLOCK
            self.n_t = n_l // self.tc
            self.grid = (n_b, self.n_t)
            self.sem = ("parallel", "arbitrary")
            self.seqs = 1
            self.row_map = lambda b, t: (b * self.n_t + t, 0)
            self.seq_block = lambda b, t: b
        else:
            assert CHUNK % n_l == 0 and n_l % SUBLANES == 0 and (n_b * n_l) % CHUNK == 0
            self.seg = n_l
            self.tc = CHUNK
            self.grid = (n_b * n_l // self.tc,)
            self.sem = ("parallel",)
            self.seqs = self.tc // n_l
            self.row_map = lambda i: (i, 0)
            self.seq_block = lambda i: i

    def state_spec(self, layer, tail, n_layers=1):
        zeros = (0,) * len(tail)
        first = layer if n_layers == 1 else 0
        return pl.BlockSpec((n_layers, self.seqs) + tail, lambda *g: (first, self.seq_block(*g)) + zeros)

    def last_time_block(self):
        return pl.program_id(1) == pl.num_programs(1) - 1 if self.long else None


def _state_access(geo, s0_ref, sout_ref, s_scr, out_layer=0):
    if geo.long:
        return (lambda sq, hh: s_scr[hh]), (lambda sq, hh, v: s_scr.__setitem__(hh, v))
    return ((lambda sq, hh: s0_ref[0, sq, hh]),
            (lambda sq, hh, v: sout_ref.__setitem__((out_layer, sq, hh), v)))


def _gla_recur_body(*refs, geo, has_alias, layer, out_layers):
    refs = list(refs)
    h_ref, la_ref, ng_ref = refs[:3]
    pos = 3
    s0_ref = x_ref = wout_ref = lng_ref = lnb_ref = None
    if geo.long:
        x_ref, wout_ref, lng_ref, lnb_ref = refs[pos:pos + 4]
        pos += 4
    else:
        s0_ref = refs[pos]
        pos += 1
    if has_alias:
        pos += 1
    og_ref, sout_ref = refs[pos:pos + 2]
    og_scr, s_scr = refs[pos + 2:pos + 4] if geo.long else (og_ref, None)
    c, seg, tc = CHUNK, geo.seg, geo.tc
    n_seg = c // seg
    out_layer = layer if out_layers > 1 else 0
    get_state, put_state = _state_access(geo, s0_ref, sout_ref, s_scr, out_layer)

    def zero_other_layers():
        for other in range(out_layers):
            if other != out_layer:
                sout_ref[other] = jnp.zeros(sout_ref.shape[1:], F32)

    if geo.long:
        @pl.when(pl.program_id(1) == 0)
        def _():
            s_scr[...] = jnp.zeros_like(s_scr)
    else:
        zero_other_layers()

    incl, _, _ = _seg_masks((c, c), seg)
    ng = ng_ref[...]

    heads = range(GLA_HEADS)
    kq = [slice(hh * GLA_DK, (hh + 1) * GLA_DK) for hh in heads]
    cpi = 2 if (tc // c) % 2 == 0 else 1

    def local(n):
        rows = pl.ds(_aligned(n * c, c), c)
        b = _cumsum_rows(la_ref[rows, :], seg)
        bl = _seg_last_rows(b, seg)
        k_all = h_ref[rows, GLA_KW:2 * GLA_KW]
        q_dec = h_ref[rows, 0:GLA_KW] * jnp.exp(b)
        k_inv = k_all * jnp.exp(-b)
        k_end = k_all * jnp.exp(bl - b)
        e_bl = jnp.exp(bl)
        vs = [h_ref[rows, 2 * GLA_KW + hh * GLA_DV:2 * GLA_KW + (hh + 1) * GLA_DV] for hh in heads]
        atts = [jnp.where(incl, _dot_nt(_bf(q_dec[:, kq[hh]]), _bf(k_inv[:, kq[hh]])), 0.0) for hh in heads]
        os_ = [_dot(_bf(atts[hh]), _bf(vs[hh])) for hh in heads]
        return rows, q_dec, k_end, e_bl, vs, os_

    def state_pass(n, rows, q_dec, k_end, e_bl, vs, os_):
        inter = [[] for _ in heads]
        for s in range(n_seg):
            sq = n * n_seg + s
            sr = slice(s * seg, (s + 1) * seg)
            sts = [get_state(sq, hh) for hh in heads]
            for hh in heads:
                inter[hh].append(_dot(_bf(q_dec[sr, kq[hh]]), _bf(sts[hh])))
            upd = [_dot_tn(_bf(k_end[sr, kq[hh]]), _bf(vs[hh][sr])) for hh in heads]
            for hh in heads:
                ecol = jnp.transpose(jnp.broadcast_to(e_bl[s * seg:s * seg + 1, kq[hh]], (GLA_DK, GLA_DK)))
                ecol = jnp.concatenate([ecol] * (GLA_DV // GLA_DK), axis=1)
                put_state(sq, hh, sts[hh] * ecol + upd[hh])
        for hh in heads:
            o = os_[hh] + (inter[hh][0] if n_seg == 1 else jnp.concatenate(inter[hh], axis=0))
            o = o * lax.rsqrt(jnp.mean(o * o, axis=-1, keepdims=True) + NORM_EPS) * ng
            kr = slice(2 * GLA_KW + GLA_VW + hh * GLA_DV, 2 * GLA_KW + GLA_VW + (hh + 1) * GLA_DV)
            og_scr[rows, hh * GLA_DV:(hh + 1) * GLA_DV] = o * _silu(h_ref[rows, kr])

    def trip(n, carry):
        parts = [local(n * cpi + i) for i in range(cpi)]
        for i in range(cpi):
            state_pass(n * cpi + i, *parts[i])
        return carry

    lax.fori_loop(0, tc // (c * cpi), trip, 0)

    if geo.long:
        _out_ln_body(og_scr, x_ref, wout_ref, lng_ref, lnb_ref, og_ref)

        @pl.when(geo.last_time_block())
        def _():
            sout_ref[out_layer, 0] = s_scr[...]
            zero_other_layers()


def _gla_recur(h, la, norm_g, state_in, layer, n_layers, prev_out, geo, out_ln):
    tail = (GLA_HEADS, GLA_DK, GLA_DV)
    ins = [h, la, norm_g]
    in_specs = [pl.BlockSpec((geo.tc, GLA_MAIN), geo.row_map), pl.BlockSpec((geo.tc, GLA_KW), geo.row_map),
                _const_spec(norm_g)]
    if geo.long:
        ins += list(out_ln)
        in_specs += [pl.BlockSpec((geo.tc, D_MODEL), geo.row_map)] + [_const_spec(a) for a in out_ln[1:]]
    else:
        ins.append(state_in)
        in_specs.append(geo.state_spec(layer, tail))
    aliases = {}
    if prev_out is not None:
        aliases = {len(ins): 1}
        ins.append(prev_out)
        in_specs.append(pl.BlockSpec(memory_space=pl.ANY))
    scratch = [pltpu.VMEM((geo.tc, GLA_VW), F32), pltpu.VMEM(tail, F32)] if geo.long else []
    out_layers = n_layers if prev_out is None else 1
    og, s_out = pl.pallas_call(
        functools.partial(_gla_recur_body, geo=geo, has_alias=prev_out is not None,
                          layer=layer, out_layers=out_layers),
        grid=geo.grid, in_specs=in_specs,
        out_specs=[pl.BlockSpec((geo.tc, GLA_VW), geo.row_map), geo.state_spec(layer, tail, out_layers)],
        out_shape=[jax.ShapeDtypeStruct((geo.n_b * geo.n_l, GLA_VW), F32),
                   jax.ShapeDtypeStruct((n_layers, geo.n_b) + tail, F32)],
        scratch_shapes=scratch, input_output_aliases=aliases,
        compiler_params=_params(geo.sem), name="gla_recur",
    )(*ins)
    return og, s_out


def _block_rows(x, mask):
    return jnp.where(mask, jnp.concatenate([x, x], axis=0), 0.0)


def _mm3_split(lhs, rhs_hi, rhs_lo):
    lh, ll = _split2_bf(lhs)
    return _dot(lh, rhs_hi) + (_dot(lh, rhs_lo) + _dot(ll, rhs_hi))


def _mm3_pair(lhs, x_pair, mask):
    xh, xl = _split2_bf(x_pair)
    return _mm3_split(lhs, _bf(_block_rows(xh.astype(F32), mask)), _bf(_block_rows(xl.astype(F32), mask)))


def _gdn_expand_matrix():
    h = jnp.arange(LANES)[:, None]
    e64 = (jnp.arange(GDN_HEADS * CHUNK)[None, :] // CHUNK == h)
    e128 = (jnp.arange(GDN_HEADS * GDN_DK)[None, :] // GDN_DK == h)
    return jnp.concatenate([e64, e128], axis=1).astype(BF16)


def _gdn_recur_body(*refs, geo):
    refs = list(refs)
    h_ref, ab_ref, wc_ref, alog_ref, dtb_ref, ng_ref, eall_ref = refs[:7]
    pos = 7
    conv0_ref = s0_ref = x_ref = wout_ref = lng_ref = lnb_ref = None
    if geo.long:
        x_ref, wout_ref, lng_ref, lnb_ref = refs[pos:pos + 4]
        pos += 4
    else:
        conv0_ref, s0_ref = refs[pos:pos + 2]
        pos += 2
    og_ref, convn_ref, sout_ref = refs[pos:pos + 3]
    pos += 3
    qkv_scr, g_scr, beta_scr, u_scr, w_scr, att_scr, egc_scr, kend_scr, qk_scr = refs[pos:pos + 9]
    pos += 9
    carry_scr = s_scr = None
    og_scr = og_ref
    if geo.long:
        carry_scr, og_scr, s_scr = refs[pos:pos + 3]
    c, seg, tc = CHUNK, geo.seg, geo.tc
    n_seg = c // seg
    get_state, put_state = _state_access(geo, s0_ref, sout_ref, s_scr)
    pad = SUBLANES
    lo = pad - (GDN_CONV - 1)
    n_pair = GDN_HEADS // 2
    pw = 2 * c
    hw = GDN_HEADS * c
    cpi = 2 if (tc // c) % 2 == 0 else 1

    if geo.long:
        @pl.when(pl.program_id(1) == 0)
        def _():
            carry_scr[...] = jnp.zeros_like(carry_scr)
            s_scr[...] = jnp.zeros_like(s_scr)

    def window_conv(e, cols, t_in=None):
        conv = e[pad:pad + c] * wc_ref[GDN_CONV - 1:GDN_CONV, cols]
        for j in range(GDN_CONV - 1):
            tap = e[lo + j:lo + j + c]
            if t_in is not None:
                tap = jnp.where(t_in >= GDN_CONV - 1 - j, tap, 0.0)
            conv = conv + tap * wc_ref[j:j + 1, cols]
        return conv

    def conv_chunk(m):
        static = isinstance(m, int)
        r0 = m * c if static else pl.multiple_of(m * c, c)
        for cb in range(GDN_CONV_CH // LANES):
            cols = slice(cb * LANES, (cb + 1) * LANES)
            if static and m == 0:
                e = jnp.concatenate([carry_scr[:, cols], h_ref[0:c, cols]], axis=0)
            elif static:
                e = h_ref[r0 - pad:r0 + c, cols]
            else:
                e = h_ref[pl.ds(pl.multiple_of(m * c - pad, SUBLANES), c + pad), cols]
            qkv_scr[pl.ds(r0, c), cols] = _silu(window_conv(e, cols))

    if geo.long:
        for i in range(cpi):
            conv_chunk(i)

        @pl.when(geo.last_time_block())
        def _():
            convn_ref[0, 0] = h_ref[tc - (GDN_CONV - 1):tc, 0:GDN_CONV_CH]
    else:
        t_in = lax.broadcasted_iota(jnp.int32, (c, LANES), 0) & (seg - 1)
        for cb in range(GDN_CONV_CH // LANES):
            cols = slice(cb * LANES, (cb + 1) * LANES)
            e = jnp.concatenate([jnp.zeros((pad, LANES), F32), h_ref[:, cols]], axis=0)
            qkv_scr[:, cols] = window_conv(e, cols, t_in)
        for sq in range(geo.seqs):
            c0 = conv0_ref[0, sq]
            head = []
            for t in range(GDN_CONV - 1):
                acc = c0[t:t + 1, :] * wc_ref[0:1, :]
                for j in range(1, GDN_CONV - 1 - t):
                    acc = acc + c0[t + j:t + j + 1, :] * wc_ref[j:j + 1, :]
                head.append(acc)
            head.append(jnp.zeros((SUBLANES - (GDN_CONV - 1), GDN_CONV_CH), F32))
            first = slice(sq * seg, sq * seg + SUBLANES)
            qkv_scr[first, :] = qkv_scr[first, :] + jnp.concatenate(head, axis=0)
            convn_ref[0, sq] = h_ref[(sq + 1) * seg - (GDN_CONV - 1):(sq + 1) * seg, 0:GDN_CONV_CH]
        qkv_scr[...] = _silu(qkv_scr[...])

    g_scr[...] = -jnp.exp(alog_ref[...]) * _softplus(ab_ref[:, 0:LANES] + dtb_ref[...])
    beta_scr[...] = _sigmoid(ab_ref[:, LANES:2 * LANES])

    incl, strict, delta = _seg_masks((c, hw), seg)
    eye_pair = jnp.where(delta[:, 0:pw], 1.0, 0.0)
    r2 = lax.broadcasted_iota(jnp.int32, (2 * c, pw), 0) >> 6
    mask_cc = r2 == (lax.broadcasted_iota(jnp.int32, (2 * c, pw), 1) >> 6)
    r3 = lax.broadcasted_iota(jnp.int32, (2 * c, 2 * GDN_DK), 0) >> 6
    mask_cd = r3 == (lax.broadcasted_iota(jnp.int32, (2 * c, 2 * GDN_DK), 1) >> 7)
    ones_c = jnp.ones((c, c), BF16)
    ng = ng_ref[...]

    def gate_forms(rows):
        gc = _cumsum_rows(g_scr[rows, :], seg)
        e_all = eall_ref[...]
        sh, sm, sl = _split3_bf(jnp.concatenate([gc, beta_scr[rows, :]], axis=0))
        ex = _dot(sh, e_all) + (_dot(sm, e_all) + _dot(sl, e_all))
        gcol, bcol, gwide = ex[0:c, 0:hw], ex[c:2 * c, 0:hw], ex[0:c, hw:]
        th, tm, tl = _split3_bf(jnp.concatenate(
            [jnp.where(delta, gcol, 0.0), jnp.where(delta, bcol, 0.0)], axis=1))
        rowf = _dot(ones_c, th) + (_dot(ones_c, tm) + _dot(ones_c, tl))
        grow, brow = rowf[:, 0:hw], rowf[:, hw:]
        dec = jnp.where(incl, jnp.exp(jnp.where(incl, gcol - grow, 0.0)), 0.0)
        egc_scr[rows, :] = jnp.exp(gwide)
        eend = jnp.exp(_seg_last_rows(gwide, seg) - gwide)
        return dec, bcol, brow, jnp.exp(grow), eend

    def local(n, side=None):
        tick = (lambda: None) if side is None else (lambda: next(side, None))
        chunk_rows = [pl.ds(_aligned((n * cpi + i) * c, c), c) for i in range(cpi)]
        forms = [gate_forms(rows) for rows in chunk_rows]
        units = [(i, p) for i in range(cpi) for p in range(n_pair)]
        tick()

        a_pairs = []
        for i, p in units:
            rows = chunk_rows[i]
            dec, bcol, _, _, eend = forms[i]
            qs, ks = [], []
            for hh in (2 * p, 2 * p + 1):
                kq = slice(hh * GDN_DK, (hh + 1) * GDN_DK)
                kk = slice(GDN_KW + hh * GDN_DK, GDN_KW + (hh + 1) * GDN_DK)
                q = qkv_scr[rows, kq]
                k = qkv_scr[rows, kk]
                q = q * lax.rsqrt(jnp.sum(q * q, axis=-1, keepdims=True) + NORM_EPS) * GDN_DK ** -0.5
                k = k * lax.rsqrt(jnp.sum(k * k, axis=-1, keepdims=True) + NORM_EPS)
                qk_scr[rows, kq] = q
                qk_scr[rows, kk] = k
                kend_scr[rows, kq] = k * eend[:, kq]
                qs.append(q)
                ks.append(k)
            kh = _bf(jnp.concatenate(ks, axis=1))
            kbd_h = _bf(_block_rows(kh.astype(F32), mask_cd))
            kkqk = _dot_nt(jnp.concatenate([kh, _bf(jnp.concatenate(qs, axis=1))], axis=0), kbd_h)
            cs = slice(p * pw, (p + 1) * pw)
            a_pairs.append(jnp.where(strict[:, cs], kkqk[0:c] * dec[:, cs] * bcol[:, cs], 0.0))
            att_scr[rows, cs] = kkqk[c:2 * c] * dec[:, cs]
        tick()
        ps = [eye_pair - a for a in a_pairs]
        xs = [_mm3_pair(a, a, mask_cc) for a in a_pairs]
        tick()
        steps = int(math.log2(seg)) - 1
        for s in range(steps):
            if s < steps - 1:
                rs = [_mm3_pair(jnp.concatenate([pp, x], axis=0), x, mask_cc) for pp, x in zip(ps, xs)]
                ps = [pp + r[0:c] for pp, r in zip(ps, rs)]
                xs = [r[c:2 * c] for r in rs]
            else:
                ps = [pp + _mm3_pair(pp, x, mask_cc) for pp, x in zip(ps, xs)]
            tick()
        for t_inv, (i, p) in zip(ps, units):
            rows = chunk_rows[i]
            _, _, brow, egrow, _ = forms[i]
            cs = slice(p * pw, (p + 1) * pw)
            vs = slice(p * 2 * GDN_DV, (p + 1) * 2 * GDN_DV)
            t_b = t_inv * brow[:, cs]
            vh, vl = _split2_bf(qkv_scr[rows, 2 * GDN_KW + p * 2 * GDN_DV:2 * GDN_KW + (p + 1) * 2 * GDN_DV])
            u_scr[rows, vs] = _mm3_split(t_b, _bf(_block_rows(vh.astype(F32), mask_cd)),
                                         _bf(_block_rows(vl.astype(F32), mask_cd)))
            kh, kl = _split2_bf(qk_scr[rows, GDN_KW + p * 2 * GDN_DK:GDN_KW + (p + 1) * 2 * GDN_DK])
            w_scr[rows, vs] = _mm3_split(t_b * egrow[:, cs], _bf(_block_rows(kh.astype(F32), mask_cd)),
                                         _bf(_block_rows(kl.astype(F32), mask_cd)))
        if geo.long:
            for i in range(cpi):
                nxt, cap = (n + 1) * cpi + i, tc // c - cpi + i
                conv_chunk(min(nxt, cap) if isinstance(n, int) else jnp.minimum(nxt, cap))
        if side is not None:
            for _ in side:
                pass

    def recur(n):
        r0 = _aligned(n * c, c)
        rows = pl.ds(r0, c)
        heads = range(GDN_HEADS)
        kqs = [slice(hh * GDN_DK, (hh + 1) * GDN_DK) for hh in heads]
        vn_seg = [[] for _ in heads]
        qs_seg = [[] for _ in heads]
        for s in range(n_seg):
            sq = n * n_seg + s
            sr = pl.ds(_aligned(r0 + s * seg, SUBLANES), seg)
            tail = pl.ds(_aligned(r0 + (s + 1) * seg - SUBLANES, SUBLANES), SUBLANES)
            sts = [get_state(sq, hh) for hh in heads]
            m1s = [_dot(_bf(jnp.concatenate([w_scr[sr, kqs[hh]], qk_scr[sr, kqs[hh]]], axis=0)), _bf(sts[hh]))
                   for hh in heads]
            yield
            vn = [u_scr[sr, kqs[hh]] - m1s[hh][0:seg] for hh in heads]
            upd = [_dot_tn(_bf(kend_scr[sr, kqs[hh]]), _bf(vn[hh])) for hh in heads]
            yield
            for hh in heads:
                e_last = egc_scr[tail, kqs[hh]][SUBLANES - 1:SUBLANES, :]
                put_state(sq, hh, sts[hh] * e_last + upd[hh])
                vn_seg[hh].append(vn[hh])
                qs_seg[hh].append(m1s[hh][seg:2 * seg])
        join = lambda parts: parts[0] if n_seg == 1 else jnp.concatenate(parts, axis=0)
        o_pairs = []
        for p in range(n_pair):
            pair = (2 * p, 2 * p + 1)
            vs = slice(p * 2 * GDN_DV, (p + 1) * 2 * GDN_DV)
            vbd = _bf(_block_rows(jnp.concatenate([join(vn_seg[hh]) for hh in pair], axis=1), mask_cd))
            o_pairs.append(_dot(_bf(att_scr[rows, p * pw:(p + 1) * pw]), vbd)
                           + egc_scr[rows, vs] * jnp.concatenate([join(qs_seg[hh]) for hh in pair], axis=1))
        yield
        for p in range(n_pair):
            for i, hh in enumerate((2 * p, 2 * p + 1)):
                o = o_pairs[p][:, i * GDN_DV:(i + 1) * GDN_DV]
                o = o * lax.rsqrt(jnp.mean(o * o, axis=-1, keepdims=True) + NORM_EPS) * ng
                kz = slice(GDN_CONV_CH + hh * GDN_DV, GDN_CONV_CH + (hh + 1) * GDN_DV)
                og_scr[rows, hh * GDN_DV:(hh + 1) * GDN_DV] = o * _silu(h_ref[rows, kz])
        yield

    def recur_trip(n):
        for i in range(cpi):
            yield from recur(n * cpi + i)

    def drain(gen):
        for _ in gen:
            pass

    n_trip = tc // (c * cpi)
    local(0)

    def trip(n, carry):
        local(n, recur_trip(n - 1))
        return carry

    lax.fori_loop(1, n_trip, trip, 0)
    drain(recur_trip(n_trip - 1))

    if geo.long:
        carry_scr[...] = h_ref[tc - pad:tc, 0:GDN_CONV_CH]
        _out_ln_body(og_scr, x_ref, wout_ref, lng_ref, lnb_ref, og_ref)

        @pl.when(geo.last_time_block())
        def _():
            sout_ref[0, 0] = s_scr[...]


def _gdn_recur(h, ab, w_conv, a_log, dt_bias, norm_g, conv_in, state_in, layer, n_layers, geo, out_ln):
    tail = (GDN_HEADS, GDN_DK, GDN_DV)
    cv_tail = (GDN_CONV - 1, GDN_CONV_CH)
    consts = [w_conv, a_log, dt_bias, norm_g, _gdn_expand_matrix()]
    ins = [h, ab] + consts
    in_specs = [pl.BlockSpec((geo.tc, GDN_MAIN), geo.row_map), pl.BlockSpec((geo.tc, 2 * LANES), geo.row_map)]
    in_specs += [_const_spec(a) for a in consts]
    tc = geo.tc
    if geo.long:
        ins += list(out_ln)
        in_specs += [pl.BlockSpec((tc, D_MODEL), geo.row_map)] + [_const_spec(a) for a in out_ln[1:]]
    else:
        assert tc == CHUNK
        ins += [conv_in, state_in]
        in_specs += [geo.state_spec(layer, cv_tail), geo.state_spec(layer, tail)]
    scratch = [pltpu.VMEM((tc, GDN_CONV_CH), F32),
               pltpu.VMEM((tc, LANES), F32), pltpu.VMEM((tc, LANES), F32),
               pltpu.VMEM((tc, GDN_VW), F32), pltpu.VMEM((tc, GDN_KW), F32),
               pltpu.VMEM((tc, GDN_HEADS * CHUNK), F32), pltpu.VMEM((tc, GDN_KW), F32),
               pltpu.VMEM((tc, GDN_KW), F32), pltpu.VMEM((tc, 2 * GDN_KW), F32)]
    if geo.long:
        scratch += [pltpu.VMEM((SUBLANES, GDN_CONV_CH), F32), pltpu.VMEM((tc, GDN_VW), F32),
                    pltpu.VMEM(tail, F32)]
    return pl.pallas_call(
        functools.partial(_gdn_recur_body, geo=geo),
        grid=geo.grid, in_specs=in_specs,
        out_specs=[pl.BlockSpec((tc, GDN_VW), geo.row_map), geo.state_spec(layer, cv_tail),
                   geo.state_spec(layer, tail)],
        out_shape=[jax.ShapeDtypeStruct((geo.n_b * geo.n_l, GDN_VW), F32),
                   jax.ShapeDtypeStruct((n_layers, geo.n_b) + cv_tail, F32),
                   jax.ShapeDtypeStruct((n_layers, geo.n_b) + tail, F32)],
        scratch_shapes=scratch,
        compiler_params=_params(geo.sem), name="gdn_recur",
    )(*ins)


def _s5_body(*refs, tt, nb, has_init):
    if has_init:
        (x_ref, win_ref, wb_ref, are_ref, aim_ref, wc_ref, d_ref, wglu_ref, bglu_ref, wout_ref,
         lng_ref, lnb_ref, h0re_ref, h0im_ref, y_ref, hre_ref, him_ref,
         x_scr, u_scr, z_scr, bu_scr, y_scr, st_scr) = refs
    else:
        (x_ref, win_ref, wb_ref, are_ref, aim_ref, wc_ref, d_ref, wglu_ref, bglu_ref, wout_ref,
         lng_ref, lnb_ref, y_ref, hre_ref, him_ref,
         x_scr, u_scr, z_scr, bu_scr, y_scr, st_scr) = refs
    tb = pl.program_id(1)
    ns = S5_KT_STATES

    @pl.when(tb == 0)
    def _():
        if not has_init:
            st_scr[...] = jnp.zeros_like(st_scr)
        else:
            for kt in range(S5_KT):
                st_scr[:, 2 * kt * ns:(2 * kt + 1) * ns] = h0re_ref[0, :, kt * ns:(kt + 1) * ns]
                st_scr[:, (2 * kt + 1) * ns:(2 * kt + 2) * ns] = h0im_ref[0, :, kt * ns:(kt + 1) * ns]

    x_scr[...] = jnp.swapaxes(x_ref[...], 0, 1).reshape(tt * nb, D_MODEL)
    h = _dot(_bf(x_scr[...]), win_ref[...])
    u_scr[...] = h[:, 0:S5_WIDTH]
    z_scr[...] = h[:, S5_WIDTH:]

    def input_map(kt):
        cols = slice(kt * S5_KT_W, (kt + 1) * S5_KT_W)
        bu_scr[kt % 2] = _dot(_bf(u_scr[:, cols]), wb_ref[kt])

    def scan(kt):
        bu = bu_scr.at[kt % 2]
        a_re = jnp.broadcast_to(are_ref[kt], (SUBLANES, ns))
        a_im = jnp.broadcast_to(aim_ref[kt], (SUBLANES, ns))
        base = kt * 2 * ns
        for rb in range(nb // SUBLANES):
            st_rows = slice(rb * SUBLANES, (rb + 1) * SUBLANES)
            h_re = st_scr[st_rows, base:base + ns]
            h_im = st_scr[st_rows, base + ns:base + 2 * ns]
            for t in range(tt):
                rows = slice(t * nb + rb * SUBLANES, t * nb + (rb + 1) * SUBLANES)
                h_re, h_im = (a_re * h_re - a_im * h_im + bu[rows, 0:ns],
                              a_re * h_im + a_im * h_re + bu[rows, ns:2 * ns])
                bu[rows, 0:ns] = h_re
                bu[rows, ns:2 * ns] = h_im
            st_scr[st_rows, base:base + ns] = h_re
            st_scr[st_rows, base + ns:base + 2 * ns] = h_im

    def output_map(kt):
        cols = slice(kt * S5_KT_W, (kt + 1) * S5_KT_W)
        y_scr[:, cols] = _dot(_bf(bu_scr[kt % 2]), wc_ref[kt]) + d_ref[:, cols] * u_scr[:, cols]

    input_map(0)
    for kt in range(S5_KT):
        if kt + 1 < S5_KT:
            input_map(kt + 1)
        scan(kt)
        output_map(kt)

    y = _gelu_tanh(y_scr[...])
    yg = _dot(_bf(y), wglu_ref[...]) + bglu_ref[...]
    y = yg[:, 0:S5_WIDTH] * _sigmoid(yg[:, S5_WIDTH:]) * _silu(z_scr[...])
    out = _dot(_bf(y), wout_ref[...])
    y = _layer_norm(ALPHA_RES * x_scr[...] + out, lng_ref[...], lnb_ref[...])
    y_ref[...] = jnp.swapaxes(y.reshape(tt, nb, D_MODEL), 0, 1)

    @pl.when(tb == pl.num_programs(1) - 1)
    def _():
        for kt in range(S5_KT):
            hre_ref[0, :, kt * ns:(kt + 1) * ns] = st_scr[:, 2 * kt * ns:(2 * kt + 1) * ns]
            him_ref[0, :, kt * ns:(kt + 1) * ns] = st_scr[:, (2 * kt + 1) * ns:(2 * kt + 2) * ns]


def _s5_discretize(lam_re, lam_im, log_dt, b_re, b_im, c_re, c_im):
    dt = jnp.exp(log_dt)[:, None]
    mag = jnp.exp(lam_re * dt)
    ab_re, ab_im = mag * jnp.cos(lam_im * dt), mag * jnp.sin(lam_im * dt)
    den = jnp.square(lam_re) + jnp.square(lam_im)
    num_re = ab_re - 1.0
    coef_re = (num_re * lam_re + ab_im * lam_im) / den
    coef_im = (ab_im * lam_re - num_re * lam_im) / den
    bb_re = coef_re[..., None] * b_re - coef_im[..., None] * b_im
    bb_im = coef_re[..., None] * b_im + coef_im[..., None] * b_re
    gl = S5_GROUPS // S5_KT
    eye = jnp.eye(gl, dtype=F32)

    def block_b(bb):
        t = bb.reshape(S5_KT, gl, S5_STATE, S5_GROUP)
        return jnp.einsum("kgpc,gh->kgchp", t, eye).reshape(S5_KT, S5_KT_W, S5_KT_STATES)

    def block_c(cc):
        t = cc.reshape(S5_KT, gl, S5_GROUP, S5_STATE)
        return jnp.einsum("kgcp,gh->kgphc", t, eye).reshape(S5_KT, S5_KT_STATES, S5_KT_W)

    w_b = jnp.concatenate([block_b(bb_re), block_b(bb_im)], axis=2).astype(BF16)
    w_c = jnp.concatenate([block_c(c_re), -block_c(c_im)], axis=1).astype(BF16)
    a_re = ab_re.reshape(S5_KT, 1, S5_KT_STATES)
    a_im = ab_im.reshape(S5_KT, 1, S5_KT_STATES)
    return w_b, w_c, a_re, a_im


def _s5_layer(x, prep, w_in, d_vec, w_glu, b_glu, w_out, ln_g, ln_b, h0, n_b, n_l):
    w_b, w_c, a_re, a_im = prep
    if n_l >= S5_LONG_STEPS:
        nb, tt = n_b, S5_LONG_STEPS
    else:
        nb, tt = min(S5_SHORT_SEQS, n_b), n_l
    assert nb % SUBLANES == 0 and tt % SUBLANES == 0
    rows = tt * nb
    has_init = h0 is not None
    consts = [w_in, w_b, a_re, a_im, w_c, d_vec, w_glu, b_glu, w_out, ln_g, ln_b]
    x_spec = pl.BlockSpec((nb, tt, D_MODEL), lambda b, t: (b, t, 0))
    st_spec = pl.BlockSpec((1, nb, S5_NSTATE), lambda b, t: (0, b, 0))
    ins = [x.reshape(n_b, n_l, D_MODEL)] + consts
    in_specs = [x_spec] + [_const_spec(a) for a in consts]
    if has_init:
        ins += list(h0)
        in_specs += [st_spec, st_spec]
    y, h_re, h_im = pl.pallas_call(
        functools.partial(_s5_body, tt=tt, nb=nb, has_init=has_init),
        grid=(n_b // nb, n_l // tt), in_specs=in_specs,
        out_specs=[x_spec, st_spec, st_spec],
        out_shape=[jax.ShapeDtypeStruct((n_b, n_l, D_MODEL), F32),
                   jax.ShapeDtypeStruct((1, n_b, S5_NSTATE), F32),
                   jax.ShapeDtypeStruct((1, n_b, S5_NSTATE), F32)],
        scratch_shapes=[pltpu.VMEM((rows, D_MODEL), F32),
                        pltpu.VMEM((rows, S5_WIDTH), F32), pltpu.VMEM((rows, S5_WIDTH), F32),
                        pltpu.VMEM((2, rows, 2 * S5_KT_STATES), F32), pltpu.VMEM((rows, S5_WIDTH), F32),
                        pltpu.VMEM((nb, 2 * S5_NSTATE), F32)],
        compiler_params=_params(("parallel", "arbitrary")), name="s5_layer",
    )(*ins)
    return y.reshape(n_b * n_l, D_MODEL), h_re, h_im


def _row2(v):
    return v.reshape(1, -1).astype(F32)


def _pad_cols(w, n):
    return jnp.concatenate([w, jnp.zeros((w.shape[0], n - w.shape[1]), w.dtype)], axis=1)


def _trunk(x, n_b, n_l, states, wts):
    geo = _Geometry(n_b, n_l)
    n_gla, n_gdn, n_s5 = (DEPTH + 2) // 3, (DEPTH + 1) // 3, DEPTH // 3
    assert n_gdn == 1 and n_s5 == 1
    s_gla = s_gdn = s_conv = s_re = s_im = None
    for i in range(DEPTH):
        j, kind = divmod(i, 3)
        ln_g, ln_b = _row2(wts["ln_g"][i]), _row2(wts["ln_b"][i])
        if kind == 0:
            w_in = wts["gla_w_in"][j]
            w_main = _bf(w_in)
            w_lr = _bf(_pad_cols(w_in[:, GLA_MAIN:], LANES))
            w_a2 = _bf(jnp.concatenate(
                [wts["gla_w_a2"][j], jnp.zeros((LANES - GLA_LOWRANK, GLA_KW), F32)], axis=0))
            h, la = _rowwise_call(_gla_proj_body, "gla_proj", [x],
                                  [w_main, w_lr, w_a2, _row2(wts["gla_b_a"][j])], [GLA_MAIN, GLA_KW])
            w_out = _bf(wts["gla_w_out"][j])
            og, s_gla = _gla_recur(h, la, _row2(wts["gla_norm_g"][j]),
                                   None if states is None else states[0], j, n_gla, s_gla, geo,
                                   (x, w_out, ln_g, ln_b))
        elif kind == 1:
            w_in = wts["gdn_w_in"][j]
            w_main = _bf(w_in)
            w_ab = _bf(jnp.concatenate(
                [_pad_cols(w_in[:, GDN_MAIN:GDN_MAIN + GDN_HEADS], LANES),
                 _pad_cols(w_in[:, GDN_MAIN + GDN_HEADS:], LANES)], axis=1))
            h, ab = _rowwise_call(_gdn_proj_body, "gdn_proj", [x], [w_main, w_ab], [GDN_MAIN, 2 * LANES])
            w_out = _bf(wts["gdn_w_out"][j])
            og, s_conv, s_gdn = _gdn_recur(
                h, ab, wts["gdn_w_conv"][j].astype(F32),
                _pad_cols(_row2(wts["gdn_a_log"][j]), LANES), _pad_cols(_row2(wts["gdn_dt_bias"][j]), LANES),
                _row2(wts["gdn_norm_g"][j]),
                None if states is None else states[2], None if states is None else states[1],
                j, n_gdn, geo, (x, w_out, ln_g, ln_b))
        else:
            prep = _s5_discretize(wts["s5_lam_re"][j].astype(F32), wts["s5_lam_im"][j].astype(F32),
                                  wts["s5_log_dt"][j].astype(F32), wts["s5_b_re"][j].astype(F32),
                                  wts["s5_b_im"][j].astype(F32), wts["s5_c_re"][j].astype(F32),
                                  wts["s5_c_im"][j].astype(F32))
            h0 = None if states is None else (states[3].reshape(n_s5, n_b, S5_NSTATE),
                                              states[4].reshape(n_s5, n_b, S5_NSTATE))
            x, h_re, h_im = _s5_layer(x, prep, _bf(wts["s5_w_in"][j]), _row2(wts["s5_d"][j]),
                                      _bf(wts["s5_w_glu"][j]), _row2(wts["s5_b_glu"][j]),
                                      _bf(wts["s5_w_out"][j]), ln_g, ln_b, h0, n_b, n_l)
            s_re = h_re.reshape(n_s5, n_b, S5_GROUPS, S5_STATE)
            s_im = h_im.reshape(n_s5, n_b, S5_GROUPS, S5_STATE)
            continue
        if geo.long:
            x = og
        else:
            (x,) = _rowwise_call(_out_ln_body, "out_ln", [og, x], [w_out, ln_g, ln_b], [D_MODEL])
    return x, s_gla, s_gdn, s_conv, s_re, s_im


def kernel(x_prompt, x_sample, state_gla, state_gdn, state_gdn_conv, state_s5_re, state_s5_im,
           ln_g, ln_b, gla_w_in, gla_w_a2, gla_b_a, gla_norm_g, gla_w_out,
           gdn_w_in, gdn_w_conv, gdn_a_log, gdn_dt_bias, gdn_norm_g, gdn_w_out,
           s5_w_in, s5_lam_re, s5_lam_im, s5_log_dt, s5_b_re, s5_b_im, s5_c_re, s5_c_im,
           s5_d, s5_w_glu, s5_b_glu, s5_w_out):
    wts = dict(ln_g=ln_g, ln_b=ln_b,
               gla_w_in=gla_w_in, gla_w_a2=gla_w_a2, gla_b_a=gla_b_a, gla_norm_g=gla_norm_g,
               gla_w_out=gla_w_out,
               gdn_w_in=gdn_w_in, gdn_w_conv=gdn_w_conv, gdn_a_log=gdn_a_log, gdn_dt_bias=gdn_dt_bias,
               gdn_norm_g=gdn_norm_g, gdn_w_out=gdn_w_out,
               s5_w_in=s5_w_in, s5_lam_re=s5_lam_re, s5_lam_im=s5_lam_im, s5_log_dt=s5_log_dt,
               s5_b_re=s5_b_re, s5_b_im=s5_b_im, s5_c_re=s5_c_re, s5_c_im=s5_c_im, s5_d=s5_d,
               s5_w_glu=s5_w_glu, s5_b_glu=s5_b_glu, s5_w_out=s5_w_out)

    bp, lp, _ = x_prompt.shape
    yp, p_gla, p_gdn, p_conv, p_re, p_im = _trunk(
        x_prompt.reshape(bp * lp, D_MODEL), bp, lp, None, wts)
    bs, ls, _ = x_sample.shape
    ys, s_gla, s_gdn, s_conv, s_re, s_im = _trunk(
        x_sample.reshape(bs * ls, D_MODEL), bs, ls,
        (state_gla, state_gdn, state_gdn_conv, state_s5_re, state_s5_im), wts)
    return (yp.reshape(bp, lp, D_MODEL), ys.reshape(bs, ls, D_MODEL),
            p_gla, p_gdn, p_conv, p_re, p_im, s_gla, s_gdn, s_conv, s_re, s_im)
```

```python
import functools
import math

import jax
import jax.numpy as jnp
from jax import lax
from jax.experimental import pallas as pl
from jax.experimental.pallas import tpu as pltpu

F32 = jnp.float32
BF16 = jnp.bfloat16

D_MODEL = 1024
DEPTH = 4
ALPHA_RES = (2 * DEPTH) ** 0.25
LN_EPS = 1e-5
NORM_EPS = 1e-6
CHUNK = 64

GLA_HEADS = 4
GLA_KW = 512
GLA_VW = 1024
GLA_DK = 128
GLA_DV = 256
GLA_LOWRANK = 16
GLA_TAU = 16.0
GLA_MAIN = 2 * GLA_KW + 2 * GLA_VW

GDN_HEADS = 8
GDN_DK = 128
GDN_DV = 128
GDN_KW = 1024
GDN_VW = 1024
GDN_CONV = 4
GDN_CONV_CH = 3072
GDN_MAIN = GDN_CONV_CH + GDN_VW

S5_WIDTH = 1024
S5_GROUP = 16
S5_GROUPS = 64
S5_STATE = 64
S5_KT = 4
S5_KT_W = S5_WIDTH // S5_KT
S5_KT_STATES = (S5_GROUPS // S5_KT) * S5_STATE
S5_NSTATE = S5_GROUPS * S5_STATE

LANES = 128
SUBLANES = 8
VMEM_LIMIT = 56 * 1024 * 1024
ROW_BLOCK = 512
LONG_TIME_BLOCK = 512
S5_LONG_STEPS = 64
S5_SHORT_SEQS = 64


def _bf(x):
    return x.astype(BF16)


def _dot(a, b):
    return jnp.dot(a, b, preferred_element_type=F32)


def _dot_nt(a, b):
    return lax.dot_general(a, b, (((1,), (1,)), ((), ())), preferred_element_type=F32)


def _dot_tn(a, b):
    return lax.dot_general(a, b, (((0,), (0,)), ((), ())), preferred_element_type=F32)


def _split2_bf(x):
    h = x.astype(BF16)
    return h, (x - h.astype(F32)).astype(BF16)


def _split3_bf(x):
    h = x.astype(BF16)
    r = x - h.astype(F32)
    m = r.astype(BF16)
    return h, m, (r - m.astype(F32)).astype(BF16)


def _sigmoid(x):
    return 1.0 / (1.0 + jnp.exp(-x))


def _silu(x):
    return x * _sigmoid(x)


def _softplus(x):
    return jnp.maximum(x, 0.0) + jnp.log(1.0 + jnp.exp(-jnp.abs(x)))


def _gelu_tanh(x):
    return 0.5 * x * (1.0 + jnp.tanh(math.sqrt(2.0 / math.pi) * (x + 0.044715 * (x * x * x))))


def _layer_norm(x, g, b):
    mu = jnp.mean(x, axis=-1, keepdims=True)
    xc = x - mu
    var = jnp.mean(xc * xc, axis=-1, keepdims=True)
    return xc * lax.rsqrt(var + LN_EPS) * g + b


def _cumsum_rows(x, seg):
    row = lax.broadcasted_iota(jnp.int32, x.shape, 0) & (seg - 1)
    s = 1
    while s < seg:
        x = x + jnp.where(row >= s, pltpu.roll(x, s, axis=0), 0.0)
        s *= 2
    return x


def _seg_last_rows(x, seg):
    c, w = x.shape
    if seg == c:
        return jnp.broadcast_to(x[c - 1:c, :], (c, w))
    x3 = x.reshape(c // seg, seg, w)
    return jnp.broadcast_to(x3[:, seg - 1:seg, :], x3.shape).reshape(c, w)


def _seg_masks(shape, seg):
    ri = lax.broadcasted_iota(jnp.int32, shape, 0)
    li = lax.broadcasted_iota(jnp.int32, shape, 1) & (CHUNK - 1)
    shift = int(math.log2(seg))
    same = (ri >> shift) == (li >> shift)
    return same & (ri >= li), same & (ri > li), ri == li


def _aligned(v, m):
    return v if isinstance(v, int) else pl.multiple_of(v, m)


def _const_spec(arr):
    nd = arr.ndim
    return pl.BlockSpec(arr.shape, lambda *_: (0,) * nd, pipeline_mode=pl.Buffered(1))


def _params(sem):
    return pltpu.CompilerParams(dimension_semantics=sem, vmem_limit_bytes=VMEM_LIMIT)


def _rowwise_call(body, name, rows, consts, out_widths):
    n = rows[0].shape[0]
    tm = min(ROW_BLOCK, n)
    spec = lambda w: pl.BlockSpec((tm, w), lambda i: (i, 0))
    return pl.pallas_call(
        body, grid=(n // tm,),
        in_specs=[spec(r.shape[1]) for r in rows] + [_const_spec(c) for c in consts],
        out_specs=[spec(w) for w in out_widths],
        out_shape=[jax.ShapeDtypeStruct((n, w), F32) for w in out_widths],
        compiler_params=_params(("parallel",)), name=name,
    )(*rows, *consts)


def _gla_proj_body(x_ref, w_ref, wlr_ref, wa2_ref, ba_ref, h_ref, la_ref):
    xb = _bf(x_ref[...])
    h = _dot(xb, w_ref[:, 0:GLA_MAIN])
    h_ref[:, :GLA_KW] = h[:, :GLA_KW] * GLA_DK ** -0.5
    h_ref[:, GLA_KW:] = h[:, GLA_KW:]
    lr = _dot(xb, wlr_ref[...])
    pre = _dot(_bf(lr), wa2_ref[...]) + ba_ref[...]
    la_ref[...] = -_softplus(-pre) * (1.0 / GLA_TAU)


def _gdn_proj_body(x_ref, w_ref, wab_ref, h_ref, ab_ref):
    xb = _bf(x_ref[...])
    h_ref[...] = _dot(xb, w_ref[:, 0:GDN_MAIN])
    ab_ref[...] = _dot(xb, wab_ref[...])


def _out_ln_body(og_ref, x_ref, w_ref, g_ref, b_ref, y_ref):
    out = _dot(_bf(og_ref[...]), w_ref[...])
    y_ref[...] = _layer_norm(ALPHA_RES * x_ref[...] + out, g_ref[...], b_ref[...])


class _Geometry:
    def __init__(self, n_b, n_l):
        self.n_b, self.n_l = n_b, n_l
        self.long = n_l >= CHUNK
        if self.long:
            assert n_l % LONG_TIME_BLOCK == 0
            self.seg = CHUNK
            self.tc = LONG_TIME_BLOCK
            self.n_t = n_l // self.tc
            self.grid = (n_b, self.n_t)
            self.sem = ("parallel", "arbitrary")
            self.seqs = 1
            self.row_map = lambda b, t: (b * self.n_t + t, 0)
            self.seq_block = lambda b, t: b
        else:
            assert CHUNK % n_l == 0 and n_l % SUBLANES == 0 and (n_b * n_l) % CHUNK == 0
            self.seg = n_l
            self.tc = CHUNK
            self.grid = (n_b * n_l // self.tc,)
            self.sem = ("parallel",)
            self.seqs = self.tc // n_l
            self.row_map = lambda i: (i, 0)
            self.seq_block = lambda i: i

    def state_spec(self, layer, tail, n_layers=1):
        zeros = (0,) * len(tail)
        first = layer if n_layers == 1 else 0
        return pl.BlockSpec((n_layers, self.seqs) + tail, lambda *g: (first, self.seq_block(*g)) + zeros)

    def last_time_block(self):
        return pl.program_id(1) == pl.num_programs(1) - 1 if self.long else None


def _state_access(geo, s0_ref, sout_ref, s_scr, out_layer=0):
    if geo.long:
        return (lambda sq, hh: s_scr[hh]), (lambda sq, hh, v: s_scr.__setitem__(hh, v))
    return ((lambda sq, hh: s0_ref[0, sq, hh]),
            (lambda sq, hh, v: sout_ref.__setitem__((out_layer, sq, hh), v)))


def _gla_recur_body(*refs, geo, has_alias, layer, out_layers):
    refs = list(refs)
    h_ref, la_ref, ng_ref = refs[:3]
    pos = 3
    s0_ref = x_ref = wout_ref = lng_ref = lnb_ref = None
    if geo.long:
        x_ref, wout_ref, lng_ref, lnb_ref = refs[pos:pos + 4]
        pos += 4
    else:
        s0_ref = refs[pos]
        pos += 1
    if has_alias:
        pos += 1
    og_ref, sout_ref = refs[pos:pos + 2]
    og_scr, s_scr = refs[pos + 2:pos + 4] if geo.long else (og_ref, None)
    c, seg, tc = CHUNK, geo.seg, geo.tc
    n_seg = c // seg
    out_layer = layer if out_layers > 1 else 0
    get_state, put_state = _state_access(geo, s0_ref, sout_ref, s_scr, out_layer)

    def zero_other_layers():
        for other in range(out_layers):
            if other != out_layer:
                sout_ref[other] = jnp.zeros(sout_ref.shape[1:], F32)

    if geo.long:
        @pl.when(pl.program_id(1) == 0)
        def _():
            s_scr[...] = jnp.zeros_like(s_scr)
    else:
        zero_other_layers()

    incl, _, _ = _seg_masks((c, c), seg)
    ng = ng_ref[...]

    heads = range(GLA_HEADS)
    kq = [slice(hh * GLA_DK, (hh + 1) * GLA_DK) for hh in heads]
    cpi = 2 if (tc // c) % 2 == 0 else 1

    def local(n):
        rows = pl.ds(_aligned(n * c, c), c)
        b = _cumsum_rows(la_ref[rows, :], seg)
        bl = _seg_last_rows(b, seg)
        k_all = h_ref[rows, GLA_KW:2 * GLA_KW]
        q_dec = h_ref[rows, 0:GLA_KW] * jnp.exp(b)
        k_inv = k_all * jnp.exp(-b)
        k_end = k_all * jnp.exp(bl - b)
        e_bl = jnp.exp(bl)
        vs = [h_ref[rows, 2 * GLA_KW + hh * GLA_DV:2 * GLA_KW + (hh + 1) * GLA_DV] for hh in heads]
        atts = [jnp.where(incl, _dot_nt(_bf(q_dec[:, kq[hh]]), _bf(k_inv[:, kq[hh]])), 0.0) for hh in heads]
        os_ = [_dot(_bf(atts[hh]), _bf(vs[hh])) for hh in heads]
        return rows, q_dec, k_end, e_bl, vs, os_

    def state_pass(n, rows, q_dec, k_end, e_bl, vs, os_):
        inter = [[] for _ in heads]
        for s in range(n_seg):
            sq = n * n_seg + s
            sr = slice(s * seg, (s + 1) * seg)
            sts = [get_state(sq, hh) for hh in heads]
            for hh in heads:
                inter[hh].append(_dot(_bf(q_dec[sr, kq[hh]]), _bf(sts[hh])))
            upd = [_dot_tn(_bf(k_end[sr, kq[hh]]), _bf(vs[hh][sr])) for hh in heads]
            for hh in heads:
                ecol = jnp.transpose(jnp.broadcast_to(e_bl[s * seg:s * seg + 1, kq[hh]], (GLA_DK, GLA_DK)))
                ecol = jnp.concatenate([ecol] * (GLA_DV // GLA_DK), axis=1)
                put_state(sq, hh, sts[hh] * ecol + upd[hh])
        for hh in heads:
            o = os_[hh] + (inter[hh][0] if n_seg == 1 else jnp.concatenate(inter[hh], axis=0))
            o = o * lax.rsqrt(jnp.mean(o * o, axis=-1, keepdims=True) + NORM_EPS) * ng
            kr = slice(2 * GLA_KW + GLA_VW + hh * GLA_DV, 2 * GLA_KW + GLA_VW + (hh + 1) * GLA_DV)
            og_scr[rows, hh * GLA_DV:(hh + 1) * GLA_DV] = o * _silu(h_ref[rows, kr])

    def trip(n, carry):
        parts = [local(n * cpi + i) for i in range(cpi)]
        for i in range(cpi):
            state_pass(n * cpi + i, *parts[i])
        return carry

    lax.fori_loop(0, tc // (c * cpi), trip, 0)

    if geo.long:
        _out_ln_body(og_scr, x_ref, wout_ref, lng_ref, lnb_ref, og_ref)

        @pl.when(geo.last_time_block())
        def _():
            sout_ref[out_layer, 0] = s_scr[...]
            zero_other_layers()


def _gla_recur(h, la, norm_g, state_in, layer, n_layers, prev_out, geo, out_ln):
    tail = (GLA_HEADS, GLA_DK, GLA_DV)
    ins = [h, la, norm_g]
    in_specs = [pl.BlockSpec((geo.tc, GLA_MAIN), geo.row_map), pl.BlockSpec((geo.tc, GLA_KW), geo.row_map),
                _const_spec(norm_g)]
    if geo.long:
        ins += list(out_ln)
        in_specs += [pl.BlockSpec((geo.tc, D_MODEL), geo.row_map)] + [_const_spec(a) for a in out_ln[1:]]
    else:
        ins.append(state_in)
        in_specs.append(geo.state_spec(layer, tail))
    aliases = {}
    if prev_out is not None:
        aliases = {len(ins): 1}
        ins.append(prev_out)
        in_specs.append(pl.BlockSpec(memory_space=pl.ANY))
    scratch = [pltpu.VMEM((geo.tc, GLA_VW), F32), pltpu.VMEM(tail, F32)] if geo.long else []
    out_layers = n_layers if prev_out is None else 1
    og, s_out = pl.pallas_call(
        functools.partial(_gla_recur_body, geo=geo, has_alias=prev_out is not None,
                          layer=layer, out_layers=out_layers),
        grid=geo.grid, in_specs=in_specs,
        out_specs=[pl.BlockSpec((geo.tc, GLA_VW), geo.row_map), geo.state_spec(layer, tail, out_layers)],
        out_shape=[jax.ShapeDtypeStruct((geo.n_b * geo.n_l, GLA_VW), F32),
                   jax.ShapeDtypeStruct((n_layers, geo.n_b) + tail, F32)],
        scratch_shapes=scratch, input_output_aliases=aliases,
        compiler_params=_params(geo.sem), name="gla_recur",
    )(*ins)
    return og, s_out


def _block_rows(x, mask):
    return jnp.where(mask, jnp.concatenate([x, x], axis=0), 0.0)


def _mm3_split(lhs, rhs_hi, rhs_lo):
    lh, ll = _split2_bf(lhs)
    return _dot(lh, rhs_hi) + (_dot(lh, rhs_lo) + _dot(ll, rhs_hi))


def _mm3_pair(lhs, x_pair, mask):
    xh, xl = _split2_bf(x_pair)
    return _mm3_split(lhs, _bf(_block_rows(xh.astype(F32), mask)), _bf(_block_rows(xl.astype(F32), mask)))


def _gdn_expand_matrix():
    h = jnp.arange(LANES)[:, None]
    return (jnp.arange(GDN_HEADS * CHUNK)[None, :] // CHUNK == h).astype(BF16)


def _gdn_recur_body(*refs, geo):
    refs = list(refs)
    h_ref, ab_ref, wc_ref, alog_ref, dtb_ref, ng_ref, eall_ref = refs[:7]
    pos = 7
    conv0_ref = s0_ref = x_ref = wout_ref = lng_ref = lnb_ref = None
    if geo.long:
        x_ref, wout_ref, lng_ref, lnb_ref = refs[pos:pos + 4]
        pos += 4
    else:
        conv0_ref, s0_ref = refs[pos:pos + 2]
        pos += 2
    og_ref, convn_ref, sout_ref = refs[pos:pos + 3]
    pos += 3
    qkv_scr, g_scr, beta_scr, u_scr, w_scr, att_scr, egc_scr, kend_scr, qk_scr = refs[pos:pos + 9]
    pos += 9
    carry_scr = s_scr = None
    og_scr = og_ref
    if geo.long:
        carry_scr, og_scr, s_scr = refs[pos:pos + 3]
    c, seg, tc = CHUNK, geo.seg, geo.tc
    n_seg = c // seg
    get_state, put_state = _state_access(geo, s0_ref, sout_ref, s_scr)
    pad = SUBLANES
    lo = pad - (GDN_CONV - 1)
    n_pair = GDN_HEADS // 2
    pw = 2 * c
    hw = GDN_HEADS * c
    cpi = 2 if (tc // c) % 2 == 0 else 1

    if geo.long:
        @pl.when(pl.program_id(1) == 0)
        def _():
            carry_scr[...] = jnp.zeros_like(carry_scr)
            s_scr[...] = jnp.zeros_like(s_scr)

    def window_conv(e, cols, t_in=None):
        conv = e[pad:pad + c] * wc_ref[GDN_CONV - 1:GDN_CONV, cols]
        for j in range(GDN_CONV - 1):
            tap = e[lo + j:lo + j + c]
            if t_in is not None:
                tap = jnp.where(t_in >= GDN_CONV - 1 - j, tap, 0.0)
            conv = conv + tap * wc_ref[j:j + 1, cols]
        return conv

    def conv_chunk(m):
        static = isinstance(m, int)
        r0 = m * c if static else pl.multiple_of(m * c, c)
        for cb in range(GDN_CONV_CH // LANES):
            cols = slice(cb * LANES, (cb + 1) * LANES)
            if static and m == 0:
                e = jnp.concatenate([carry_scr[:, cols], h_ref[0:c, cols]], axis=0)
            elif static:
                e = h_ref[r0 - pad:r0 + c, cols]
            else:
                e = h_ref[pl.ds(pl.multiple_of(m * c - pad, SUBLANES), c + pad), cols]
            qkv_scr[pl.ds(r0, c), cols] = _silu(window_conv(e, cols))

    if geo.long:
        for i in range(cpi):
            conv_chunk(i)

        @pl.when(geo.last_time_block())
        def _():
            convn_ref[0, 0] = h_ref[tc - (GDN_CONV - 1):tc, 0:GDN_CONV_CH]
    else:
        t_in = lax.broadcasted_iota(jnp.int32, (c, LANES), 0) & (seg - 1)
        for cb in range(GDN_CONV_CH // LANES):
            cols = slice(cb * LANES, (cb + 1) * LANES)
            e = jnp.concatenate([jnp.zeros((pad, LANES), F32), h_ref[:, cols]], axis=0)
            qkv_scr[:, cols] = window_conv(e, cols, t_in)
        for sq in range(geo.seqs):
            c0 = conv0_ref[0, sq]
            head = []
            for t in range(GDN_CONV - 1):
                acc = c0[t:t + 1, :] * wc_ref[0:1, :]
                for j in range(1, GDN_CONV - 1 - t):
                    acc = acc + c0[t + j:t + j + 1, :] * wc_ref[j:j + 1, :]
                head.append(acc)
            head.append(jnp.zeros((SUBLANES - (GDN_CONV - 1), GDN_CONV_CH), F32))
            first = slice(sq * seg, sq * seg + SUBLANES)
            qkv_scr[first, :] = qkv_scr[first, :] + jnp.concatenate(head, axis=0)
            convn_ref[0, sq] = h_ref[(sq + 1) * seg - (GDN_CONV - 1):(sq + 1) * seg, 0:GDN_CONV_CH]
        qkv_scr[...] = _silu(qkv_scr[...])

    g_scr[...] = -jnp.exp(alog_ref[...]) * _softplus(ab_ref[:, 0:LANES] + dtb_ref[...])
    beta_scr[...] = _sigmoid(ab_ref[:, LANES:2 * LANES])

    incl, strict, delta = _seg_masks((c, hw), seg)
    eye_pair = jnp.where(delta[:, 0:pw], 1.0, 0.0)
    r2 = lax.broadcasted_iota(jnp.int32, (2 * c, pw), 0) >> 6
    mask_cc = r2 == (lax.broadcasted_iota(jnp.int32, (2 * c, pw), 1) >> 6)
    r3 = lax.broadcasted_iota(jnp.int32, (2 * c, 2 * GDN_DK), 0) >> 6
    mask_cd = r3 == (lax.broadcasted_iota(jnp.int32, (2 * c, 2 * GDN_DK), 1) >> 7)
    ones_c = jnp.ones((c, c), BF16)
    ng = ng_ref[...]

    def gate_forms(rows):
        gc = _cumsum_rows(g_scr[rows, :], seg)
        e_all = eall_ref[...]
        sh, sm, sl = _split3_bf(jnp.concatenate([gc, beta_scr[rows, :]], axis=0))
        ex = _dot(sh, e_all) + (_dot(sm, e_all) + _dot(sl, e_all))
        gcol, bcol = ex[0:c], ex[c:2 * c]
        gwide = jnp.concatenate([jnp.broadcast_to(gc[:, hh:hh + 1], (c, GDN_DK)) for hh in range(GDN_HEADS)],
                                axis=1)
        th, tm, tl = _split3_bf(jnp.concatenate(
            [jnp.where(delta, gcol, 0.0), jnp.where(delta, bcol, 0.0)], axis=1))
        rowf = _dot(ones_c, th) + (_dot(ones_c, tm) + _dot(ones_c, tl))
        grow, brow = rowf[:, 0:hw], rowf[:, hw:]
        dec = jnp.where(incl, jnp.exp(jnp.where(incl, gcol - grow, 0.0)), 0.0)
        egc_scr[rows, :] = jnp.exp(gwide)
        eend = jnp.exp(_seg_last_rows(gwide, seg) - gwide)
        return dec, bcol, brow, jnp.exp(grow), eend

    def local(n, side=None):
        tick = (lambda: None) if side is None else (lambda: next(side, None))
        chunk_rows = [pl.ds(_aligned((n * cpi + i) * c, c), c) for i in range(cpi)]
        forms = [gate_forms(rows) for rows in chunk_rows]
        units = [(i, p) for i in range(cpi) for p in range(n_pair)]
        tick()

        a_pairs = []
        for i, p in units:
            rows = chunk_rows[i]
            dec, bcol, _, _, eend = forms[i]
            qs, ks = [], []
            for hh in (2 * p, 2 * p + 1):
                kq = slice(hh * GDN_DK, (hh + 1) * GDN_DK)
                kk = slice(GDN_KW + hh * GDN_DK, GDN_KW + (hh + 1) * GDN_DK)
                q = qkv_scr[rows, kq]
                k = qkv_scr[rows, kk]
                q = q * lax.rsqrt(jnp.sum(q * q, axis=-1, keepdims=True) + NORM_EPS) * GDN_DK ** -0.5
                k = k * lax.rsqrt(jnp.sum(k * k, axis=-1, keepdims=True) + NORM_EPS)
                qk_scr[rows, kq] = q
                qk_scr[rows, kk] = k
                kend_scr[rows, kq] = k * eend[:, kq]
                qs.append(q)
                ks.append(k)
            kh = _bf(jnp.concatenate(ks, axis=1))
            kbd_h = _bf(_block_rows(kh.astype(F32), mask_cd))
            kkqk = _dot_nt(jnp.concatenate([kh, _bf(jnp.concatenate(qs, axis=1))], axis=0), kbd_h)
            cs = slice(p * pw, (p + 1) * pw)
            a_pairs.append(jnp.where(strict[:, cs], kkqk[0:c] * dec[:, cs] * bcol[:, cs], 0.0))
            att_scr[rows, cs] = kkqk[c:2 * c] * dec[:, cs]
        tick()
        ps = [eye_pair - a for a in a_pairs]
        xs = [_mm3_pair(a, a, mask_cc) for a in a_pairs]
        tick()
        steps = int(math.log2(seg)) - 1
        for s in range(steps):
            if s < steps - 1:
                rs = [_mm3_pair(jnp.concatenate([pp, x], axis=0), x, mask_cc) for pp, x in zip(ps, xs)]
                ps = [pp + r[0:c] for pp, r in zip(ps, rs)]
                xs = [r[c:2 * c] for r in rs]
            else:
                ps = [pp + _mm3_pair(pp, x, mask_cc) for pp, x in zip(ps, xs)]
            tick()
        for t_inv, (i, p) in zip(ps, units):
            rows = chunk_rows[i]
            _, _, brow, egrow, _ = forms[i]
            cs = slice(p * pw, (p + 1) * pw)
            vs = slice(p * 2 * GDN_DV, (p + 1) * 2 * GDN_DV)
            t_b = t_inv * brow[:, cs]
            vh, vl = _split2_bf(qkv_scr[rows, 2 * GDN_KW + p * 2 * GDN_DV:2 * GDN_KW + (p + 1) * 2 * GDN_DV])
            u_scr[rows, vs] = _mm3_split(t_b, _bf(_block_rows(vh.astype(F32), mask_cd)),
                                         _bf(_block_rows(vl.astype(F32), mask_cd)))
            kh, kl = _split2_bf(qk_scr[rows, GDN_KW + p * 2 * GDN_DK:GDN_KW + (p + 1) * 2 * GDN_DK])
            w_scr[rows, vs] = _mm3_split(t_b * egrow[:, cs], _bf(_block_rows(kh.astype(F32), mask_cd)),
                                         _bf(_block_rows(kl.astype(F32), mask_cd)))
        if geo.long:
            for i in range(cpi):
                nxt, cap = (n + 1) * cpi + i, tc // c - cpi + i
                conv_chunk(min(nxt, cap) if isinstance(n, int) else jnp.minimum(nxt, cap))
        if side is not None:
            for _ in side:
                pass

    def recur(n):
        r0 = _aligned(n * c, c)
        rows = pl.ds(r0, c)
        heads = range(GDN_HEADS)
        kqs = [slice(hh * GDN_DK, (hh + 1) * GDN_DK) for hh in heads]
        vn_seg = [[] for _ in heads]
        qs_seg = [[] for _ in heads]
        for s in range(n_seg):
            sq = n * n_seg + s
            sr = pl.ds(_aligned(r0 + s * seg, SUBLANES), seg)
            tail = pl.ds(_aligned(r0 + (s + 1) * seg - SUBLANES, SUBLANES), SUBLANES)
            sts = [get_state(sq, hh) for hh in heads]
            m1s = [_dot(_bf(jnp.concatenate([w_scr[sr, kqs[hh]], qk_scr[sr, kqs[hh]]], axis=0)), _bf(sts[hh]))
                   for hh in heads]
            yield
            vn = [u_scr[sr, kqs[hh]] - m1s[hh][0:seg] for hh in heads]
            upd = [_dot_tn(_bf(kend_scr[sr, kqs[hh]]), _bf(vn[hh])) for hh in heads]
            yield
            for hh in heads:
                e_last = egc_scr[tail, kqs[hh]][SUBLANES - 1:SUBLANES, :]
                put_state(sq, hh, sts[hh] * e_last + upd[hh])
                vn_seg[hh].append(vn[hh])
                qs_seg[hh].append(m1s[hh][seg:2 * seg])
        join = lambda parts: parts[0] if n_seg == 1 else jnp.concatenate(parts, axis=0)
        o_pairs = []
        for p in range(n_pair):
            pair = (2 * p, 2 * p + 1)
            vs = slice(p * 2 * GDN_DV, (p + 1) * 2 * GDN_DV)
            vbd = _bf(_block_rows(jnp.concatenate([join(vn_seg[hh]) for hh in pair], axis=1), mask_cd))
            o_pairs.append(_dot(_bf(att_scr[rows, p * pw:(p + 1) * pw]), vbd)
                           + egc_scr[rows, vs] * jnp.concatenate([join(qs_seg[hh]) for hh in pair], axis=1))
        yield
        for p in range(n_pair):
            for i, hh in enumerate((2 * p, 2 * p + 1)):
                o = o_pairs[p][:, i * GDN_DV:(i + 1) * GDN_DV]
                o = o * lax.rsqrt(jnp.mean(o * o, axis=-1, keepdims=True) + NORM_EPS) * ng
                kz = slice(GDN_CONV_CH + hh * GDN_DV, GDN_CONV_CH + (hh + 1) * GDN_DV)
                og_scr[rows, hh * GDN_DV:(hh + 1) * GDN_DV] = o * _silu(h_ref[rows, kz])
        yield

    def recur_trip(n):
        for i in range(cpi):
            yield from recur(n * cpi + i)

    def drain(gen):
        for _ in gen:
            pass

    n_trip = tc // (c * cpi)
    local(0)

    def trip(n, carry):
        local(n, recur_trip(n - 1))
        return carry

    lax.fori_loop(1, n_trip, trip, 0)
    drain(recur_trip(n_trip - 1))

    if geo.long:
        carry_scr[...] = h_ref[tc - pad:tc, 0:GDN_CONV_CH]
        _out_ln_body(og_scr, x_ref, wout_ref, lng_ref, lnb_ref, og_ref)

        @pl.when(geo.last_time_block())
        def _():
            sout_ref[0, 0] = s_scr[...]


def _gdn_recur(h, ab, w_conv, a_log, dt_bias, norm_g, conv_in, state_in, layer, n_layers, geo, out_ln):
    tail = (GDN_HEADS, GDN_DK, GDN_DV)
    cv_tail = (GDN_CONV - 1, GDN_CONV_CH)
    consts = [w_conv, a_log, dt_bias, norm_g, _gdn_expand_matrix()]
    ins = [h, ab] + consts
    in_specs = [pl.BlockSpec((geo.tc, GDN_MAIN), geo.row_map), pl.BlockSpec((geo.tc, 2 * LANES), geo.row_map)]
    in_specs += [_const_spec(a) for a in consts]
    tc = geo.tc
    if geo.long:
        ins += list(out_ln)
        in_specs += [pl.BlockSpec((tc, D_MODEL), geo.row_map)] + [_const_spec(a) for a in out_ln[1:]]
    else:
        assert tc == CHUNK
        ins += [conv_in, state_in]
        in_specs += [geo.state_spec(layer, cv_tail), geo.state_spec(layer, tail)]
    scratch = [pltpu.VMEM((tc, GDN_CONV_CH), F32),
               pltpu.VMEM((tc, LANES), F32), pltpu.VMEM((tc, LANES), F32),
               pltpu.VMEM((tc, GDN_VW), F32), pltpu.VMEM((tc, GDN_KW), F32),
               pltpu.VMEM((tc, GDN_HEADS * CHUNK), F32), pltpu.VMEM((tc, GDN_KW), F32),
               pltpu.VMEM((tc, GDN_KW), F32), pltpu.VMEM((tc, 2 * GDN_KW), F32)]
    if geo.long:
        scratch += [pltpu.VMEM((SUBLANES, GDN_CONV_CH), F32), pltpu.VMEM((tc, GDN_VW), F32),
                    pltpu.VMEM(tail, F32)]
    return pl.pallas_call(
        functools.partial(_gdn_recur_body, geo=geo),
        grid=geo.grid, in_specs=in_specs,
        out_specs=[pl.BlockSpec((tc, GDN_VW), geo.row_map), geo.state_spec(layer, cv_tail),
                   geo.state_spec(layer, tail)],
        out_shape=[jax.ShapeDtypeStruct((geo.n_b * geo.n_l, GDN_VW), F32),
                   jax.ShapeDtypeStruct((n_layers, geo.n_b) + cv_tail, F32),
                   jax.ShapeDtypeStruct((n_layers, geo.n_b) + tail, F32)],
        scratch_shapes=scratch,
        compiler_params=_params(geo.sem), name="gdn_recur",
    )(*ins)


def _s5_body(*refs, tt, nb, has_init):
    if has_init:
        (x_ref, win_ref, wb_ref, are_ref, aim_ref, wc_ref, d_ref, wglu_ref, bglu_ref, wout_ref,
         lng_ref, lnb_ref, h0re_ref, h0im_ref, y_ref, hre_ref, him_ref,
         x_scr, u_scr, z_scr, bu_scr, y_scr, st_scr) = refs
    else:
        (x_ref, win_ref, wb_ref, are_ref, aim_ref, wc_ref, d_ref, wglu_ref, bglu_ref, wout_ref,
         lng_ref, lnb_ref, y_ref, hre_ref, him_ref,
         x_scr, u_scr, z_scr, bu_scr, y_scr, st_scr) = refs
    tb = pl.program_id(1)
    ns = S5_KT_STATES

    @pl.when(tb == 0)
    def _():
        if not has_init:
            st_scr[...] = jnp.zeros_like(st_scr)
        else:
            for kt in range(S5_KT):
                st_scr[:, 2 * kt * ns:(2 * kt + 1) * ns] = h0re_ref[0, :, kt * ns:(kt + 1) * ns]
                st_scr[:, (2 * kt + 1) * ns:(2 * kt + 2) * ns] = h0im_ref[0, :, kt * ns:(kt + 1) * ns]

    x_scr[...] = jnp.swapaxes(x_ref[...], 0, 1).reshape(tt * nb, D_MODEL)
    h = _dot(_bf(x_scr[...]), win_ref[...])
    u_scr[...] = h[:, 0:S5_WIDTH]
    z_scr[...] = h[:, S5_WIDTH:]

    def input_map(kt):
        cols = slice(kt * S5_KT_W, (kt + 1) * S5_KT_W)
        bu_scr[kt % 2] = _dot(_bf(u_scr[:, cols]), wb_ref[kt])

    def scan(kt):
        bu = bu_scr.at[kt % 2]
        a_re = jnp.broadcast_to(are_ref[kt], (SUBLANES, ns))
        a_im = jnp.broadcast_to(aim_ref[kt], (SUBLANES, ns))
        base = kt * 2 * ns
        for rb in range(nb // SUBLANES):
            st_rows = slice(rb * SUBLANES, (rb + 1) * SUBLANES)
            h_re = st_scr[st_rows, base:base + ns]
            h_im = st_scr[st_rows, base + ns:base + 2 * ns]
            for t in range(tt):
                rows = slice(t * nb + rb * SUBLANES, t * nb + (rb + 1) * SUBLANES)
                h_re, h_im = (a_re * h_re - a_im * h_im + bu[rows, 0:ns],
                              a_re * h_im + a_im * h_re + bu[rows, ns:2 * ns])
                bu[rows, 0:ns] = h_re
                bu[rows, ns:2 * ns] = h_im
            st_scr[st_rows, base:base + ns] = h_re
            st_scr[st_rows, base + ns:base + 2 * ns] = h_im

    def output_map(kt):
        cols = slice(kt * S5_KT_W, (kt + 1) * S5_KT_W)
        y_scr[:, cols] = _dot(_bf(bu_scr[kt % 2]), wc_ref[kt]) + d_ref[:, cols] * u_scr[:, cols]

    input_map(0)
    for kt in range(S5_KT):
        if kt + 1 < S5_KT:
            input_map(kt + 1)
        scan(kt)
        output_map(kt)

    y = _gelu_tanh(y_scr[...])
    yg = _dot(_bf(y), wglu_ref[...]) + bglu_ref[...]
    y = yg[:, 0:S5_WIDTH] * _sigmoid(yg[:, S5_WIDTH:]) * _silu(z_scr[...])
    out = _dot(_bf(y), wout_ref[...])
    y = _layer_norm(ALPHA_RES * x_scr[...] + out, lng_ref[...], lnb_ref[...])
    y_ref[...] = jnp.swapaxes(y.reshape(tt, nb, D_MODEL), 0, 1)

    @pl.when(tb == pl.num_programs(1) - 1)
    def _():
        for kt in range(S5_KT):
            hre_ref[0, :, kt * ns:(kt + 1) * ns] = st_scr[:, 2 * kt * ns:(2 * kt + 1) * ns]
            him_ref[0, :, kt * ns:(kt + 1) * ns] = st_scr[:, (2 * kt + 1) * ns:(2 * kt + 2) * ns]


def _s5_discretize(lam_re, lam_im, log_dt, b_re, b_im, c_re, c_im):
    dt = jnp.exp(log_dt)[:, None]
    mag = jnp.exp(lam_re * dt)
    ab_re, ab_im = mag * jnp.cos(lam_im * dt), mag * jnp.sin(lam_im * dt)
    den = jnp.square(lam_re) + jnp.square(lam_im)
    num_re = ab_re - 1.0
    coef_re = (num_re * lam_re + ab_im * lam_im) / den
    coef_im = (ab_im * lam_re - num_re * lam_im) / den
    bb_re = coef_re[..., None] * b_re - coef_im[..., None] * b_im
    bb_im = coef_re[..., None] * b_im + coef_im[..., None] * b_re
    gl = S5_GROUPS // S5_KT
    rep = (jnp.arange(S5_KT_STATES)[None, :] % S5_STATE == jnp.arange(S5_STATE)[:, None]).astype(F32)
    same = jnp.arange(S5_KT_W)[:, None] // S5_GROUP == jnp.arange(S5_KT_STATES)[None, :] // S5_STATE
    exact = lax.Precision.HIGHEST

    def block_b(bb):
        t = bb.reshape(S5_KT, gl, S5_STATE, S5_GROUP).transpose(0, 1, 3, 2).reshape(S5_KT, S5_KT_W, S5_STATE)
        return jnp.where(same, jnp.einsum("krp,pn->krn", t, rep, precision=exact), 0.0)

    def block_c(cc):
        t = cc.reshape(S5_KT, gl, S5_GROUP, S5_STATE).transpose(0, 3, 1, 2).reshape(S5_KT, S5_STATE, S5_KT_W)
        return jnp.where(same.T, jnp.einsum("pn,kpr->knr", rep, t, precision=exact), 0.0)

    w_b = jnp.concatenate([block_b(bb_re), block_b(bb_im)], axis=2).astype(BF16)
    w_c = jnp.concatenate([block_c(c_re), -block_c(c_im)], axis=1).astype(BF16)
    a_re = ab_re.reshape(S5_KT, 1, S5_KT_STATES)
    a_im = ab_im.reshape(S5_KT, 1, S5_KT_STATES)
    return w_b, w_c, a_re, a_im


def _s5_layer(x, prep, w_in, d_vec, w_glu, b_glu, w_out, ln_g, ln_b, h0, n_b, n_l):
    w_b, w_c, a_re, a_im = prep
    if n_l >= S5_LONG_STEPS:
        nb, tt = n_b, S5_LONG_STEPS
    else:
        nb, tt = min(S5_SHORT_SEQS, n_b), n_l
    assert nb % SUBLANES == 0 and tt % SUBLANES == 0
    rows = tt * nb
    has_init = h0 is not None
    consts = [w_in, w_b, a_re, a_im, w_c, d_vec, w_glu, b_glu, w_out, ln_g, ln_b]
    x_spec = pl.BlockSpec((nb, tt, D_MODEL), lambda b, t: (b, t, 0))
    st_spec = pl.BlockSpec((1, nb, S5_NSTATE), lambda b, t: (0, b, 0))
    ins = [x.reshape(n_b, n_l, D_MODEL)] + consts
    in_specs = [x_spec] + [_const_spec(a) for a in consts]
    if has_init:
        ins += list(h0)
        in_specs += [st_spec, st_spec]
    y, h_re, h_im = pl.pallas_call(
        functools.partial(_s5_body, tt=tt, nb=nb, has_init=has_init),
        grid=(n_b // nb, n_l // tt), in_specs=in_specs,
        out_specs=[x_spec, st_spec, st_spec],
        out_shape=[jax.ShapeDtypeStruct((n_b, n_l, D_MODEL), F32),
                   jax.ShapeDtypeStruct((1, n_b, S5_NSTATE), F32),
                   jax.ShapeDtypeStruct((1, n_b, S5_NSTATE), F32)],
        scratch_shapes=[pltpu.VMEM((rows, D_MODEL), F32),
                        pltpu.VMEM((rows, S5_WIDTH), F32), pltpu.VMEM((rows, S5_WIDTH), F32),
                        pltpu.VMEM((2, rows, 2 * S5_KT_STATES), F32), pltpu.VMEM((rows, S5_WIDTH), F32),
                        pltpu.VMEM((nb, 2 * S5_NSTATE), F32)],
        compiler_params=_params(("parallel", "arbitrary")), name="s5_layer",
    )(*ins)
    return y.reshape(n_b * n_l, D_MODEL), h_re, h_im


def _row2(v):
    return v.reshape(1, -1).astype(F32)


def _pad_cols(w, n):
    return jnp.concatenate([w, jnp.zeros((w.shape[0], n - w.shape[1]), w.dtype)], axis=1)


def _trunk(x, n_b, n_l, states, wts):
    geo = _Geometry(n_b, n_l)
    n_gla, n_gdn, n_s5 = (DEPTH + 2) // 3, (DEPTH + 1) // 3, DEPTH // 3
    assert n_gdn == 1 and n_s5 == 1
    s_gla = s_gdn = s_conv = s_re = s_im = None
    for i in range(DEPTH):
        j, kind = divmod(i, 3)
        ln_g, ln_b = _row2(wts["ln_g"][i]), _row2(wts["ln_b"][i])
        if kind == 0:
            w_in = wts["gla_w_in"][j]
            w_main = _bf(w_in)
            w_lr = _bf(_pad_cols(w_in[:, GLA_MAIN:], LANES))
            w_a2 = _bf(jnp.concatenate(
                [wts["gla_w_a2"][j], jnp.zeros((LANES - GLA_LOWRANK, GLA_KW), F32)], axis=0))
            h, la = _rowwise_call(_gla_proj_body, "gla_proj", [x],
                                  [w_main, w_lr, w_a2, _row2(wts["gla_b_a"][j])], [GLA_MAIN, GLA_KW])
            w_out = _bf(wts["gla_w_out"][j])
            og, s_gla = _gla_recur(h, la, _row2(wts["gla_norm_g"][j]),
                                   None if states is None else states[0], j, n_gla, s_gla, geo,
                                   (x, w_out, ln_g, ln_b))
        elif kind == 1:
            w_in = wts["gdn_w_in"][j]
            w_main = _bf(w_in)
            w_ab = _bf(jnp.concatenate(
                [_pad_cols(w_in[:, GDN_MAIN:GDN_MAIN + GDN_HEADS], LANES),
                 _pad_cols(w_in[:, GDN_MAIN + GDN_HEADS:], LANES)], axis=1))
            h, ab = _rowwise_call(_gdn_proj_body, "gdn_proj", [x], [w_main, w_ab], [GDN_MAIN, 2 * LANES])
            w_out = _bf(wts["gdn_w_out"][j])
            og, s_conv, s_gdn = _gdn_recur(
                h, ab, wts["gdn_w_conv"][j].astype(F32),
                _pad_cols(_row2(wts["gdn_a_log"][j]), LANES), _pad_cols(_row2(wts["gdn_dt_bias"][j]), LANES),
                _row2(wts["gdn_norm_g"][j]),
                None if states is None else states[2], None if states is None else states[1],
                j, n_gdn, geo, (x, w_out, ln_g, ln_b))
        else:
            prep = _s5_discretize(wts["s5_lam_re"][j].astype(F32), wts["s5_lam_im"][j].astype(F32),
                                  wts["s5_log_dt"][j].astype(F32), wts["s5_b_re"][j].astype(F32),
                                  wts["s5_b_im"][j].astype(F32), wts["s5_c_re"][j].astype(F32),
                                  wts["s5_c_im"][j].astype(F32))
            h0 = None if states is None else (states[3].reshape(n_s5, n_b, S5_NSTATE),
                                              states[4].reshape(n_s5, n_b, S5_NSTATE))
            x, h_re, h_im = _s5_layer(x, prep, _bf(wts["s5_w_in"][j]), _row2(wts["s5_d"][j]),
                                      _bf(wts["s5_w_glu"][j]), _row2(wts["s5_b_glu"][j]),
                                      _bf(wts["s5_w_out"][j]), ln_g, ln_b, h0, n_b, n_l)
            s_re = h_re.reshape(n_s5, n_b, S5_GROUPS, S5_STATE)
            s_im = h_im.reshape(n_s5, n_b, S5_GROUPS, S5_STATE)
            continue
        if geo.long:
            x = og
        else:
            (x,) = _rowwise_call(_out_ln_body, "out_ln", [og, x], [w_out, ln_g, ln_b], [D_MODEL])
    return x, s_gla, s_gdn, s_conv, s_re, s_im


def kernel(x_prompt, x_sample, state_gla, state_gdn, state_gdn_conv, state_s5_re, state_s5_im,
           ln_g, ln_b, gla_w_in, gla_w_a2, gla_b_a, gla_norm_g, gla_w_out,
           gdn_w_in, gdn_w_conv, gdn_a_log, gdn_dt_bias, gdn_norm_g, gdn_w_out,
           s5_w_in, s5_lam_re, s5_lam_im, s5_log_dt, s5_b_re, s5_b_im, s5_c_re, s5_c_im,
           s5_d, s5_w_glu, s5_b_glu, s5_w_out):
    wts = dict(ln_g=ln_g, ln_b=ln_b,
               gla_w_in=gla_w_in, gla_w_a2=gla_w_a2, gla_b_a=gla_b_a, gla_norm_g=gla_norm_g,
               gla_w_out=gla_w_out,
               gdn_w_in=gdn_w_in, gdn_w_conv=gdn_w_conv, gdn_a_log=gdn_a_log, gdn_dt_bias=gdn_dt_bias,
               gdn_norm_g=gdn_norm_g, gdn_w_out=gdn_w_out,
               s5_w_in=s5_w_in, s5_lam_re=s5_lam_re, s5_lam_im=s5_lam_im, s5_log_dt=s5_log_dt,
               s5_b_re=s5_b_re, s5_b_im=s5_b_im, s5_c_re=s5_c_re, s5_c_im=s5_c_im, s5_d=s5_d,
               s5_w_glu=s5_w_glu, s5_b_glu=s5_b_glu, s5_w_out=s5_w_out)

    bp, lp, _ = x_prompt.shape
    yp, p_gla, p_gdn, p_conv, p_re, p_im = _trunk(
        x_prompt.reshape(bp * lp, D_MODEL), bp, lp, None, wts)
    bs, ls, _ = x_sample.shape
    ys, s_gla, s_gdn, s_conv, s_re, s_im = _trunk(
        x_sample.reshape(bs * ls, D_MODEL), bs, ls,
        (state_gla, state_gdn, state_gdn_conv, state_s5_re, state_s5_im), wts)
    return (yp.reshape(bp, lp, D_MODEL), ys.reshape(bs, ls, D_MODEL),
            p_gla, p_gdn, p_conv, p_re, p_im, s_gla, s_gdn, s_conv, s_re, s_im)
```

```python
import functools
import math

import jax
import jax.numpy as jnp
from jax import lax
from jax.experimental import pallas as pl
from jax.experimental.pallas import tpu as pltpu

F32 = jnp.float32
BF16 = jnp.bfloat16

D_MODEL = 1024
DEPTH = 4
ALPHA_RES = (2 * DEPTH) ** 0.25
LN_EPS = 1e-5
NORM_EPS = 1e-6
CHUNK = 64

GLA_HEADS = 4
GLA_KW = 512
GLA_VW = 1024
GLA_DK = 128
GLA_DV = 256
GLA_LOWRANK = 16
GLA_TAU = 16.0
GLA_MAIN = 2 * GLA_KW + 2 * GLA_VW

GDN_HEADS = 8
GDN_DK = 128
GDN_DV = 128
GDN_KW = 1024
GDN_VW = 1024
GDN_CONV = 4
GDN_CONV_CH = 3072
GDN_MAIN = GDN_CONV_CH + GDN_VW

S5_WIDTH = 1024
S5_GROUP = 16
S5_GROUPS = 64
S5_STATE = 64
S5_KT = 4
S5_KT_W = S5_WIDTH // S5_KT
S5_KT_STATES = (S5_GROUPS // S5_KT) * S5_STATE
S5_NSTATE = S5_GROUPS * S5_STATE

LANES = 128
SUBLANES = 8
VMEM_LIMIT = 56 * 1024 * 1024
ROW_BLOCK = 512
LONG_TIME_BLOCK = 512
S5_LONG_STEPS = 64
S5_SHORT_SEQS = 64


def _bf(x):
    return x.astype(BF16)


def _dot(a, b):
    return jnp.dot(a, b, preferred_element_type=F32)


def _dot_nt(a, b):
    return lax.dot_general(a, b, (((1,), (1,)), ((), ())), preferred_element_type=F32)


def _dot_tn(a, b):
    return lax.dot_general(a, b, (((0,), (0,)), ((), ())), preferred_element_type=F32)


def _split2_bf(x):
    h = x.astype(BF16)
    return h, (x - h.astype(F32)).astype(BF16)


def _split3_bf(x):
    h = x.astype(BF16)
    r = x - h.astype(F32)
    m = r.astype(BF16)
    return h, m, (r - m.astype(F32)).astype(BF16)


def _sigmoid(x):
    return 1.0 / (1.0 + jnp.exp(-x))


def _silu(x):
    return x * _sigmoid(x)


def _softplus(x):
    return jnp.maximum(x, 0.0) + jnp.log(1.0 + jnp.exp(-jnp.abs(x)))


def _gelu_tanh(x):
    return 0.5 * x * (1.0 + jnp.tanh(math.sqrt(2.0 / math.pi) * (x + 0.044715 * (x * x * x))))


def _layer_norm(x, g, b):
    mu = jnp.mean(x, axis=-1, keepdims=True)
    xc = x - mu
    var = jnp.mean(xc * xc, axis=-1, keepdims=True)
    return xc * lax.rsqrt(var + LN_EPS) * g + b


def _cumsum_rows(x, seg):
    row = lax.broadcasted_iota(jnp.int32, x.shape, 0) & (seg - 1)
    s = 1
    while s < seg:
        x = x + jnp.where(row >= s, pltpu.roll(x, s, axis=0), 0.0)
        s *= 2
    return x


def _seg_last_rows(x, seg):
    c, w = x.shape
    if seg == c:
        return jnp.broadcast_to(x[c - 1:c, :], (c, w))
    x3 = x.reshape(c // seg, seg, w)
    return jnp.broadcast_to(x3[:, seg - 1:seg, :], x3.shape).reshape(c, w)


def _seg_masks(shape, seg):
    ri = lax.broadcasted_iota(jnp.int32, shape, 0)
    li = lax.broadcasted_iota(jnp.int32, shape, 1) & (CHUNK - 1)
    shift = int(math.log2(seg))
    same = (ri >> shift) == (li >> shift)
    return same & (ri >= li), same & (ri > li), ri == li


def _aligned(v, m):
    return v if isinstance(v, int) else pl.multiple_of(v, m)


def _const_spec(arr):
    nd = arr.ndim
    return pl.BlockSpec(arr.shape, lambda *_: (0,) * nd, pipeline_mode=pl.Buffered(1))


def _params(sem):
    return pltpu.CompilerParams(dimension_semantics=sem, vmem_limit_bytes=VMEM_LIMIT)


def _rowwise_call(body, name, rows, consts, out_widths):
    n = rows[0].shape[0]
    tm = min(ROW_BLOCK, n)
    spec = lambda w: pl.BlockSpec((tm, w), lambda i: (i, 0))
    return pl.pallas_call(
        body, grid=(n // tm,),
        in_specs=[spec(r.shape[1]) for r in rows] + [_const_spec(c) for c in consts],
        out_specs=[spec(w) for w in out_widths],
        out_shape=[jax.ShapeDtypeStruct((n, w), F32) for w in out_widths],
        compiler_params=_params(("parallel",)), name=name,
    )(*rows, *consts)


def _gla_proj_body(x_ref, w_ref, wlr_ref, wa2_ref, ba_ref, h_ref, la_ref):
    xb = _bf(x_ref[...])
    lr = _dot(xb, wlr_ref[...])
    pre = _dot(_bf(lr), wa2_ref[...]) + ba_ref[...]
    la_ref[...] = -_softplus(-pre) * (1.0 / GLA_TAU)
    h = _dot(xb, w_ref[:, 0:GLA_MAIN])
    h_ref[:, :GLA_KW] = h[:, :GLA_KW] * GLA_DK ** -0.5
    h_ref[:, GLA_KW:] = h[:, GLA_KW:]


def _gdn_proj_body(x_ref, w_ref, wab_ref, h_ref, ab_ref):
    xb = _bf(x_ref[...])
    h_ref[...] = _dot(xb, w_ref[:, 0:GDN_MAIN])
    ab_ref[...] = _dot(xb, wab_ref[...])


def _out_ln_body(og_ref, x_ref, w_ref, g_ref, b_ref, y_ref):
    n = og_ref.shape[0]
    halves = [slice(0, n // 2), slice(n // 2, n)]
    outs = [_dot(_bf(og_ref[r, :]), w_ref[...]) for r in halves]
    for r, out in zip(halves, outs):
        y_ref[r, :] = _layer_norm(ALPHA_RES * x_ref[r, :] + out, g_ref[...], b_ref[...])


class _Geometry:
    def __init__(self, n_b, n_l):
        self.n_b, self.n_l = n_b, n_l
        self.long = n_l >= CHUNK
        if self.long:
            assert n_l % LONG_TIME_BLOCK == 0
            self.seg = CHUNK
            self.tc = LONG_TIME_BLOCK
            self.n_t = n_l // self.tc
            self.grid = (n_b, self.n_t)
            self.sem = ("parallel", "arbitrary")
            self.seqs = 1
            self.row_map = lambda b, t: (b * self.n_t + t, 0)
            self.seq_block = lambda b, t: b
        else:
            assert CHUNK % n_l == 0 and n_l % SUBLANES == 0 and (n_b * n_l) % CHUNK == 0
            self.seg = n_l
            self.tc = CHUNK
            self.grid = (n_b * n_l // self.tc,)
            self.sem = ("parallel",)
            self.seqs = self.tc // n_l
            self.row_map = lambda i: (i, 0)
            self.seq_block = lambda i: i

    def state_spec(self, layer, tail, n_layers=1):
        zeros = (0,) * len(tail)
        first = layer if n_layers == 1 else 0
        return pl.BlockSpec((n_layers, self.seqs) + tail, lambda *g: (first, self.seq_block(*g)) + zeros)

    def last_time_block(self):
        return pl.program_id(1) == pl.num_programs(1) - 1 if self.long else None


def _state_access(geo, s0_ref, sout_ref, s_scr, out_layer=0):
    if geo.long:
        return (lambda sq, hh: s_scr[hh]), (lambda sq, hh, v: s_scr.__setitem__(hh, v))
    return ((lambda sq, hh: s0_ref[0, sq, hh]),
            (lambda sq, hh, v: sout_ref.__setitem__((out_layer, sq, hh), v)))


def _gla_recur_body(*refs, geo, has_alias, layer, out_layers):
    refs = list(refs)
    h_ref, la_ref, ng_ref = refs[:3]
    pos = 3
    s0_ref = x_ref = wout_ref = lng_ref = lnb_ref = None
    if geo.long:
        x_ref, wout_ref, lng_ref, lnb_ref = refs[pos:pos + 4]
        pos += 4
    else:
        s0_ref = refs[pos]
        pos += 1
    if has_alias:
        pos += 1
    og_ref, sout_ref = refs[pos:pos + 2]
    og_scr, s_scr = refs[pos + 2:pos + 4] if geo.long else (og_ref, None)
    c, seg, tc = CHUNK, geo.seg, geo.tc
    n_seg = c // seg
    out_layer = layer if out_layers > 1 else 0
    get_state, put_state = _state_access(geo, s0_ref, sout_ref, s_scr, out_layer)

    def zero_other_layers():
        for other in range(out_layers):
            if other != out_layer:
                sout_ref[other] = jnp.zeros(sout_ref.shape[1:], F32)

    if geo.long:
        @pl.when(pl.program_id(1) == 0)
        def _():
            s_scr[...] = jnp.zeros_like(s_scr)
    else:
        zero_other_layers()

    incl, _, _ = _seg_masks((c, c), seg)
    ng = ng_ref[...]

    heads = range(GLA_HEADS)
    kq = [slice(hh * GLA_DK, (hh + 1) * GLA_DK) for hh in heads]
    cpi = 2 if (tc // c) % 2 == 0 else 1

    def local(n):
        rows = pl.ds(_aligned(n * c, c), c)
        b = _cumsum_rows(la_ref[rows, :], seg)
        bl = _seg_last_rows(b, seg)
        k_all = h_ref[rows, GLA_KW:2 * GLA_KW]
        q_dec = h_ref[rows, 0:GLA_KW] * jnp.exp(b)
        k_inv = k_all * jnp.exp(-b)
        k_end = k_all * jnp.exp(bl - b)
        e_bl = jnp.exp(bl)
        vs = [h_ref[rows, 2 * GLA_KW + hh * GLA_DV:2 * GLA_KW + (hh + 1) * GLA_DV] for hh in heads]
        atts = [jnp.where(incl, _dot_nt(_bf(q_dec[:, kq[hh]]), _bf(k_inv[:, kq[hh]])), 0.0) for hh in heads]
        os_ = [_dot(_bf(atts[hh]), _bf(vs[hh])) for hh in heads]
        return rows, q_dec, k_end, e_bl, vs, os_

    def state_pass(n, rows, q_dec, k_end, e_bl, vs, os_):
        inter = [[] for _ in heads]
        for s in range(n_seg):
            sq = n * n_seg + s
            sr = slice(s * seg, (s + 1) * seg)
            sts = [get_state(sq, hh) for hh in heads]
            for hh in heads:
                inter[hh].append(_dot(_bf(q_dec[sr, kq[hh]]), _bf(sts[hh])))
            upd = [_dot_tn(_bf(k_end[sr, kq[hh]]), _bf(vs[hh][sr])) for hh in heads]
            for hh in heads:
                ecol = jnp.transpose(jnp.broadcast_to(e_bl[s * seg:s * seg + 1, kq[hh]], (GLA_DK, GLA_DK)))
                ecol = jnp.concatenate([ecol] * (GLA_DV // GLA_DK), axis=1)
                put_state(sq, hh, sts[hh] * ecol + upd[hh])
        for hh in heads:
            o = os_[hh] + (inter[hh][0] if n_seg == 1 else jnp.concatenate(inter[hh], axis=0))
            o = o * lax.rsqrt(jnp.mean(o * o, axis=-1, keepdims=True) + NORM_EPS) * ng
            kr = slice(2 * GLA_KW + GLA_VW + hh * GLA_DV, 2 * GLA_KW + GLA_VW + (hh + 1) * GLA_DV)
            og_scr[rows, hh * GLA_DV:(hh + 1) * GLA_DV] = o * _silu(h_ref[rows, kr])

    def trip(n, carry):
        parts = [local(n * cpi + i) for i in range(cpi)]
        for i in range(cpi):
            state_pass(n * cpi + i, *parts[i])
        return carry

    lax.fori_loop(0, tc // (c * cpi), trip, 0)

    if geo.long:
        _out_ln_body(og_scr, x_ref, wout_ref, lng_ref, lnb_ref, og_ref)

        @pl.when(geo.last_time_block())
        def _():
            sout_ref[out_layer, 0] = s_scr[...]
            zero_other_layers()


def _gla_recur(h, la, norm_g, state_in, layer, n_layers, prev_out, geo, out_ln):
    tail = (GLA_HEADS, GLA_DK, GLA_DV)
    ins = [h, la, norm_g]
    in_specs = [pl.BlockSpec((geo.tc, GLA_MAIN), geo.row_map), pl.BlockSpec((geo.tc, GLA_KW), geo.row_map),
                _const_spec(norm_g)]
    if geo.long:
        ins += list(out_ln)
        in_specs += [pl.BlockSpec((geo.tc, D_MODEL), geo.row_map)] + [_const_spec(a) for a in out_ln[1:]]
    else:
        ins.append(state_in)
        in_specs.append(geo.state_spec(layer, tail))
    aliases = {}
    if prev_out is not None:
        aliases = {len(ins): 1}
        ins.append(prev_out)
        in_specs.append(pl.BlockSpec(memory_space=pl.ANY))
    scratch = [pltpu.VMEM((geo.tc, GLA_VW), F32), pltpu.VMEM(tail, F32)] if geo.long else []
    out_layers = n_layers if prev_out is None else 1
    og, s_out = pl.pallas_call(
        functools.partial(_gla_recur_body, geo=geo, has_alias=prev_out is not None,
                          layer=layer, out_layers=out_layers),
        grid=geo.grid, in_specs=in_specs,
        out_specs=[pl.BlockSpec((geo.tc, GLA_VW), geo.row_map), geo.state_spec(layer, tail, out_layers)],
        out_shape=[jax.ShapeDtypeStruct((geo.n_b * geo.n_l, GLA_VW), F32),
                   jax.ShapeDtypeStruct((n_layers, geo.n_b) + tail, F32)],
        scratch_shapes=scratch, input_output_aliases=aliases,
        compiler_params=_params(geo.sem), name="gla_recur",
    )(*ins)
    return og, s_out


def _block_rows(x, mask):
    return jnp.where(mask, jnp.concatenate([x, x], axis=0), 0.0)


def _mm3_split(lhs, rhs_hi, rhs_lo):
    lh, ll = _split2_bf(lhs)
    return _dot(lh, rhs_hi) + (_dot(lh, rhs_lo) + _dot(ll, rhs_hi))


def _mm3_pair(lhs, x_pair, mask):
    xh, xl = _split2_bf(x_pair)
    return _mm3_split(lhs, _bf(_block_rows(xh.astype(F32), mask)), _bf(_block_rows(xl.astype(F32), mask)))


def _gdn_expand_matrix():
    h = jnp.arange(LANES)[:, None]
    return (jnp.arange(GDN_HEADS * CHUNK)[None, :] // CHUNK == h).astype(BF16)


def _gdn_recur_body(*refs, geo):
    refs = list(refs)
    h_ref, ab_ref, wc_ref, alog_ref, dtb_ref, ng_ref, eall_ref = refs[:7]
    pos = 7
    conv0_ref = s0_ref = x_ref = wout_ref = lng_ref = lnb_ref = None
    if geo.long:
        x_ref, wout_ref, lng_ref, lnb_ref = refs[pos:pos + 4]
        pos += 4
    else:
        conv0_ref, s0_ref = refs[pos:pos + 2]
        pos += 2
    og_ref, convn_ref, sout_ref = refs[pos:pos + 3]
    pos += 3
    qkv_scr, g_scr, beta_scr, u_scr, w_scr, att_scr, egc_scr, kend_scr, qk_scr = refs[pos:pos + 9]
    pos += 9
    carry_scr = s_scr = None
    og_scr = og_ref
    if geo.long:
        carry_scr, og_scr, s_scr = refs[pos:pos + 3]
    c, seg, tc = CHUNK, geo.seg, geo.tc
    n_seg = c // seg
    get_state, put_state = _state_access(geo, s0_ref, sout_ref, s_scr)
    pad = SUBLANES
    lo = pad - (GDN_CONV - 1)
    n_pair = GDN_HEADS // 2
    pw = 2 * c
    hw = GDN_HEADS * c
    cpi = 2 if (tc // c) % 2 == 0 else 1

    if geo.long:
        @pl.when(pl.program_id(1) == 0)
        def _():
            carry_scr[...] = jnp.zeros_like(carry_scr)
            s_scr[...] = jnp.zeros_like(s_scr)

    def window_conv(e, cols, t_in=None):
        conv = e[pad:pad + c] * wc_ref[GDN_CONV - 1:GDN_CONV, cols]
        for j in range(GDN_CONV - 1):
            tap = e[lo + j:lo + j + c]
            if t_in is not None:
                tap = jnp.where(t_in >= GDN_CONV - 1 - j, tap, 0.0)
            conv = conv + tap * wc_ref[j:j + 1, cols]
        return conv

    def conv_chunk(m):
        static = isinstance(m, int)
        r0 = m * c if static else pl.multiple_of(m * c, c)
        for cb in range(GDN_CONV_CH // LANES):
            cols = slice(cb * LANES, (cb + 1) * LANES)
            if static and m == 0:
                e = jnp.concatenate([carry_scr[:, cols], h_ref[0:c, cols]], axis=0)
            elif static:
                e = h_ref[r0 - pad:r0 + c, cols]
            else:
                e = h_ref[pl.ds(pl.multiple_of(m * c - pad, SUBLANES), c + pad), cols]
            qkv_scr[pl.ds(r0, c), cols] = _silu(window_conv(e, cols))

    if geo.long:
        for i in range(cpi):
            conv_chunk(i)

        @pl.when(geo.last_time_block())
        def _():
            convn_ref[0, 0] = h_ref[tc - (GDN_CONV - 1):tc, 0:GDN_CONV_CH]
    else:
        t_in = lax.broadcasted_iota(jnp.int32, (c, LANES), 0) & (seg - 1)
        for cb in range(GDN_CONV_CH // LANES):
            cols = slice(cb * LANES, (cb + 1) * LANES)
            e = jnp.concatenate([jnp.zeros((pad, LANES), F32), h_ref[:, cols]], axis=0)
            qkv_scr[:, cols] = window_conv(e, cols, t_in)
        for sq in range(geo.seqs):
            c0 = conv0_ref[0, sq]
            head = []
            for t in range(GDN_CONV - 1):
                acc = c0[t:t + 1, :] * wc_ref[0:1, :]
                for j in range(1, GDN_CONV - 1 - t):
                    acc = acc + c0[t + j:t + j + 1, :] * wc_ref[j:j + 1, :]
                head.append(acc)
            head.append(jnp.zeros((SUBLANES - (GDN_CONV - 1), GDN_CONV_CH), F32))
            first = slice(sq * seg, sq * seg + SUBLANES)
            qkv_scr[first, :] = qkv_scr[first, :] + jnp.concatenate(head, axis=0)
            convn_ref[0, sq] = h_ref[(sq + 1) * seg - (GDN_CONV - 1):(sq + 1) * seg, 0:GDN_CONV_CH]
        qkv_scr[...] = _silu(qkv_scr[...])

    g_scr[...] = -jnp.exp(alog_ref[...]) * _softplus(ab_ref[:, 0:LANES] + dtb_ref[...])
    beta_scr[...] = _sigmoid(ab_ref[:, LANES:2 * LANES])

    incl, strict, delta = _seg_masks((c, hw), seg)
    eye_pair = jnp.where(delta[:, 0:pw], 1.0, 0.0)
    r2 = lax.broadcasted_iota(jnp.int32, (2 * c, pw), 0) >> 6
    mask_cc = r2 == (lax.broadcasted_iota(jnp.int32, (2 * c, pw), 1) >> 6)
    r3 = lax.broadcasted_iota(jnp.int32, (2 * c, 2 * GDN_DK), 0) >> 6
    mask_cd = r3 == (lax.broadcasted_iota(jnp.int32, (2 * c, 2 * GDN_DK), 1) >> 7)
    ones_c = jnp.ones((c, c), BF16)
    ng = ng_ref[...]

    def series_inverse(a_list, eye, n, nil, mask, tick):
        ps = [eye - a for a in a_list]
        xs = [_mm3_pair(a, a, mask) for a in a_list]
        tick()
        steps = int(math.log2(nil)) - 1
        for s in range(steps):
            if s < steps - 1:
                rs = [_mm3_pair(jnp.concatenate([pp, x], axis=0), x, mask) for pp, x in zip(ps, xs)]
                ps = [pp + r[0:n] for pp, r in zip(ps, rs)]
                xs = [r[n:2 * n] for r in rs]
            else:
                ps = [pp + _mm3_pair(pp, x, mask) for pp, x in zip(ps, xs)]
            tick()
        return ps

    def gate_forms(rows):
        gc = _cumsum_rows(g_scr[rows, :], seg)
        e_all = eall_ref[...]
        sh, sm, sl = _split3_bf(jnp.concatenate([gc, beta_scr[rows, :]], axis=0))
        ex = _dot(sh, e_all) + (_dot(sm, e_all) + _dot(sl, e_all))
        gcol, bcol = ex[0:c], ex[c:2 * c]
        gwide = jnp.concatenate([jnp.broadcast_to(gc[:, hh:hh + 1], (c, GDN_DK)) for hh in range(GDN_HEADS)],
                                axis=1)
        th, tm, tl = _split3_bf(jnp.concatenate(
            [jnp.where(delta, gcol, 0.0), jnp.where(delta, bcol, 0.0)], axis=1))
        rowf = _dot(ones_c, th) + (_dot(ones_c, tm) + _dot(ones_c, tl))
        grow, brow = rowf[:, 0:hw], rowf[:, hw:]
        dec = jnp.where(incl, jnp.exp(jnp.where(incl, gcol - grow, 0.0)), 0.0)
        egc_scr[rows, :] = jnp.exp(gwide)
        eend = jnp.exp(_seg_last_rows(gwide, seg) - gwide)
        return dec, bcol, brow, jnp.exp(grow), eend

    def local(n, side=None):
        tick = (lambda: None) if side is None else (lambda: next(side, None))
        chunk_rows = [pl.ds(_aligned((n * cpi + i) * c, c), c) for i in range(cpi)]
        forms = [gate_forms(rows) for rows in chunk_rows]
        units = [(i, p) for i in range(cpi) for p in range(n_pair)]
        tick()

        a_pairs = []
        for i, p in units:
            rows = chunk_rows[i]
            dec, bcol, _, _, eend = forms[i]
            qs, ks = [], []
            for hh in (2 * p, 2 * p + 1):
                kq = slice(hh * GDN_DK, (hh + 1) * GDN_DK)
                kk = slice(GDN_KW + hh * GDN_DK, GDN_KW + (hh + 1) * GDN_DK)
                q = qkv_scr[rows, kq]
                k = qkv_scr[rows, kk]
                q = q * lax.rsqrt(jnp.sum(q * q, axis=-1, keepdims=True) + NORM_EPS) * GDN_DK ** -0.5
                k = k * lax.rsqrt(jnp.sum(k * k, axis=-1, keepdims=True) + NORM_EPS)
                qk_scr[rows, kq] = q
                qk_scr[rows, kk] = k
                kend_scr[rows, kq] = k * eend[:, kq]
                qs.append(q)
                ks.append(k)
            kh = _bf(jnp.concatenate(ks, axis=1))
            kbd_h = _bf(_block_rows(kh.astype(F32), mask_cd))
            kkqk = _dot_nt(jnp.concatenate([kh, _bf(jnp.concatenate(qs, axis=1))], axis=0), kbd_h)
            cs = slice(p * pw, (p + 1) * pw)
            a_pairs.append(jnp.where(strict[:, cs], kkqk[0:c] * dec[:, cs] * bcol[:, cs], 0.0))
            att_scr[rows, cs] = kkqk[c:2 * c] * dec[:, cs]
        tick()
        ps = series_inverse(a_pairs, eye_pair, c, seg, mask_cc, tick)
        for t_inv, (i, p) in zip(ps, units):
            rows = chunk_rows[i]
            _, _, brow, egrow, _ = forms[i]
            cs = slice(p * pw, (p + 1) * pw)
            vs = slice(p * 2 * GDN_DV, (p + 1) * 2 * GDN_DV)
            t_b = t_inv * brow[:, cs]
            vh, vl = _split2_bf(qkv_scr[rows, 2 * GDN_KW + p * 2 * GDN_DV:2 * GDN_KW + (p + 1) * 2 * GDN_DV])
            u_scr[rows, vs] = _mm3_split(t_b, _bf(_block_rows(vh.astype(F32), mask_cd)),
                                         _bf(_block_rows(vl.astype(F32), mask_cd)))
            kh, kl = _split2_bf(qk_scr[rows, GDN_KW + p * 2 * GDN_DK:GDN_KW + (p + 1) * 2 * GDN_DK])
            w_scr[rows, vs] = _mm3_split(t_b * egrow[:, cs], _bf(_block_rows(kh.astype(F32), mask_cd)),
                                         _bf(_block_rows(kl.astype(F32), mask_cd)))
        if geo.long:
            for i in range(cpi):
                nxt, cap = (n + 1) * cpi + i, tc // c - cpi + i
                conv_chunk(min(nxt, cap) if isinstance(n, int) else jnp.minimum(nxt, cap))
        if side is not None:
            for _ in side:
                pass

    def recur(n):
        r0 = _aligned(n * c, c)
        rows = pl.ds(r0, c)
        heads = range(GDN_HEADS)
        kqs = [slice(hh * GDN_DK, (hh + 1) * GDN_DK) for hh in heads]
        vn_seg = [[] for _ in heads]
        qs_seg = [[] for _ in heads]
        for s in range(n_seg):
            sq = n * n_seg + s
            sr = pl.ds(_aligned(r0 + s * seg, SUBLANES), seg)
            tail = pl.ds(_aligned(r0 + (s + 1) * seg - SUBLANES, SUBLANES), SUBLANES)
            sts = [get_state(sq, hh) for hh in heads]
            m1s = [_dot(_bf(jnp.concatenate([w_scr[sr, kqs[hh]], qk_scr[sr, kqs[hh]]], axis=0)), _bf(sts[hh]))
                   for hh in heads]
            yield
            vn = [u_scr[sr, kqs[hh]] - m1s[hh][0:seg] for hh in heads]
            upd = [_dot_tn(_bf(kend_scr[sr, kqs[hh]]), _bf(vn[hh])) for hh in heads]
            yield
            for hh in heads:
                e_last = egc_scr[tail, kqs[hh]][SUBLANES - 1:SUBLANES, :]
                put_state(sq, hh, sts[hh] * e_last + upd[hh])
                vn_seg[hh].append(vn[hh])
                qs_seg[hh].append(m1s[hh][seg:2 * seg])
        join = lambda parts: parts[0] if n_seg == 1 else jnp.concatenate(parts, axis=0)
        o_pairs = []
        for p in range(n_pair):
            pair = (2 * p, 2 * p + 1)
            vs = slice(p * 2 * GDN_DV, (p + 1) * 2 * GDN_DV)
            vbd = _bf(_block_rows(jnp.concatenate([join(vn_seg[hh]) for hh in pair], axis=1), mask_cd))
            o_pairs.append(_dot(_bf(att_scr[rows, p * pw:(p + 1) * pw]), vbd)
                           + egc_scr[rows, vs] * jnp.concatenate([join(qs_seg[hh]) for hh in pair], axis=1))
        yield
        for p in range(n_pair):
            for i, hh in enumerate((2 * p, 2 * p + 1)):
                o = o_pairs[p][:, i * GDN_DV:(i + 1) * GDN_DV]
                o = o * lax.rsqrt(jnp.mean(o * o, axis=-1, keepdims=True) + NORM_EPS) * ng
                kz = slice(GDN_CONV_CH + hh * GDN_DV, GDN_CONV_CH + (hh + 1) * GDN_DV)
                og_scr[rows, hh * GDN_DV:(hh + 1) * GDN_DV] = o * _silu(h_ref[rows, kz])
        yield

    def recur_trip(n):
        for i in range(cpi):
            yield from recur(n * cpi + i)

    def drain(gen):
        for _ in gen:
            pass

    n_trip = tc // (c * cpi)
    local(0)

    def trip(n, carry):
        local(n, recur_trip(n - 1))
        return carry

    lax.fori_loop(1, n_trip, trip, 0)
    drain(recur_trip(n_trip - 1))

    if geo.long:
        carry_scr[...] = h_ref[tc - pad:tc, 0:GDN_CONV_CH]
        _out_ln_body(og_scr, x_ref, wout_ref, lng_ref, lnb_ref, og_ref)

        @pl.when(geo.last_time_block())
        def _():
            sout_ref[0, 0] = s_scr[...]


def _gdn_recur(h, ab, w_conv, a_log, dt_bias, norm_g, conv_in, state_in, layer, n_layers, geo, out_ln):
    tail = (GDN_HEADS, GDN_DK, GDN_DV)
    cv_tail = (GDN_CONV - 1, GDN_CONV_CH)
    consts = [w_conv, a_log, dt_bias, norm_g, _gdn_expand_matrix()]
    ins = [h, ab] + consts
    in_specs = [pl.BlockSpec((geo.tc, GDN_MAIN), geo.row_map), pl.BlockSpec((geo.tc, 2 * LANES), geo.row_map)]
    in_specs += [_const_spec(a) for a in consts]
    tc = geo.tc
    if geo.long:
        ins += list(out_ln)
        in_specs += [pl.BlockSpec((tc, D_MODEL), geo.row_map)] + [_const_spec(a) for a in out_ln[1:]]
    else:
        assert tc == CHUNK
        ins += [conv_in, state_in]
        in_specs += [geo.state_spec(layer, cv_tail), geo.state_spec(layer, tail)]
    scratch = [pltpu.VMEM((tc, GDN_CONV_CH), F32),
               pltpu.VMEM((tc, LANES), F32), pltpu.VMEM((tc, LANES), F32),
               pltpu.VMEM((tc, GDN_VW), F32), pltpu.VMEM((tc, GDN_KW), F32),
               pltpu.VMEM((tc, GDN_HEADS * CHUNK), F32), pltpu.VMEM((tc, GDN_KW), F32),
               pltpu.VMEM((tc, GDN_KW), F32), pltpu.VMEM((tc, 2 * GDN_KW), F32)]
    if geo.long:
        scratch += [pltpu.VMEM((SUBLANES, GDN_CONV_CH), F32), pltpu.VMEM((tc, GDN_VW), F32),
                    pltpu.VMEM(tail, F32)]
    return pl.pallas_call(
        functools.partial(_gdn_recur_body, geo=geo),
        grid=geo.grid, in_specs=in_specs,
        out_specs=[pl.BlockSpec((tc, GDN_VW), geo.row_map), geo.state_spec(layer, cv_tail),
                   geo.state_spec(layer, tail)],
        out_shape=[jax.ShapeDtypeStruct((geo.n_b * geo.n_l, GDN_VW), F32),
                   jax.ShapeDtypeStruct((n_layers, geo.n_b) + cv_tail, F32),
                   jax.ShapeDtypeStruct((n_layers, geo.n_b) + tail, F32)],
        scratch_shapes=scratch,
        compiler_params=_params(geo.sem), name="gdn_recur",
    )(*ins)


def _s5_body(*refs, tt, nb, has_init):
    if has_init:
        (x_ref, win_ref, wb_ref, are_ref, aim_ref, wc_ref, d_ref, wglu_ref, bglu_ref, wout_ref,
         lng_ref, lnb_ref, h0re_ref, h0im_ref, y_ref, hre_ref, him_ref,
         x_scr, u_scr, z_scr, bu_scr, y_scr, st_scr) = refs
    else:
        (x_ref, win_ref, wb_ref, are_ref, aim_ref, wc_ref, d_ref, wglu_ref, bglu_ref, wout_ref,
         lng_ref, lnb_ref, y_ref, hre_ref, him_ref,
         x_scr, u_scr, z_scr, bu_scr, y_scr, st_scr) = refs
    tb = pl.program_id(1)
    ns = S5_KT_STATES

    @pl.when(tb == 0)
    def _():
        if not has_init:
            st_scr[...] = jnp.zeros_like(st_scr)
        else:
            for kt in range(S5_KT):
                st_scr[:, 2 * kt * ns:(2 * kt + 1) * ns] = h0re_ref[0, :, kt * ns:(kt + 1) * ns]
                st_scr[:, (2 * kt + 1) * ns:(2 * kt + 2) * ns] = h0im_ref[0, :, kt * ns:(kt + 1) * ns]

    x_scr[...] = jnp.swapaxes(x_ref[...], 0, 1).reshape(tt * nb, D_MODEL)
    h = _dot(_bf(x_scr[...]), win_ref[...])
    u_scr[...] = h[:, 0:S5_WIDTH]
    z_scr[...] = h[:, S5_WIDTH:]

    def input_map(kt):
        cols = slice(kt * S5_KT_W, (kt + 1) * S5_KT_W)
        bu_scr[kt % 2] = _dot(_bf(u_scr[:, cols]), wb_ref[kt])

    def scan(kt):
        bu = bu_scr.at[kt % 2]
        a_re = jnp.broadcast_to(are_ref[kt], (SUBLANES, ns))
        a_im = jnp.broadcast_to(aim_ref[kt], (SUBLANES, ns))
        base = kt * 2 * ns
        for rb in range(nb // SUBLANES):
            st_rows = slice(rb * SUBLANES, (rb + 1) * SUBLANES)
            h_re = st_scr[st_rows, base:base + ns]
            h_im = st_scr[st_rows, base + ns:base + 2 * ns]
            for t in range(tt):
                rows = slice(t * nb + rb * SUBLANES, t * nb + (rb + 1) * SUBLANES)
                h_re, h_im = (a_re * h_re - a_im * h_im + bu[rows, 0:ns],
                              a_re * h_im + a_im * h_re + bu[rows, ns:2 * ns])
                bu[rows, 0:ns] = h_re
                bu[rows, ns:2 * ns] = h_im
            st_scr[st_rows, base:base + ns] = h_re
            st_scr[st_rows, base + ns:base + 2 * ns] = h_im

    def output_map(kt):
        cols = slice(kt * S5_KT_W, (kt + 1) * S5_KT_W)
        y_scr[:, cols] = _dot(_bf(bu_scr[kt % 2]), wc_ref[kt]) + d_ref[:, cols] * u_scr[:, cols]

    input_map(0)
    for kt in range(S5_KT):
        if kt + 1 < S5_KT:
            input_map(kt + 1)
        scan(kt)
        output_map(kt)

    th = tt // 2
    halves = [slice(i * th * nb, (i + 1) * th * nb) for i in range(2)]

    def glu(r):
        return _dot(_bf(_gelu_tanh(y_scr[r, :])), wglu_ref[...]) + bglu_ref[...]

    def gated_out(r, yg):
        y = yg[:, 0:S5_WIDTH] * _sigmoid(yg[:, S5_WIDTH:]) * _silu(z_scr[r, :])
        return _dot(_bf(y), wout_ref[...])

    def finish(i, out):
        y = _layer_norm(ALPHA_RES * x_scr[halves[i], :] + out, lng_ref[...], lnb_ref[...])
        y_ref[:, i * th:(i + 1) * th, :] = jnp.swapaxes(y.reshape(th, nb, D_MODEL), 0, 1)

    yg0 = glu(halves[0])
    yg1 = glu(halves[1])
    out0 = gated_out(halves[0], yg0)
    out1 = gated_out(halves[1], yg1)
    finish(0, out0)
    finish(1, out1)

    @pl.when(tb == pl.num_programs(1) - 1)
    def _():
        for kt in range(S5_KT):
            hre_ref[0, :, kt * ns:(kt + 1) * ns] = st_scr[:, 2 * kt * ns:(2 * kt + 1) * ns]
            him_ref[0, :, kt * ns:(kt + 1) * ns] = st_scr[:, (2 * kt + 1) * ns:(2 * kt + 2) * ns]


def _s5_discretize(lam_re, lam_im, log_dt, b_re, b_im, c_re, c_im):
    dt = jnp.exp(log_dt)[:, None]
    mag = jnp.exp(lam_re * dt)
    ab_re, ab_im = mag * jnp.cos(lam_im * dt), mag * jnp.sin(lam_im * dt)
    den = jnp.square(lam_re) + jnp.square(lam_im)
    num_re = ab_re - 1.0
    coef_re = (num_re * lam_re + ab_im * lam_im) / den
    coef_im = (ab_im * lam_re - num_re * lam_im) / den
    bb_re = coef_re[..., None] * b_re - coef_im[..., None] * b_im
    bb_im = coef_re[..., None] * b_im + coef_im[..., None] * b_re
    gl = S5_GROUPS // S5_KT
    rep = (jnp.arange(S5_KT_STATES)[None, :] % S5_STATE == jnp.arange(S5_STATE)[:, None]).astype(F32)
    same = jnp.arange(S5_KT_W)[:, None] // S5_GROUP == jnp.arange(S5_KT_STATES)[None, :] // S5_STATE
    exact = lax.Precision.HIGHEST

    def block_b(bb):
        t = bb.reshape(S5_KT, gl, S5_STATE, S5_GROUP).transpose(0, 1, 3, 2).reshape(S5_KT, S5_KT_W, S5_STATE)
        return jnp.where(same, jnp.einsum("krp,pn->krn", t, rep, precision=exact), 0.0)

    def block_c(cc):
        t = cc.reshape(S5_KT, gl, S5_GROUP, S5_STATE).transpose(0, 3, 1, 2).reshape(S5_KT, S5_STATE, S5_KT_W)
        return jnp.where(same.T, jnp.einsum("pn,kpr->knr", rep, t, precision=exact), 0.0)

    w_b = jnp.concatenate([block_b(bb_re), block_b(bb_im)], axis=2).astype(BF16)
    w_c = jnp.concatenate([block_c(c_re), -block_c(c_im)], axis=1).astype(BF16)
    a_re = ab_re.reshape(S5_KT, 1, S5_KT_STATES)
    a_im = ab_im.reshape(S5_KT, 1, S5_KT_STATES)
    return w_b, w_c, a_re, a_im


def _s5_layer(x, prep, w_in, d_vec, w_glu, b_glu, w_out, ln_g, ln_b, h0, n_b, n_l):
    w_b, w_c, a_re, a_im = prep
    if n_l >= S5_LONG_STEPS:
        nb, tt = n_b, S5_LONG_STEPS
    else:
        nb, tt = min(S5_SHORT_SEQS, n_b), n_l
    assert nb % SUBLANES == 0 and tt % SUBLANES == 0
    rows = tt * nb
    has_init = h0 is not None
    consts = [w_in, w_b, a_re, a_im, w_c, d_vec, w_glu, b_glu, w_out, ln_g, ln_b]
    x_spec = pl.BlockSpec((nb, tt, D_MODEL), lambda b, t: (b, t, 0))
    st_spec = pl.BlockSpec((1, nb, S5_NSTATE), lambda b, t: (0, b, 0))
    ins = [x.reshape(n_b, n_l, D_MODEL)] + consts
    in_specs = [x_spec] + [_const_spec(a) for a in consts]
    if has_init:
        ins += list(h0)
        in_specs += [st_spec, st_spec]
    y, h_re, h_im = pl.pallas_call(
        functools.partial(_s5_body, tt=tt, nb=nb, has_init=has_init),
        grid=(n_b // nb, n_l // tt), in_specs=in_specs,
        out_specs=[x_spec, st_spec, st_spec],
        out_shape=[jax.ShapeDtypeStruct((n_b, n_l, D_MODEL), F32),
                   jax.ShapeDtypeStruct((1, n_b, S5_NSTATE), F32),
                   jax.ShapeDtypeStruct((1, n_b, S5_NSTATE), F32)],
        scratch_shapes=[pltpu.VMEM((rows, D_MODEL), F32),
                        pltpu.VMEM((rows, S5_WIDTH), F32), pltpu.VMEM((rows, S5_WIDTH), F32),
                        pltpu.VMEM((2, rows, 2 * S5_KT_STATES), F32), pltpu.VMEM((rows, S5_WIDTH), F32),
                        pltpu.VMEM((nb, 2 * S5_NSTATE), F32)],
        compiler_params=_params(("parallel", "arbitrary")), name="s5_layer",
    )(*ins)
    return y.reshape(n_b * n_l, D_MODEL), h_re, h_im


def _row2(v):
    return v.reshape(1, -1).astype(F32)


def _pad_cols(w, n):
    return jnp.concatenate([w, jnp.zeros((w.shape[0], n - w.shape[1]), w.dtype)], axis=1)


def _trunk(x, n_b, n_l, states, wts):
    geo = _Geometry(n_b, n_l)
    n_gla, n_gdn, n_s5 = (DEPTH + 2) // 3, (DEPTH + 1) // 3, DEPTH // 3
    assert n_gdn == 1 and n_s5 == 1
    s_gla = s_gdn = s_conv = s_re = s_im = None
    for i in range(DEPTH):
        j, kind = divmod(i, 3)
        ln_g, ln_b = _row2(wts["ln_g"][i]), _row2(wts["ln_b"][i])
        if kind == 0:
            w_in = wts["gla_w_in"][j]
            w_main = _bf(w_in)
            w_lr = _bf(_pad_cols(w_in[:, GLA_MAIN:], LANES))
            w_a2 = _bf(jnp.concatenate(
                [wts["gla_w_a2"][j], jnp.zeros((LANES - GLA_LOWRANK, GLA_KW), F32)], axis=0))
            h, la = _rowwise_call(_gla_proj_body, "gla_proj", [x],
                                  [w_main, w_lr, w_a2, _row2(wts["gla_b_a"][j])], [GLA_MAIN, GLA_KW])
            w_out = _bf(wts["gla_w_out"][j])
            og, s_gla = _gla_recur(h, la, _row2(wts["gla_norm_g"][j]),
                                   None if states is None else states[0], j, n_gla, s_gla, geo,
                                   (x, w_out, ln_g, ln_b))
        elif kind == 1:
            w_in = wts["gdn_w_in"][j]
            w_main = _bf(w_in)
            w_ab = _bf(jnp.concatenate(
                [_pad_cols(w_in[:, GDN_MAIN:GDN_MAIN + GDN_HEADS], LANES),
                 _pad_cols(w_in[:, GDN_MAIN + GDN_HEADS:], LANES)], axis=1))
            h, ab = _rowwise_call(_gdn_proj_body, "gdn_proj", [x], [w_main, w_ab], [GDN_MAIN, 2 * LANES])
            w_out = _bf(wts["gdn_w_out"][j])
            og, s_conv, s_gdn = _gdn_recur(
                h, ab, wts["gdn_w_conv"][j].astype(F32),
                _pad_cols(_row2(wts["gdn_a_log"][j]), LANES), _pad_cols(_row2(wts["gdn_dt_bias"][j]), LANES),
                _row2(wts["gdn_norm_g"][j]),
                None if states is None else states[2], None if states is None else states[1],
                j, n_gdn, geo, (x, w_out, ln_g, ln_b))
        else:
            prep = _s5_discretize(wts["s5_lam_re"][j].astype(F32), wts["s5_lam_im"][j].astype(F32),
                                  wts["s5_log_dt"][j].astype(F32), wts["s5_b_re"][j].astype(F32),
                                  wts["s5_b_im"][j].astype(F32), wts["s5_c_re"][j].astype(F32),
                                  wts["s5_c_im"][j].astype(F32))
            h0 = None if states is None else (states[3].reshape(n_s5, n_b, S5_NSTATE),
                                              states[4].reshape(n_s5, n_b, S5_NSTATE))
            x, h_re, h_im = _s5_layer(x, prep, _bf(wts["s5_w_in"][j]), _row2(wts["s5_d"][j]),
                                      _bf(wts["s5_w_glu"][j]), _row2(wts["s5_b_glu"][j]),
                                      _bf(wts["s5_w_out"][j]), ln_g, ln_b, h0, n_b, n_l)
            s_re = h_re.reshape(n_s5, n_b, S5_GROUPS, S5_STATE)
            s_im = h_im.reshape(n_s5, n_b, S5_GROUPS, S5_STATE)
            continue
        if geo.long:
            x = og
        else:
            (x,) = _rowwise_call(_out_ln_body, "out_ln", [og, x], [w_out, ln_g, ln_b], [D_MODEL])
    return x, s_gla, s_gdn, s_conv, s_re, s_im


def kernel(x_prompt, x_sample, state_gla, state_gdn, state_gdn_conv, state_s5_re, state_s5_im,
           ln_g, ln_b, gla_w_in, gla_w_a2, gla_b_a, gla_norm_g, gla_w_out,
           gdn_w_in, gdn_w_conv, gdn_a_log, gdn_dt_bias, gdn_norm_g, gdn_w_out,
           s5_w_in, s5_lam_re, s5_lam_im, s5_log_dt, s5_b_re, s5_b_im, s5_c_re, s5_c_im,
           s5_d, s5_w_glu, s5_b_glu, s5_w_out):
    wts = dict(ln_g=ln_g, ln_b=ln_b,
               gla_w_in=gla_w_in, gla_w_a2=gla_w_a2, gla_b_a=gla_b_a, gla_norm_g=gla_norm_g,
               gla_w_out=gla_w_out,
               gdn_w_in=gdn_w_in, gdn_w_conv=gdn_w_conv, gdn_a_log=gdn_a_log, gdn_dt_bias=gdn_dt_bias,
               gdn_norm_g=gdn_norm_g, gdn_w_out=gdn_w_out,
               s5_w_in=s5_w_in, s5_lam_re=s5_lam_re, s5_lam_im=s5_lam_im, s5_log_dt=s5_log_dt,
               s5_b_re=s5_b_re, s5_b_im=s5_b_im, s5_c_re=s5_c_re, s5_c_im=s5_c_im, s5_d=s5_d,
               s5_w_glu=s5_w_glu, s5_b_glu=s5_b_glu, s5_w_out=s5_w_out)

    bp, lp, _ = x_prompt.shape
    yp, p_gla, p_gdn, p_conv, p_re, p_im = _trunk(
        x_prompt.reshape(bp * lp, D_MODEL), bp, lp, None, wts)
    bs, ls, _ = x_sample.shape
    ys, s_gla, s_gdn, s_conv, s_re, s_im = _trunk(
        x_sample.reshape(bs * ls, D_MODEL), bs, ls,
        (state_gla, state_gdn, state_gdn_conv, state_s5_re, state_s5_im), wts)
    return (yp.reshape(bp, lp, D_MODEL), ys.reshape(bs, ls, D_MODEL),
            p_gla, p_gdn, p_conv, p_re, p_im, s_gla, s_gdn, s_conv, s_re, s_im)
```

```python
import functools
import math

import jax
import jax.numpy as jnp
from jax import lax
from jax.experimental import pallas as pl
from jax.experimental.pallas import tpu as pltpu

F32 = jnp.float32
BF16 = jnp.bfloat16

D_MODEL = 1024
DEPTH = 4
ALPHA_RES = (2 * DEPTH) ** 0.25
LN_EPS = 1e-5
NORM_EPS = 1e-6
CHUNK = 64

GLA_HEADS = 4
GLA_KW = 512
GLA_VW = 1024
GLA_DK = 128
GLA_DV = 256
GLA_LOWRANK = 16
GLA_TAU = 16.0
GLA_MAIN = 2 * GLA_KW + 2 * GLA_VW

GDN_HEADS = 8
GDN_DK = 128
GDN_DV = 128
GDN_KW = 1024
GDN_VW = 1024
GDN_CONV = 4
GDN_CONV_CH = 3072
GDN_MAIN = GDN_CONV_CH + GDN_VW

S5_WIDTH = 1024
S5_GROUP = 16
S5_GROUPS = 64
S5_STATE = 64
S5_KT = 4
S5_KT_W = S5_WIDTH // S5_KT
S5_KT_STATES = (S5_GROUPS // S5_KT) * S5_STATE
S5_NSTATE = S5_GROUPS * S5_STATE

LANES = 128
SUBLANES = 8
VMEM_LIMIT = 56 * 1024 * 1024
ROW_BLOCK = 512
LONG_TIME_BLOCK = 512
S5_LONG_STEPS = 64
S5_SHORT_SEQS = 64


def _bf(x):
    return x.astype(BF16)


def _dot(a, b):
    return jnp.dot(a, b, preferred_element_type=F32)


def _dot_nt(a, b):
    return lax.dot_general(a, b, (((1,), (1,)), ((), ())), preferred_element_type=F32)


def _dot_tn(a, b):
    return lax.dot_general(a, b, (((0,), (0,)), ((), ())), preferred_element_type=F32)


def _split2_bf(x):
    h = x.astype(BF16)
    return h, (x - h.astype(F32)).astype(BF16)


def _split3_bf(x):
    h = x.astype(BF16)
    r = x - h.astype(F32)
    m = r.astype(BF16)
    return h, m, (r - m.astype(F32)).astype(BF16)


def _sigmoid(x):
    return 1.0 / (1.0 + jnp.exp(-x))


def _silu(x):
    return x * _sigmoid(x)


def _softplus(x):
    return jnp.maximum(x, 0.0) + jnp.log(1.0 + jnp.exp(-jnp.abs(x)))


def _gelu_tanh(x):
    return 0.5 * x * (1.0 + jnp.tanh(math.sqrt(2.0 / math.pi) * (x + 0.044715 * (x * x * x))))


def _layer_norm(x, g, b):
    mu = jnp.mean(x, axis=-1, keepdims=True)
    xc = x - mu
    var = jnp.mean(xc * xc, axis=-1, keepdims=True)
    return xc * lax.rsqrt(var + LN_EPS) * g + b


def _cumsum_rows(x, seg):
    row = lax.broadcasted_iota(jnp.int32, x.shape, 0) & (seg - 1)
    s = 1
    while s < seg:
        x = x + jnp.where(row >= s, pltpu.roll(x, s, axis=0), 0.0)
        s *= 2
    return x


def _seg_last_rows(x, seg):
    c, w = x.shape
    if seg == c:
        return jnp.broadcast_to(x[c - 1:c, :], (c, w))
    x3 = x.reshape(c // seg, seg, w)
    return jnp.broadcast_to(x3[:, seg - 1:seg, :], x3.shape).reshape(c, w)


def _seg_masks(shape, seg):
    ri = lax.broadcasted_iota(jnp.int32, shape, 0)
    li = lax.broadcasted_iota(jnp.int32, shape, 1) & (CHUNK - 1)
    shift = int(math.log2(seg))
    same = (ri >> shift) == (li >> shift)
    return same & (ri >= li), same & (ri > li), ri == li


def _aligned(v, m):
    return v if isinstance(v, int) else pl.multiple_of(v, m)


def _const_spec(arr):
    nd = arr.ndim
    return pl.BlockSpec(arr.shape, lambda *_: (0,) * nd, pipeline_mode=pl.Buffered(1))


def _params(sem):
    return pltpu.CompilerParams(dimension_semantics=sem, vmem_limit_bytes=VMEM_LIMIT)


def _rowwise_call(body, name, rows, consts, out_widths):
    n = rows[0].shape[0]
    tm = min(ROW_BLOCK, n)
    spec = lambda w: pl.BlockSpec((tm, w), lambda i: (i, 0))
    return pl.pallas_call(
        body, grid=(n // tm,),
        in_specs=[spec(r.shape[1]) for r in rows] + [_const_spec(c) for c in consts],
        out_specs=[spec(w) for w in out_widths],
        out_shape=[jax.ShapeDtypeStruct((n, w), F32) for w in out_widths],
        compiler_params=_params(("parallel",)), name=name,
    )(*rows, *consts)


def _gla_proj_body(x_ref, w_ref, wlr_ref, wa2_ref, ba_ref, h_ref, la_ref):
    xb = _bf(x_ref[...])
    lr = _dot(xb, wlr_ref[...])
    pre = _dot(_bf(lr), wa2_ref[...]) + ba_ref[...]
    la_ref[...] = -_softplus(-pre) * (1.0 / GLA_TAU)
    h = _dot(xb, w_ref[:, 0:GLA_MAIN])
    h_ref[:, :GLA_KW] = h[:, :GLA_KW] * GLA_DK ** -0.5
    h_ref[:, GLA_KW:] = h[:, GLA_KW:]


def _gdn_proj_body(x_ref, w_ref, wab_ref, h_ref, ab_ref):
    xb = _bf(x_ref[...])
    h_ref[...] = _dot(xb, w_ref[:, 0:GDN_MAIN])
    ab_ref[...] = _dot(xb, wab_ref[...])


def _out_ln_body(og_ref, x_ref, w_ref, g_ref, b_ref, y_ref):
    n = og_ref.shape[0]
    halves = [slice(0, n // 2), slice(n // 2, n)]
    outs = [_dot(_bf(og_ref[r, :]), w_ref[...]) for r in halves]
    for r, out in zip(halves, outs):
        y_ref[r, :] = _layer_norm(ALPHA_RES * x_ref[r, :] + out, g_ref[...], b_ref[...])


class _Geometry:
    def __init__(self, n_b, n_l):
        self.n_b, self.n_l = n_b, n_l
        self.long = n_l >= CHUNK
        if self.long:
            assert n_l % LONG_TIME_BLOCK == 0
            self.seg = CHUNK
            self.tc = LONG_TIME_BLOCK
            self.n_t = n_l // self.tc
            self.grid = (n_b, self.n_t)
            self.sem = ("parallel", "arbitrary")
            self.seqs = 1
            self.row_map = lambda b, t: (b * self.n_t + t, 0)
            self.seq_block = lambda b, t: b
        else:
            assert CHUNK % n_l == 0 and n_l % SUBLANES == 0 and (n_b * n_l) % CHUNK == 0
            self.seg = n_l
            self.tc = CHUNK
            self.grid = (n_b * n_l // self.tc,)
            self.sem = ("parallel",)
            self.seqs = self.tc // n_l
            self.row_map = lambda i: (i, 0)
            self.seq_block = lambda i: i

    def state_spec(self, layer, tail, n_layers=1):
        zeros = (0,) * len(tail)
        first = layer if n_layers == 1 else 0
        return pl.BlockSpec((n_layers, self.seqs) + tail, lambda *g: (first, self.seq_block(*g)) + zeros)

    def last_time_block(self):
        return pl.program_id(1) == pl.num_programs(1) - 1 if self.long else None


def _state_access(geo, s0_ref, sout_ref, s_scr, out_layer=0):
    if geo.long:
        return (lambda sq, hh: s_scr[hh]), (lambda sq, hh, v: s_scr.__setitem__(hh, v))
    return ((lambda sq, hh: s0_ref[0, sq, hh]),
            (lambda sq, hh, v: sout_ref.__setitem__((out_layer, sq, hh), v)))


def _gla_recur_body(*refs, geo, has_alias, layer, out_layers):
    refs = list(refs)
    h_ref, la_ref, ng_ref = refs[:3]
    pos = 3
    s0_ref = x_ref = wout_ref = lng_ref = lnb_ref = None
    if geo.long:
        x_ref, wout_ref, lng_ref, lnb_ref = refs[pos:pos + 4]
        pos += 4
    else:
        s0_ref = refs[pos]
        pos += 1
    if has_alias:
        pos += 1
    og_ref, sout_ref = refs[pos:pos + 2]
    og_scr, s_scr = refs[pos + 2:pos + 4] if geo.long else (og_ref, None)
    c, seg, tc = CHUNK, geo.seg, geo.tc
    n_seg = c // seg
    out_layer = layer if out_layers > 1 else 0
    get_state, put_state = _state_access(geo, s0_ref, sout_ref, s_scr, out_layer)

    def zero_other_layers():
        for other in range(out_layers):
            if other != out_layer:
                sout_ref[other] = jnp.zeros(sout_ref.shape[1:], F32)

    if geo.long:
        @pl.when(pl.program_id(1) == 0)
        def _():
            s_scr[...] = jnp.zeros_like(s_scr)
    else:
        zero_other_layers()

    incl, _, _ = _seg_masks((c, c), seg)
    ng = ng_ref[...]

    heads = range(GLA_HEADS)
    kq = [slice(hh * GLA_DK, (hh + 1) * GLA_DK) for hh in heads]
    cpi = 2 if (tc // c) % 2 == 0 else 1

    def local(n):
        rows = pl.ds(_aligned(n * c, c), c)
        b = _cumsum_rows(la_ref[rows, :], seg)
        bl = _seg_last_rows(b, seg)
        k_all = h_ref[rows, GLA_KW:2 * GLA_KW]
        q_dec = h_ref[rows, 0:GLA_KW] * jnp.exp(b)
        k_inv = k_all * jnp.exp(-b)
        k_end = k_all * jnp.exp(bl - b)
        e_bl = jnp.exp(bl)
        vs = [h_ref[rows, 2 * GLA_KW + hh * GLA_DV:2 * GLA_KW + (hh + 1) * GLA_DV] for hh in heads]
        atts = [jnp.where(incl, _dot_nt(_bf(q_dec[:, kq[hh]]), _bf(k_inv[:, kq[hh]])), 0.0) for hh in heads]
        os_ = [_dot(_bf(atts[hh]), _bf(vs[hh])) for hh in heads]
        return rows, q_dec, k_end, e_bl, vs, os_

    def state_pass(n, rows, q_dec, k_end, e_bl, vs, os_):
        inter = [[] for _ in heads]
        for s in range(n_seg):
            sq = n * n_seg + s
            sr = slice(s * seg, (s + 1) * seg)
            sts = [get_state(sq, hh) for hh in heads]
            for hh in heads:
                inter[hh].append(_dot(_bf(q_dec[sr, kq[hh]]), _bf(sts[hh])))
            upd = [_dot_tn(_bf(k_end[sr, kq[hh]]), _bf(vs[hh][sr])) for hh in heads]
            for hh in heads:
                ecol = jnp.transpose(jnp.broadcast_to(e_bl[s * seg:s * seg + 1, kq[hh]], (GLA_DK, GLA_DK)))
                ecol = jnp.concatenate([ecol] * (GLA_DV // GLA_DK), axis=1)
                put_state(sq, hh, sts[hh] * ecol + upd[hh])
        for hh in heads:
            o = os_[hh] + (inter[hh][0] if n_seg == 1 else jnp.concatenate(inter[hh], axis=0))
            o = o * lax.rsqrt(jnp.mean(o * o, axis=-1, keepdims=True) + NORM_EPS) * ng
            kr = slice(2 * GLA_KW + GLA_VW + hh * GLA_DV, 2 * GLA_KW + GLA_VW + (hh + 1) * GLA_DV)
            og_scr[rows, hh * GLA_DV:(hh + 1) * GLA_DV] = o * _silu(h_ref[rows, kr])

    def trip(n, carry):
        parts = [local(n * cpi + i) for i in range(cpi)]
        for i in range(cpi):
            state_pass(n * cpi + i, *parts[i])
        return carry

    lax.fori_loop(0, tc // (c * cpi), trip, 0)

    if geo.long:
        _out_ln_body(og_scr, x_ref, wout_ref, lng_ref, lnb_ref, og_ref)

        @pl.when(geo.last_time_block())
        def _():
            sout_ref[out_layer, 0] = s_scr[...]
            zero_other_layers()


def _gla_recur(h, la, norm_g, state_in, layer, n_layers, prev_out, geo, out_ln):
    tail = (GLA_HEADS, GLA_DK, GLA_DV)
    ins = [h, la, norm_g]
    in_specs = [pl.BlockSpec((geo.tc, GLA_MAIN), geo.row_map), pl.BlockSpec((geo.tc, GLA_KW), geo.row_map),
                _const_spec(norm_g)]
    if geo.long:
        ins += list(out_ln)
        in_specs += [pl.BlockSpec((geo.tc, D_MODEL), geo.row_map)] + [_const_spec(a) for a in out_ln[1:]]
    else:
        ins.append(state_in)
        in_specs.append(geo.state_spec(layer, tail))
    aliases = {}
    if prev_out is not None:
        aliases = {len(ins): 1}
        ins.append(prev_out)
        in_specs.append(pl.BlockSpec(memory_space=pl.ANY))
    scratch = [pltpu.VMEM((geo.tc, GLA_VW), F32), pltpu.VMEM(tail, F32)] if geo.long else []
    out_layers = n_layers if prev_out is None else 1
    og, s_out = pl.pallas_call(
        functools.partial(_gla_recur_body, geo=geo, has_alias=prev_out is not None,
                          layer=layer, out_layers=out_layers),
        grid=geo.grid, in_specs=in_specs,
        out_specs=[pl.BlockSpec((geo.tc, GLA_VW), geo.row_map), geo.state_spec(layer, tail, out_layers)],
        out_shape=[jax.ShapeDtypeStruct((geo.n_b * geo.n_l, GLA_VW), F32),
                   jax.ShapeDtypeStruct((n_layers, geo.n_b) + tail, F32)],
        scratch_shapes=scratch, input_output_aliases=aliases,
        compiler_params=_params(geo.sem), name="gla_recur",
    )(*ins)
    return og, s_out


def _block_rows(x, mask):
    return jnp.where(mask, jnp.concatenate([x, x], axis=0), 0.0)


def _mm3_split(lhs, rhs_hi, rhs_lo):
    lh, ll = _split2_bf(lhs)
    return _dot(lh, rhs_hi) + (_dot(lh, rhs_lo) + _dot(ll, rhs_hi))


def _mm3_pair(lhs, x_pair, mask):
    xh, xl = _split2_bf(x_pair)
    return _mm3_split(lhs, _bf(_block_rows(xh.astype(F32), mask)), _bf(_block_rows(xl.astype(F32), mask)))


def _gdn_expand_matrix():
    h = jnp.arange(LANES)[:, None]
    return (jnp.arange(GDN_HEADS * CHUNK)[None, :] // CHUNK == h).astype(BF16)


def _gdn_recur_body(*refs, geo):
    refs = list(refs)
    h_ref, ab_ref, wc_ref, alog_ref, dtb_ref, ng_ref, eall_ref = refs[:7]
    pos = 7
    conv0_ref = s0_ref = x_ref = wout_ref = lng_ref = lnb_ref = None
    if geo.long:
        x_ref, wout_ref, lng_ref, lnb_ref = refs[pos:pos + 4]
        pos += 4
    else:
        conv0_ref, s0_ref = refs[pos:pos + 2]
        pos += 2
    og_ref, convn_ref, sout_ref = refs[pos:pos + 3]
    pos += 3
    qkv_scr, g_scr, beta_scr, u_scr, w_scr, att_scr, egc_scr, kend_scr, qk_scr = refs[pos:pos + 9]
    pos += 9
    carry_scr = s_scr = None
    og_scr = og_ref
    if geo.long:
        carry_scr, og_scr, s_scr = refs[pos:pos + 3]
    c, seg, tc = CHUNK, geo.seg, geo.tc
    n_seg = c // seg
    get_state, put_state = _state_access(geo, s0_ref, sout_ref, s_scr)
    pad = SUBLANES
    lo = pad - (GDN_CONV - 1)
    n_pair = GDN_HEADS // 2
    pw = 2 * c
    hw = GDN_HEADS * c
    cpi = 2 if (tc // c) % 2 == 0 else 1

    if geo.long:
        @pl.when(pl.program_id(1) == 0)
        def _():
            carry_scr[...] = jnp.zeros_like(carry_scr)
            s_scr[...] = jnp.zeros_like(s_scr)

    def window_conv(e, cols, t_in=None):
        conv = e[pad:pad + c] * wc_ref[GDN_CONV - 1:GDN_CONV, cols]
        for j in range(GDN_CONV - 1):
            tap = e[lo + j:lo + j + c]
            if t_in is not None:
                tap = jnp.where(t_in >= GDN_CONV - 1 - j, tap, 0.0)
            conv = conv + tap * wc_ref[j:j + 1, cols]
        return conv

    def conv_chunk(m):
        static = isinstance(m, int)
        r0 = m * c if static else pl.multiple_of(m * c, c)
        for cb in range(GDN_CONV_CH // LANES):
            cols = slice(cb * LANES, (cb + 1) * LANES)
            if static and m == 0:
                e = jnp.concatenate([carry_scr[:, cols], h_ref[0:c, cols]], axis=0)
            elif static:
                e = h_ref[r0 - pad:r0 + c, cols]
            else:
                e = h_ref[pl.ds(pl.multiple_of(m * c - pad, SUBLANES), c + pad), cols]
            qkv_scr[pl.ds(r0, c), cols] = _silu(window_conv(e, cols))

    if geo.long:
        for i in range(cpi):
            conv_chunk(i)

        @pl.when(geo.last_time_block())
        def _():
            convn_ref[0, 0] = h_ref[tc - (GDN_CONV - 1):tc, 0:GDN_CONV_CH]
    else:
        t_in = lax.broadcasted_iota(jnp.int32, (c, LANES), 0) & (seg - 1)
        for cb in range(GDN_CONV_CH // LANES):
            cols = slice(cb * LANES, (cb + 1) * LANES)
            e = jnp.concatenate([jnp.zeros((pad, LANES), F32), h_ref[:, cols]], axis=0)
            qkv_scr[:, cols] = window_conv(e, cols, t_in)
        for sq in range(geo.seqs):
            c0 = conv0_ref[0, sq]
            head = []
            for t in range(GDN_CONV - 1):
                acc = c0[t:t + 1, :] * wc_ref[0:1, :]
                for j in range(1, GDN_CONV - 1 - t):
                    acc = acc + c0[t + j:t + j + 1, :] * wc_ref[j:j + 1, :]
                head.append(acc)
            head.append(jnp.zeros((SUBLANES - (GDN_CONV - 1), GDN_CONV_CH), F32))
            first = slice(sq * seg, sq * seg + SUBLANES)
            qkv_scr[first, :] = qkv_scr[first, :] + jnp.concatenate(head, axis=0)
            convn_ref[0, sq] = h_ref[(sq + 1) * seg - (GDN_CONV - 1):(sq + 1) * seg, 0:GDN_CONV_CH]
        qkv_scr[...] = _silu(qkv_scr[...])

    g_scr[...] = -jnp.exp(alog_ref[...]) * _softplus(ab_ref[:, 0:LANES] + dtb_ref[...])
    beta_scr[...] = _sigmoid(ab_ref[:, LANES:2 * LANES])

    incl, strict, delta = _seg_masks((c, hw), seg)
    eye_pair = jnp.where(delta[:, 0:pw], 1.0, 0.0)
    c_bits, dk_bits = int(math.log2(c)), int(math.log2(GDN_DK))
    r2 = lax.broadcasted_iota(jnp.int32, (2 * c, pw), 0) >> c_bits
    mask_cc = r2 == (lax.broadcasted_iota(jnp.int32, (2 * c, pw), 1) >> c_bits)
    r3 = lax.broadcasted_iota(jnp.int32, (2 * c, 2 * GDN_DK), 0) >> c_bits
    mask_cd = r3 == (lax.broadcasted_iota(jnp.int32, (2 * c, 2 * GDN_DK), 1) >> dk_bits)
    ones_c = jnp.ones((c, c), BF16)
    ng = ng_ref[...]

    def series_inverse(a_list, eye, n, nil, mask, tick):
        ps = [eye - a for a in a_list]
        xs = [_mm3_pair(a, a, mask) for a in a_list]
        tick()
        steps = int(math.log2(nil)) - 1
        for s in range(steps):
            if s < steps - 1:
                rs = [_mm3_pair(jnp.concatenate([pp, x], axis=0), x, mask) for pp, x in zip(ps, xs)]
                ps = [pp + r[0:n] for pp, r in zip(ps, rs)]
                xs = [r[n:2 * n] for r in rs]
            else:
                ps = [pp + _mm3_pair(pp, x, mask) for pp, x in zip(ps, xs)]
            tick()
        return ps

    def gate_forms(rows):
        gc = _cumsum_rows(g_scr[rows, :], seg)
        e_all = eall_ref[...]
        sh, sm, sl = _split3_bf(jnp.concatenate([gc, beta_scr[rows, :]], axis=0))
        ex = _dot(sh, e_all) + (_dot(sm, e_all) + _dot(sl, e_all))
        gcol, bcol = ex[0:c], ex[c:2 * c]
        gwide = jnp.concatenate([jnp.broadcast_to(gc[:, hh:hh + 1], (c, GDN_DK)) for hh in range(GDN_HEADS)],
                                axis=1)
        th, tm, tl = _split3_bf(jnp.concatenate(
            [jnp.where(delta, gcol, 0.0), jnp.where(delta, bcol, 0.0)], axis=1))
        rowf = _dot(ones_c, th) + (_dot(ones_c, tm) + _dot(ones_c, tl))
        grow, brow = rowf[:, 0:hw], rowf[:, hw:]
        dec = jnp.where(incl, jnp.exp(jnp.where(incl, gcol - grow, 0.0)), 0.0)
        egc_scr[rows, :] = jnp.exp(gwide)
        eend = jnp.exp(_seg_last_rows(gwide, seg) - gwide)
        return dec, bcol, brow, jnp.exp(grow), eend

    def local(n, side=None):
        tick = (lambda: None) if side is None else (lambda: next(side, None))
        chunk_rows = [pl.ds(_aligned((n * cpi + i) * c, c), c) for i in range(cpi)]
        forms = [gate_forms(rows) for rows in chunk_rows]
        units = [(i, p) for i in range(cpi) for p in range(n_pair)]
        tick()

        a_pairs = []
        for i, p in units:
            rows = chunk_rows[i]
            dec, bcol, _, _, eend = forms[i]
            qs, ks = [], []
            for hh in (2 * p, 2 * p + 1):
                kq = slice(hh * GDN_DK, (hh + 1) * GDN_DK)
                kk = slice(GDN_KW + hh * GDN_DK, GDN_KW + (hh + 1) * GDN_DK)
                q = qkv_scr[rows, kq]
                k = qkv_scr[rows, kk]
                q = q * lax.rsqrt(jnp.sum(q * q, axis=-1, keepdims=True) + NORM_EPS) * GDN_DK ** -0.5
                k = k * lax.rsqrt(jnp.sum(k * k, axis=-1, keepdims=True) + NORM_EPS)
                qk_scr[rows, kq] = q
                qk_scr[rows, kk] = k
                kend_scr[rows, kq] = k * eend[:, kq]
                qs.append(q)
                ks.append(k)
            kh = _bf(jnp.concatenate(ks, axis=1))
            kbd_h = _bf(_block_rows(kh.astype(F32), mask_cd))
            kkqk = _dot_nt(jnp.concatenate([kh, _bf(jnp.concatenate(qs, axis=1))], axis=0), kbd_h)
            cs = slice(p * pw, (p + 1) * pw)
            a_pairs.append(jnp.where(strict[:, cs], kkqk[0:c] * dec[:, cs] * bcol[:, cs], 0.0))
            att_scr[rows, cs] = kkqk[c:2 * c] * dec[:, cs]
        tick()
        ps = series_inverse(a_pairs, eye_pair, c, seg, mask_cc, tick)
        for t_inv, (i, p) in zip(ps, units):
            rows = chunk_rows[i]
            _, _, brow, egrow, _ = forms[i]
            cs = slice(p * pw, (p + 1) * pw)
            vs = slice(p * 2 * GDN_DV, (p + 1) * 2 * GDN_DV)
            t_b = t_inv * brow[:, cs]
            vh, vl = _split2_bf(qkv_scr[rows, 2 * GDN_KW + p * 2 * GDN_DV:2 * GDN_KW + (p + 1) * 2 * GDN_DV])
            u_scr[rows, vs] = _mm3_split(t_b, _bf(_block_rows(vh.astype(F32), mask_cd)),
                                         _bf(_block_rows(vl.astype(F32), mask_cd)))
            kh, kl = _split2_bf(qk_scr[rows, GDN_KW + p * 2 * GDN_DK:GDN_KW + (p + 1) * 2 * GDN_DK])
            w_scr[rows, vs] = _mm3_split(t_b * egrow[:, cs], _bf(_block_rows(kh.astype(F32), mask_cd)),
                                         _bf(_block_rows(kl.astype(F32), mask_cd)))
        if geo.long:
            for i in range(cpi):
                nxt, cap = (n + 1) * cpi + i, tc // c - cpi + i
                conv_chunk(min(nxt, cap) if isinstance(n, int) else jnp.minimum(nxt, cap))
        if side is not None:
            for _ in side:
                pass

    def recur(n):
        r0 = _aligned(n * c, c)
        rows = pl.ds(r0, c)
        heads = range(GDN_HEADS)
        kqs = [slice(hh * GDN_DK, (hh + 1) * GDN_DK) for hh in heads]
        vn_seg = [[] for _ in heads]
        qs_seg = [[] for _ in heads]
        for s in range(n_seg):
            sq = n * n_seg + s
            sr = pl.ds(_aligned(r0 + s * seg, SUBLANES), seg)
            tail = pl.ds(_aligned(r0 + (s + 1) * seg - SUBLANES, SUBLANES), SUBLANES)
            sts = [get_state(sq, hh) for hh in heads]
            m1s = [_dot(_bf(jnp.concatenate([w_scr[sr, kqs[hh]], qk_scr[sr, kqs[hh]]], axis=0)), _bf(sts[hh]))
                   for hh in heads]
            yield
            vn = [u_scr[sr, kqs[hh]] - m1s[hh][0:seg] for hh in heads]
            upd = [_dot_tn(_bf(kend_scr[sr, kqs[hh]]), _bf(vn[hh])) for hh in heads]
            yield
            for hh in heads:
                e_last = egc_scr[tail, kqs[hh]][SUBLANES - 1:SUBLANES, :]
                put_state(sq, hh, sts[hh] * e_last + upd[hh])
                vn_seg[hh].append(vn[hh])
                qs_seg[hh].append(m1s[hh][seg:2 * seg])
        join = lambda parts: parts[0] if n_seg == 1 else jnp.concatenate(parts, axis=0)
        o_pairs = []
        for p in range(n_pair):
            pair = (2 * p, 2 * p + 1)
            vs = slice(p * 2 * GDN_DV, (p + 1) * 2 * GDN_DV)
            vbd = _bf(_block_rows(jnp.concatenate([join(vn_seg[hh]) for hh in pair], axis=1), mask_cd))
            o_pairs.append(_dot(_bf(att_scr[rows, p * pw:(p + 1) * pw]), vbd)
                           + egc_scr[rows, vs] * jnp.concatenate([join(qs_seg[hh]) for hh in pair], axis=1))
        yield
        for p in range(n_pair):
            for i, hh in enumerate((2 * p, 2 * p + 1)):
                o = o_pairs[p][:, i * GDN_DV:(i + 1) * GDN_DV]
                o = o * lax.rsqrt(jnp.mean(o * o, axis=-1, keepdims=True) + NORM_EPS) * ng
                kz = slice(GDN_CONV_CH + hh * GDN_DV, GDN_CONV_CH + (hh + 1) * GDN_DV)
                og_scr[rows, hh * GDN_DV:(hh + 1) * GDN_DV] = o * _silu(h_ref[rows, kz])
        yield

    def recur_trip(n):
        for i in range(cpi):
            yield from recur(n * cpi + i)

    def drain(gen):
        for _ in gen:
            pass

    n_trip = tc // (c * cpi)
    local(0)

    def trip(n, carry):
        local(n, recur_trip(n - 1))
        return carry

    lax.fori_loop(1, n_trip, trip, 0)
    drain(recur_trip(n_trip - 1))

    if geo.long:
        carry_scr[...] = h_ref[tc - pad:tc, 0:GDN_CONV_CH]
        _out_ln_body(og_scr, x_ref, wout_ref, lng_ref, lnb_ref, og_ref)

        @pl.when(geo.last_time_block())
        def _():
            sout_ref[0, 0] = s_scr[...]


def _gdn_recur(h, ab, w_conv, a_log, dt_bias, norm_g, conv_in, state_in, layer, n_layers, geo, out_ln):
    tail = (GDN_HEADS, GDN_DK, GDN_DV)
    cv_tail = (GDN_CONV - 1, GDN_CONV_CH)
    consts = [w_conv, a_log, dt_bias, norm_g, _gdn_expand_matrix()]
    ins = [h, ab] + consts
    in_specs = [pl.BlockSpec((geo.tc, GDN_MAIN), geo.row_map), pl.BlockSpec((geo.tc, 2 * LANES), geo.row_map)]
    in_specs += [_const_spec(a) for a in consts]
    tc = geo.tc
    if geo.long:
        ins += list(out_ln)
        in_specs += [pl.BlockSpec((tc, D_MODEL), geo.row_map)] + [_const_spec(a) for a in out_ln[1:]]
    else:
        assert tc == CHUNK
        ins += [conv_in, state_in]
        in_specs += [geo.state_spec(layer, cv_tail), geo.state_spec(layer, tail)]
    scratch = [pltpu.VMEM((tc, GDN_CONV_CH), F32),
               pltpu.VMEM((tc, LANES), F32), pltpu.VMEM((tc, LANES), F32),
               pltpu.VMEM((tc, GDN_VW), F32), pltpu.VMEM((tc, GDN_KW), F32),
               pltpu.VMEM((tc, GDN_HEADS * CHUNK), F32), pltpu.VMEM((tc, GDN_KW), F32),
               pltpu.VMEM((tc, GDN_KW), F32), pltpu.VMEM((tc, 2 * GDN_KW), F32)]
    if geo.long:
        scratch += [pltpu.VMEM((SUBLANES, GDN_CONV_CH), F32), pltpu.VMEM((tc, GDN_VW), F32),
                    pltpu.VMEM(tail, F32)]
    return pl.pallas_call(
        functools.partial(_gdn_recur_body, geo=geo),
        grid=geo.grid, in_specs=in_specs,
        out_specs=[pl.BlockSpec((tc, GDN_VW), geo.row_map), geo.state_spec(layer, cv_tail),
                   geo.state_spec(layer, tail)],
        out_shape=[jax.ShapeDtypeStruct((geo.n_b * geo.n_l, GDN_VW), F32),
                   jax.ShapeDtypeStruct((n_layers, geo.n_b) + cv_tail, F32),
                   jax.ShapeDtypeStruct((n_layers, geo.n_b) + tail, F32)],
        scratch_shapes=scratch,
        compiler_params=_params(geo.sem), name="gdn_recur",
    )(*ins)


def _s5_body(*refs, tt, nb, has_init):
    if has_init:
        (x_ref, win_ref, wb_ref, are_ref, aim_ref, wc_ref, d_ref, wglu_ref, bglu_ref, wout_ref,
         lng_ref, lnb_ref, h0re_ref, h0im_ref, y_ref, hre_ref, him_ref,
         x_scr, u_scr, z_scr, bu_scr, y_scr, st_scr) = refs
    else:
        (x_ref, win_ref, wb_ref, are_ref, aim_ref, wc_ref, d_ref, wglu_ref, bglu_ref, wout_ref,
         lng_ref, lnb_ref, y_ref, hre_ref, him_ref,
         x_scr, u_scr, z_scr, bu_scr, y_scr, st_scr) = refs
    tb = pl.program_id(1)
    ns = S5_KT_STATES

    @pl.when(tb == 0)
    def _():
        if not has_init:
            st_scr[...] = jnp.zeros_like(st_scr)
        else:
            for kt in range(S5_KT):
                st_scr[:, 2 * kt * ns:(2 * kt + 1) * ns] = h0re_ref[0, :, kt * ns:(kt + 1) * ns]
                st_scr[:, (2 * kt + 1) * ns:(2 * kt + 2) * ns] = h0im_ref[0, :, kt * ns:(kt + 1) * ns]

    x_scr[...] = jnp.swapaxes(x_ref[...], 0, 1).reshape(tt * nb, D_MODEL)
    h = _dot(_bf(x_scr[...]), win_ref[...])
    u_scr[...] = h[:, 0:S5_WIDTH]
    z_scr[...] = h[:, S5_WIDTH:]

    def input_map(kt):
        cols = slice(kt * S5_KT_W, (kt + 1) * S5_KT_W)
        bu_scr[kt % 2] = _dot(_bf(u_scr[:, cols]), wb_ref[kt])

    def scan(kt):
        bu = bu_scr.at[kt % 2]
        a_re = jnp.broadcast_to(are_ref[kt], (SUBLANES, ns))
        a_im = jnp.broadcast_to(aim_ref[kt], (SUBLANES, ns))
        base = kt * 2 * ns
        for rb in range(nb // SUBLANES):
            st_rows = slice(rb * SUBLANES, (rb + 1) * SUBLANES)
            h_re = st_scr[st_rows, base:base + ns]
            h_im = st_scr[st_rows, base + ns:base + 2 * ns]
            for t in range(tt):
                rows = slice(t * nb + rb * SUBLANES, t * nb + (rb + 1) * SUBLANES)
                h_re, h_im = (a_re * h_re - a_im * h_im + bu[rows, 0:ns],
                              a_re * h_im + a_im * h_re + bu[rows, ns:2 * ns])
                bu[rows, 0:ns] = h_re
                bu[rows, ns:2 * ns] = h_im
            st_scr[st_rows, base:base + ns] = h_re
            st_scr[st_rows, base + ns:base + 2 * ns] = h_im

    def output_map(kt):
        cols = slice(kt * S5_KT_W, (kt + 1) * S5_KT_W)
        y_scr[:, cols] = _dot(_bf(bu_scr[kt % 2]), wc_ref[kt]) + d_ref[:, cols] * u_scr[:, cols]

    input_map(0)
    for kt in range(S5_KT):
        if kt + 1 < S5_KT:
            input_map(kt + 1)
        scan(kt)
        output_map(kt)

    th = tt // 2
    halves = [slice(i * th * nb, (i + 1) * th * nb) for i in range(2)]

    def glu(r):
        return _dot(_bf(_gelu_tanh(y_scr[r, :])), wglu_ref[...]) + bglu_ref[...]

    def gated_out(r, yg):
        y = yg[:, 0:S5_WIDTH] * _sigmoid(yg[:, S5_WIDTH:]) * _silu(z_scr[r, :])
        return _dot(_bf(y), wout_ref[...])

    def finish(i, out):
        y = _layer_norm(ALPHA_RES * x_scr[halves[i], :] + out, lng_ref[...], lnb_ref[...])
        y_ref[:, i * th:(i + 1) * th, :] = jnp.swapaxes(y.reshape(th, nb, D_MODEL), 0, 1)

    yg0 = glu(halves[0])
    yg1 = glu(halves[1])
    out0 = gated_out(halves[0], yg0)
    out1 = gated_out(halves[1], yg1)
    finish(0, out0)
    finish(1, out1)

    @pl.when(tb == pl.num_programs(1) - 1)
    def _():
        for kt in range(S5_KT):
            hre_ref[0, :, kt * ns:(kt + 1) * ns] = st_scr[:, 2 * kt * ns:(2 * kt + 1) * ns]
            him_ref[0, :, kt * ns:(kt + 1) * ns] = st_scr[:, (2 * kt + 1) * ns:(2 * kt + 2) * ns]


def _s5_discretize(lam_re, lam_im, log_dt, b_re, b_im, c_re, c_im):
    dt = jnp.exp(log_dt)[:, None]
    mag = jnp.exp(lam_re * dt)
    ab_re, ab_im = mag * jnp.cos(lam_im * dt), mag * jnp.sin(lam_im * dt)
    den = jnp.square(lam_re) + jnp.square(lam_im)
    num_re = ab_re - 1.0
    coef_re = (num_re * lam_re + ab_im * lam_im) / den
    coef_im = (ab_im * lam_re - num_re * lam_im) / den
    bb_re = coef_re[..., None] * b_re - coef_im[..., None] * b_im
    bb_im = coef_re[..., None] * b_im + coef_im[..., None] * b_re
    gl = S5_GROUPS // S5_KT
    rep = (jnp.arange(S5_KT_STATES)[None, :] % S5_STATE == jnp.arange(S5_STATE)[:, None]).astype(F32)
    same = jnp.arange(S5_KT_W)[:, None] // S5_GROUP == jnp.arange(S5_KT_STATES)[None, :] // S5_STATE
    exact = lax.Precision.HIGHEST

    def block_b(bb):
        t = bb.reshape(S5_KT, gl, S5_STATE, S5_GROUP).transpose(0, 1, 3, 2).reshape(S5_KT, S5_KT_W, S5_STATE)
        return jnp.where(same, jnp.einsum("krp,pn->krn", t, rep, precision=exact), 0.0)

    def block_c(cc):
        t = cc.reshape(S5_KT, gl, S5_GROUP, S5_STATE).transpose(0, 3, 1, 2).reshape(S5_KT, S5_STATE, S5_KT_W)
        return jnp.where(same.T, jnp.einsum("pn,kpr->knr", rep, t, precision=exact), 0.0)

    w_b = jnp.concatenate([block_b(bb_re), block_b(bb_im)], axis=2).astype(BF16)
    w_c = jnp.concatenate([block_c(c_re), -block_c(c_im)], axis=1).astype(BF16)
    a_re = ab_re.reshape(S5_KT, 1, S5_KT_STATES)
    a_im = ab_im.reshape(S5_KT, 1, S5_KT_STATES)
    return w_b, w_c, a_re, a_im


def _s5_layer(x, prep, w_in, d_vec, w_glu, b_glu, w_out, ln_g, ln_b, h0, n_b, n_l):
    w_b, w_c, a_re, a_im = prep
    if n_l >= S5_LONG_STEPS:
        nb, tt = n_b, S5_LONG_STEPS
    else:
        nb, tt = min(S5_SHORT_SEQS, n_b), n_l
    assert nb % SUBLANES == 0 and tt % SUBLANES == 0
    rows = tt * nb
    has_init = h0 is not None
    consts = [w_in, w_b, a_re, a_im, w_c, d_vec, w_glu, b_glu, w_out, ln_g, ln_b]
    x_spec = pl.BlockSpec((nb, tt, D_MODEL), lambda b, t: (b, t, 0))
    st_spec = pl.BlockSpec((1, nb, S5_NSTATE), lambda b, t: (0, b, 0))
    ins = [x.reshape(n_b, n_l, D_MODEL)] + consts
    in_specs = [x_spec] + [_const_spec(a) for a in consts]
    if has_init:
        ins += list(h0)
        in_specs += [st_spec, st_spec]
    y, h_re, h_im = pl.pallas_call(
        functools.partial(_s5_body, tt=tt, nb=nb, has_init=has_init),
        grid=(n_b // nb, n_l // tt), in_specs=in_specs,
        out_specs=[x_spec, st_spec, st_spec],
        out_shape=[jax.ShapeDtypeStruct((n_b, n_l, D_MODEL), F32),
                   jax.ShapeDtypeStruct((1, n_b, S5_NSTATE), F32),
                   jax.ShapeDtypeStruct((1, n_b, S5_NSTATE), F32)],
        scratch_shapes=[pltpu.VMEM((rows, D_MODEL), F32),
                        pltpu.VMEM((rows, S5_WIDTH), F32), pltpu.VMEM((rows, S5_WIDTH), F32),
                        pltpu.VMEM((2, rows, 2 * S5_KT_STATES), F32), pltpu.VMEM((rows, S5_WIDTH), F32),
                        pltpu.VMEM((nb, 2 * S5_NSTATE), F32)],
        compiler_params=_params(("parallel", "arbitrary")), name="s5_layer",
    )(*ins)
    return y.reshape(n_b * n_l, D_MODEL), h_re, h_im


def _row2(v):
    return v.reshape(1, -1).astype(F32)


def _pad_cols(w, n):
    return jnp.concatenate([w, jnp.zeros((w.shape[0], n - w.shape[1]), w.dtype)], axis=1)


def _trunk(x, n_b, n_l, states, wts):
    geo = _Geometry(n_b, n_l)
    n_gla, n_gdn, n_s5 = (DEPTH + 2) // 3, (DEPTH + 1) // 3, DEPTH // 3
    assert n_gdn == 1 and n_s5 == 1
    s_gla = s_gdn = s_conv = s_re = s_im = None
    for i in range(DEPTH):
        j, kind = divmod(i, 3)
        ln_g, ln_b = _row2(wts["ln_g"][i]), _row2(wts["ln_b"][i])
        if kind == 0:
            w_in = wts["gla_w_in"][j]
            w_main = _bf(w_in)
            w_lr = _bf(_pad_cols(w_in[:, GLA_MAIN:], LANES))
            w_a2 = _bf(jnp.concatenate(
                [wts["gla_w_a2"][j], jnp.zeros((LANES - GLA_LOWRANK, GLA_KW), F32)], axis=0))
            h, la = _rowwise_call(_gla_proj_body, "gla_proj", [x],
                                  [w_main, w_lr, w_a2, _row2(wts["gla_b_a"][j])], [GLA_MAIN, GLA_KW])
            w_out = _bf(wts["gla_w_out"][j])
            og, s_gla = _gla_recur(h, la, _row2(wts["gla_norm_g"][j]),
                                   None if states is None else states[0], j, n_gla, s_gla, geo,
                                   (x, w_out, ln_g, ln_b))
        elif kind == 1:
            w_in = wts["gdn_w_in"][j]
            w_main = _bf(w_in)
            w_ab = _bf(jnp.concatenate(
                [_pad_cols(w_in[:, GDN_MAIN:GDN_MAIN + GDN_HEADS], LANES),
                 _pad_cols(w_in[:, GDN_MAIN + GDN_HEADS:], LANES)], axis=1))
            h, ab = _rowwise_call(_gdn_proj_body, "gdn_proj", [x], [w_main, w_ab], [GDN_MAIN, 2 * LANES])
            w_out = _bf(wts["gdn_w_out"][j])
            og, s_conv, s_gdn = _gdn_recur(
                h, ab, wts["gdn_w_conv"][j].astype(F32),
                _pad_cols(_row2(wts["gdn_a_log"][j]), LANES), _pad_cols(_row2(wts["gdn_dt_bias"][j]), LANES),
                _row2(wts["gdn_norm_g"][j]),
                None if states is None else states[2], None if states is None else states[1],
                j, n_gdn, geo, (x, w_out, ln_g, ln_b))
        else:
            prep = _s5_discretize(wts["s5_lam_re"][j].astype(F32), wts["s5_lam_im"][j].astype(F32),
                                  wts["s5_log_dt"][j].astype(F32), wts["s5_b_re"][j].astype(F32),
                                  wts["s5_b_im"][j].astype(F32), wts["s5_c_re"][j].astype(F32),
                                  wts["s5_c_im"][j].astype(F32))
            h0 = None if states is None else (states[3].reshape(n_s5, n_b, S5_NSTATE),
                                              states[4].reshape(n_s5, n_b, S5_NSTATE))
            x, h_re, h_im = _s5_layer(x, prep, _bf(wts["s5_w_in"][j]), _row2(wts["s5_d"][j]),
                                      _bf(wts["s5_w_glu"][j]), _row2(wts["s5_b_glu"][j]),
                                      _bf(wts["s5_w_out"][j]), ln_g, ln_b, h0, n_b, n_l)
            s_re = h_re.reshape(n_s5, n_b, S5_GROUPS, S5_STATE)
            s_im = h_im.reshape(n_s5, n_b, S5_GROUPS, S5_STATE)
            continue
        if geo.long:
            x = og
        else:
            (x,) = _rowwise_call(_out_ln_body, "out_ln", [og, x], [w_out, ln_g, ln_b], [D_MODEL])
    return x, s_gla, s_gdn, s_conv, s_re, s_im


def kernel(x_prompt, x_sample, state_gla, state_gdn, state_gdn_conv, state_s5_re, state_s5_im,
           ln_g, ln_b, gla_w_in, gla_w_a2, gla_b_a, gla_norm_g, gla_w_out,
           gdn_w_in, gdn_w_conv, gdn_a_log, gdn_dt_bias, gdn_norm_g, gdn_w_out,
           s5_w_in, s5_lam_re, s5_lam_im, s5_log_dt, s5_b_re, s5_b_im, s5_c_re, s5_c_im,
           s5_d, s5_w_glu, s5_b_glu, s5_w_out):
    wts = dict(ln_g=ln_g, ln_b=ln_b,
               gla_w_in=gla_w_in, gla_w_a2=gla_w_a2, gla_b_a=gla_b_a, gla_norm_g=gla_norm_g,
               gla_w_out=gla_w_out,
               gdn_w_in=gdn_w_in, gdn_w_conv=gdn_w_conv, gdn_a_log=gdn_a_log, gdn_dt_bias=gdn_dt_bias,
               gdn_norm_g=gdn_norm_g, gdn_w_out=gdn_w_out,
               s5_w_in=s5_w_in, s5_lam_re=s5_lam_re, s5_lam_im=s5_lam_im, s5_log_dt=s5_log_dt,
               s5_b_re=s5_b_re, s5_b_im=s5_b_im, s5_c_re=s5_c_re, s5_c_im=s5_c_im, s5_d=s5_d,
               s5_w_glu=s5_w_glu, s5_b_glu=s5_b_glu, s5_w_out=s5_w_out)

    bp, lp, _ = x_prompt.shape
    yp, p_gla, p_gdn, p_conv, p_re, p_im = _trunk(
        x_prompt.reshape(bp * lp, D_MODEL), bp, lp, None, wts)
    bs, ls, _ = x_sample.shape
    ys, s_gla, s_gdn, s_conv, s_re, s_im = _trunk(
        x_sample.reshape(bs * ls, D_MODEL), bs, ls,
        (state_gla, state_gdn, state_gdn_conv, state_s5_re, state_s5_im), wts)
    return (yp.reshape(bp, lp, D_MODEL), ys.reshape(bs, ls, D_MODEL),
            p_gla, p_gdn, p_conv, p_re, p_im, s_gla, s_gdn, s_conv, s_re, s_im)
```

```python
import functools
import math

import jax
import jax.numpy as jnp
from jax import lax
from jax.experimental import pallas as pl
from jax.experimental.pallas import tpu as pltpu

F32 = jnp.float32
BF16 = jnp.bfloat16

D_MODEL = 1024
DEPTH = 4
ALPHA_RES = (2 * DEPTH) ** 0.25
LN_EPS = 1e-5
NORM_EPS = 1e-6
CHUNK = 64

GLA_HEADS = 4
GLA_KW = 512
GLA_VW = 1024
GLA_DK = 128
GLA_DV = 256
GLA_LOWRANK = 16
GLA_TAU = 16.0
GLA_MAIN = 2 * GLA_KW + 2 * GLA_VW

GDN_HEADS = 8
GDN_DK = 128
GDN_DV = 128
GDN_KW = 1024
GDN_VW = 1024
GDN_CONV = 4
GDN_CONV_CH = 3072
GDN_MAIN = GDN_CONV_CH + GDN_VW

S5_WIDTH = 1024
S5_GROUP = 16
S5_GROUPS = 64
S5_STATE = 64
S5_KT = 4
S5_KT_W = S5_WIDTH // S5_KT
S5_KT_STATES = (S5_GROUPS // S5_KT) * S5_STATE
S5_NSTATE = S5_GROUPS * S5_STATE

LANES = 128
SUBLANES = 8
VMEM_LIMIT = 56 * 1024 * 1024
ROW_BLOCK = 512
LONG_TIME_BLOCK = 512
S5_LONG_STEPS = 64
S5_SHORT_SEQS = 64


def _bf(x):
    return x.astype(BF16)


def _dot(a, b):
    return jnp.dot(a, b, preferred_element_type=F32)


def _dot_nt(a, b):
    return lax.dot_general(a, b, (((1,), (1,)), ((), ())), preferred_element_type=F32)


def _dot_tn(a, b):
    return lax.dot_general(a, b, (((0,), (0,)), ((), ())), preferred_element_type=F32)


def _split2_bf(x):
    h = x.astype(BF16)
    return h, (x - h.astype(F32)).astype(BF16)


def _split3_bf(x):
    h = x.astype(BF16)
    r = x - h.astype(F32)
    m = r.astype(BF16)
    return h, m, (r - m.astype(F32)).astype(BF16)


def _sigmoid(x):
    return 1.0 / (1.0 + jnp.exp(-x))


def _silu(x):
    return x * _sigmoid(x)


def _softplus(x):
    return jnp.maximum(x, 0.0) + jnp.log(1.0 + jnp.exp(-jnp.abs(x)))


def _gelu_tanh(x):
    return 0.5 * x * (1.0 + jnp.tanh(math.sqrt(2.0 / math.pi) * (x + 0.044715 * (x * x * x))))


def _layer_norm(x, g, b):
    mu = jnp.mean(x, axis=-1, keepdims=True)
    xc = x - mu
    var = jnp.mean(xc * xc, axis=-1, keepdims=True)
    return xc * lax.rsqrt(var + LN_EPS) * g + b


def _cumsum_rows(x, seg):
    row = lax.broadcasted_iota(jnp.int32, x.shape, 0) & (seg - 1)
    s = 1
    while s < seg:
        x = x + jnp.where(row >= s, pltpu.roll(x, s, axis=0), 0.0)
        s *= 2
    return x


def _seg_last_rows(x, seg):
    c, w = x.shape
    if seg == c:
        return jnp.broadcast_to(x[c - 1:c, :], (c, w))
    x3 = x.reshape(c // seg, seg, w)
    return jnp.broadcast_to(x3[:, seg - 1:seg, :], x3.shape).reshape(c, w)


def _seg_masks(shape, seg):
    ri = lax.broadcasted_iota(jnp.int32, shape, 0)
    li = lax.broadcasted_iota(jnp.int32, shape, 1) & (CHUNK - 1)
    shift = int(math.log2(seg))
    same = (ri >> shift) == (li >> shift)
    return same & (ri >= li), same & (ri > li), ri == li


def _aligned(v, m):
    return v if isinstance(v, int) else pl.multiple_of(v, m)


def _const_spec(arr):
    nd = arr.ndim
    return pl.BlockSpec(arr.shape, lambda *_: (0,) * nd, pipeline_mode=pl.Buffered(1))


def _params(sem):
    return pltpu.CompilerParams(dimension_semantics=sem, vmem_limit_bytes=VMEM_LIMIT)


def _rowwise_call(body, name, rows, consts, out_widths):
    n = rows[0].shape[0]
    tm = min(ROW_BLOCK, n)
    spec = lambda w: pl.BlockSpec((tm, w), lambda i: (i, 0))
    return pl.pallas_call(
        body, grid=(n // tm,),
        in_specs=[spec(r.shape[1]) for r in rows] + [_const_spec(c) for c in consts],
        out_specs=[spec(w) for w in out_widths],
        out_shape=[jax.ShapeDtypeStruct((n, w), F32) for w in out_widths],
        compiler_params=_params(("parallel",)), name=name,
    )(*rows, *consts)


def _gla_proj_body(x_ref, w_ref, wlr_ref, wa2_ref, ba_ref, h_ref, la_ref):
    xb = _bf(x_ref[...])
    lr = _dot(xb, wlr_ref[...])
    pre = _dot(_bf(lr), wa2_ref[...]) + ba_ref[...]
    la_ref[...] = -_softplus(-pre) * (1.0 / GLA_TAU)
    h = _dot(xb, w_ref[:, 0:GLA_MAIN])
    h_ref[:, :GLA_KW] = h[:, :GLA_KW] * GLA_DK ** -0.5
    h_ref[:, GLA_KW:] = h[:, GLA_KW:]


def _gdn_proj_body(x_ref, w_ref, wab_ref, h_ref, ab_ref):
    xb = _bf(x_ref[...])
    h_ref[...] = _dot(xb, w_ref[:, 0:GDN_MAIN])
    ab_ref[...] = _dot(xb, wab_ref[...])


def _out_ln_body(og_ref, x_ref, w_ref, g_ref, b_ref, y_ref):
    n = og_ref.shape[0]
    halves = [slice(0, n // 2), slice(n // 2, n)]
    outs = [_dot(_bf(og_ref[r, :]), w_ref[...]) for r in halves]
    for r, out in zip(halves, outs):
        y_ref[r, :] = _layer_norm(ALPHA_RES * x_ref[r, :] + out, g_ref[...], b_ref[...])


class _Geometry:
    def __init__(self, n_b, n_l):
        self.n_b, self.n_l = n_b, n_l
        self.long = n_l >= CHUNK
        if self.long:
            assert n_l % LONG_TIME_BLOCK == 0
            self.seg = CHUNK
            self.tc = LONG_TIME_BLOCK
            self.n_t = n_l // self.tc
            self.grid = (n_b, self.n_t)
            self.sem = ("parallel", "arbitrary")
            self.seqs = 1
            self.row_map = lambda b, t: (b * self.n_t + t, 0)
            self.seq_block = lambda b, t: b
        else:
            assert CHUNK % n_l == 0 and n_l % SUBLANES == 0 and (n_b * n_l) % CHUNK == 0
            self.seg = n_l
            self.tc = CHUNK
            self.grid = (n_b * n_l // self.tc,)
            self.sem = ("parallel",)
            self.seqs = self.tc // n_l
            self.row_map = lambda i: (i, 0)
            self.seq_block = lambda i: i

    def state_spec(self, layer, tail, n_layers=1):
        zeros = (0,) * len(tail)
        first = layer if n_layers == 1 else 0
        return pl.BlockSpec((n_layers, self.seqs) + tail, lambda *g: (first, self.seq_block(*g)) + zeros)

    def last_time_block(self):
        return pl.program_id(1) == pl.num_programs(1) - 1 if self.long else None


def _state_access(geo, s0_ref, sout_ref, s_scr, out_layer=0):
    if geo.long:
        return (lambda sq, hh: s_scr[hh]), (lambda sq, hh, v: s_scr.__setitem__(hh, v))
    return ((lambda sq, hh: s0_ref[0, sq, hh]),
            (lambda sq, hh, v: sout_ref.__setitem__((out_layer, sq, hh), v)))


def _gla_recur_body(*refs, geo, has_alias, layer, out_layers):
    refs = list(refs)
    h_ref, la_ref, ng_ref = refs[:3]
    pos = 3
    s0_ref = x_ref = wout_ref = lng_ref = lnb_ref = None
    if geo.long:
        x_ref, wout_ref, lng_ref, lnb_ref = refs[pos:pos + 4]
        pos += 4
    else:
        s0_ref = refs[pos]
        pos += 1
    if has_alias:
        pos += 1
    og_ref, sout_ref = refs[pos:pos + 2]
    og_scr, s_scr = refs[pos + 2:pos + 4] if geo.long else (og_ref, None)
    c, seg, tc = CHUNK, geo.seg, geo.tc
    n_seg = c // seg
    out_layer = layer if out_layers > 1 else 0
    get_state, put_state = _state_access(geo, s0_ref, sout_ref, s_scr, out_layer)

    def zero_other_layers():
        for other in range(out_layers):
            if other != out_layer:
                sout_ref[other] = jnp.zeros(sout_ref.shape[1:], F32)

    if geo.long:
        @pl.when(pl.program_id(1) == 0)
        def _():
            s_scr[...] = jnp.zeros_like(s_scr)
    else:
        zero_other_layers()

    incl, _, _ = _seg_masks((c, c), seg)
    ng = ng_ref[...]

    heads = range(GLA_HEADS)
    kq = [slice(hh * GLA_DK, (hh + 1) * GLA_DK) for hh in heads]
    cpi = next(k for k in (4, 2, 1) if (tc // c) % k == 0)

    def local(n):
        rows = pl.ds(_aligned(n * c, c), c)
        b = _cumsum_rows(la_ref[rows, :], seg)
        bl = _seg_last_rows(b, seg)
        k_all = h_ref[rows, GLA_KW:2 * GLA_KW]
        q_dec = h_ref[rows, 0:GLA_KW] * jnp.exp(b)
        k_inv = k_all * jnp.exp(-b)
        k_end = k_all * jnp.exp(bl - b)
        e_bl = jnp.exp(bl)
        vs = [h_ref[rows, 2 * GLA_KW + hh * GLA_DV:2 * GLA_KW + (hh + 1) * GLA_DV] for hh in heads]
        atts = [jnp.where(incl, _dot_nt(_bf(q_dec[:, kq[hh]]), _bf(k_inv[:, kq[hh]])), 0.0) for hh in heads]
        os_ = [_dot(_bf(atts[hh]), _bf(vs[hh])) for hh in heads]
        return rows, q_dec, k_end, e_bl, vs, os_

    def state_pass(n, rows, q_dec, k_end, e_bl, vs, os_):
        inter = [[] for _ in heads]
        for s in range(n_seg):
            sq = n * n_seg + s
            sr = slice(s * seg, (s + 1) * seg)
            sts = [get_state(sq, hh) for hh in heads]
            for hh in heads:
                inter[hh].append(_dot(_bf(q_dec[sr, kq[hh]]), _bf(sts[hh])))
            upd = [_dot_tn(_bf(k_end[sr, kq[hh]]), _bf(vs[hh][sr])) for hh in heads]
            for hh in heads:
                ecol = jnp.transpose(jnp.broadcast_to(e_bl[s * seg:s * seg + 1, kq[hh]], (GLA_DK, GLA_DK)))
                ecol = jnp.concatenate([ecol] * (GLA_DV // GLA_DK), axis=1)
                put_state(sq, hh, sts[hh] * ecol + upd[hh])
        for hh in heads:
            o = os_[hh] + (inter[hh][0] if n_seg == 1 else jnp.concatenate(inter[hh], axis=0))
            o = o * lax.rsqrt(jnp.mean(o * o, axis=-1, keepdims=True) + NORM_EPS) * ng
            kr = slice(2 * GLA_KW + GLA_VW + hh * GLA_DV, 2 * GLA_KW + GLA_VW + (hh + 1) * GLA_DV)
            og_scr[rows, hh * GLA_DV:(hh + 1) * GLA_DV] = o * _silu(h_ref[rows, kr])

    def trip(n, carry):
        parts = [local(n * cpi + i) for i in range(cpi)]
        for i in range(cpi):
            state_pass(n * cpi + i, *parts[i])
        return carry

    lax.fori_loop(0, tc // (c * cpi), trip, 0)

    if geo.long:
        _out_ln_body(og_scr, x_ref, wout_ref, lng_ref, lnb_ref, og_ref)

        @pl.when(geo.last_time_block())
        def _():
            sout_ref[out_layer, 0] = s_scr[...]
            zero_other_layers()


def _gla_recur(h, la, norm_g, state_in, layer, n_layers, prev_out, geo, out_ln):
    tail = (GLA_HEADS, GLA_DK, GLA_DV)
    ins = [h, la, norm_g]
    in_specs = [pl.BlockSpec((geo.tc, GLA_MAIN), geo.row_map), pl.BlockSpec((geo.tc, GLA_KW), geo.row_map),
                _const_spec(norm_g)]
    if geo.long:
        ins += list(out_ln)
        in_specs += [pl.BlockSpec((geo.tc, D_MODEL), geo.row_map)] + [_const_spec(a) for a in out_ln[1:]]
    else:
        ins.append(state_in)
        in_specs.append(geo.state_spec(layer, tail))
    aliases = {}
    if prev_out is not None:
        aliases = {len(ins): 1}
        ins.append(prev_out)
        in_specs.append(pl.BlockSpec(memory_space=pl.ANY))
    scratch = [pltpu.VMEM((geo.tc, GLA_VW), F32), pltpu.VMEM(tail, F32)] if geo.long else []
    out_layers = n_layers if prev_out is None else 1
    og, s_out = pl.pallas_call(
        functools.partial(_gla_recur_body, geo=geo, has_alias=prev_out is not None,
                          layer=layer, out_layers=out_layers),
        grid=geo.grid, in_specs=in_specs,
        out_specs=[pl.BlockSpec((geo.tc, GLA_VW), geo.row_map), geo.state_spec(layer, tail, out_layers)],
        out_shape=[jax.ShapeDtypeStruct((geo.n_b * geo.n_l, GLA_VW), F32),
                   jax.ShapeDtypeStruct((n_layers, geo.n_b) + tail, F32)],
        scratch_shapes=scratch, input_output_aliases=aliases,
        compiler_params=_params(geo.sem), name="gla_recur",
    )(*ins)
    return og, s_out


def _block_rows(x, mask):
    return jnp.where(mask, jnp.concatenate([x, x], axis=0), 0.0)


def _mm3_split(lhs, rhs_hi, rhs_lo):
    lh, ll = _split2_bf(lhs)
    return _dot(lh, rhs_hi) + (_dot(lh, rhs_lo) + _dot(ll, rhs_hi))


def _mm3_pair(lhs, x_pair, mask):
    xh, xl = _split2_bf(x_pair)
    return _mm3_split(lhs, _bf(_block_rows(xh.astype(F32), mask)), _bf(_block_rows(xl.astype(F32), mask)))


def _gdn_expand_matrix():
    h = jnp.arange(LANES)[:, None]
    return (jnp.arange(GDN_HEADS * CHUNK)[None, :] // CHUNK == h).astype(BF16)


def _gdn_recur_body(*refs, geo):
    refs = list(refs)
    h_ref, ab_ref, wc_ref, alog_ref, dtb_ref, ng_ref, eall_ref = refs[:7]
    pos = 7
    conv0_ref = s0_ref = x_ref = wout_ref = lng_ref = lnb_ref = None
    if geo.long:
        x_ref, wout_ref, lng_ref, lnb_ref = refs[pos:pos + 4]
        pos += 4
    else:
        conv0_ref, s0_ref = refs[pos:pos + 2]
        pos += 2
    og_ref, convn_ref, sout_ref = refs[pos:pos + 3]
    pos += 3
    qkv_scr, g_scr, beta_scr, u_scr, w_scr, att_scr, egc_scr, kend_scr, qk_scr = refs[pos:pos + 9]
    pos += 9
    carry_scr = s_scr = None
    og_scr = og_ref
    if geo.long:
        carry_scr, og_scr, s_scr = refs[pos:pos + 3]
    c, seg, tc = CHUNK, geo.seg, geo.tc
    n_seg = c // seg
    get_state, put_state = _state_access(geo, s0_ref, sout_ref, s_scr)
    pad = SUBLANES
    lo = pad - (GDN_CONV - 1)
    n_pair = GDN_HEADS // 2
    pw = 2 * c
    hw = GDN_HEADS * c
    cpi = 2 if (tc // c) % 2 == 0 else 1

    if geo.long:
        @pl.when(pl.program_id(1) == 0)
        def _():
            carry_scr[...] = jnp.zeros_like(carry_scr)
            s_scr[...] = jnp.zeros_like(s_scr)

    def window_conv(e, cols, t_in=None):
        conv = e[pad:pad + c] * wc_ref[GDN_CONV - 1:GDN_CONV, cols]
        for j in range(GDN_CONV - 1):
            tap = e[lo + j:lo + j + c]
            if t_in is not None:
                tap = jnp.where(t_in >= GDN_CONV - 1 - j, tap, 0.0)
            conv = conv + tap * wc_ref[j:j + 1, cols]
        return conv

    def conv_chunk(m):
        static = isinstance(m, int)
        r0 = m * c if static else pl.multiple_of(m * c, c)
        for cb in range(GDN_CONV_CH // LANES):
            cols = slice(cb * LANES, (cb + 1) * LANES)
            if static and m == 0:
                e = jnp.concatenate([carry_scr[:, cols], h_ref[0:c, cols]], axis=0)
            elif static:
                e = h_ref[r0 - pad:r0 + c, cols]
            else:
                e = h_ref[pl.ds(pl.multiple_of(m * c - pad, SUBLANES), c + pad), cols]
            qkv_scr[pl.ds(r0, c), cols] = _silu(window_conv(e, cols))

    if geo.long:
        @pl.when(geo.last_time_block())
        def _():
            convn_ref[0, 0] = h_ref[tc - (GDN_CONV - 1):tc, 0:GDN_CONV_CH]
    else:
        t_in = lax.broadcasted_iota(jnp.int32, (c, LANES), 0) & (seg - 1)
        for cb in range(GDN_CONV_CH // LANES):
            cols = slice(cb * LANES, (cb + 1) * LANES)
            e = jnp.concatenate([jnp.zeros((pad, LANES), F32), h_ref[:, cols]], axis=0)
            qkv_scr[:, cols] = window_conv(e, cols, t_in)
        for sq in range(geo.seqs):
            c0 = conv0_ref[0, sq]
            head = []
            for t in range(GDN_CONV - 1):
                acc = c0[t:t + 1, :] * wc_ref[0:1, :]
                for j in range(1, GDN_CONV - 1 - t):
                    acc = acc + c0[t + j:t + j + 1, :] * wc_ref[j:j + 1, :]
                head.append(acc)
            head.append(jnp.zeros((SUBLANES - (GDN_CONV - 1), GDN_CONV_CH), F32))
            first = slice(sq * seg, sq * seg + SUBLANES)
            qkv_scr[first, :] = qkv_scr[first, :] + jnp.concatenate(head, axis=0)
            convn_ref[0, sq] = h_ref[(sq + 1) * seg - (GDN_CONV - 1):(sq + 1) * seg, 0:GDN_CONV_CH]
        qkv_scr[...] = _silu(qkv_scr[...])

    g_scr[...] = -jnp.exp(alog_ref[...]) * _softplus(ab_ref[:, 0:LANES] + dtb_ref[...])
    beta_scr[...] = _sigmoid(ab_ref[:, LANES:2 * LANES])

    incl, strict, delta = _seg_masks((c, hw), seg)
    eye_pair = jnp.where(delta[:, 0:pw], 1.0, 0.0)
    c_bits, dk_bits = int(math.log2(c)), int(math.log2(GDN_DK))
    r2 = lax.broadcasted_iota(jnp.int32, (2 * c, pw), 0) >> c_bits
    mask_cc = r2 == (lax.broadcasted_iota(jnp.int32, (2 * c, pw), 1) >> c_bits)
    r3 = lax.broadcasted_iota(jnp.int32, (2 * c, 2 * GDN_DK), 0) >> c_bits
    mask_cd = r3 == (lax.broadcasted_iota(jnp.int32, (2 * c, 2 * GDN_DK), 1) >> dk_bits)
    ones_c = jnp.ones((c, c), BF16)
    ng = ng_ref[...]

    def series_inverse(a_list, eye, n, nil, mask, tick):
        ps = [eye - a for a in a_list]
        xs = [_mm3_pair(a, a, mask) for a in a_list]
        tick()
        steps = int(math.log2(nil)) - 1
        for s in range(steps):
            if s < steps - 1:
                rs = [_mm3_pair(jnp.concatenate([pp, x], axis=0), x, mask) for pp, x in zip(ps, xs)]
                ps = [pp + r[0:n] for pp, r in zip(ps, rs)]
                xs = [r[n:2 * n] for r in rs]
            else:
                ps = [pp + _mm3_pair(pp, x, mask) for pp, x in zip(ps, xs)]
            tick()
        return ps

    def gate_forms(rows):
        gc = _cumsum_rows(g_scr[rows, :], seg)
        e_all = eall_ref[...]
        sh, sm, sl = _split3_bf(jnp.concatenate([gc, beta_scr[rows, :]], axis=0))
        ex = _dot(sh, e_all) + (_dot(sm, e_all) + _dot(sl, e_all))
        gcol, bcol = ex[0:c], ex[c:2 * c]
        gwide = jnp.concatenate([jnp.broadcast_to(gc[:, hh:hh + 1], (c, GDN_DK)) for hh in range(GDN_HEADS)],
                                axis=1)
        th, tm, tl = _split3_bf(jnp.concatenate(
            [jnp.where(delta, gcol, 0.0), jnp.where(delta, bcol, 0.0)], axis=1))
        rowf = _dot(ones_c, th) + (_dot(ones_c, tm) + _dot(ones_c, tl))
        grow, brow = rowf[:, 0:hw], rowf[:, hw:]
        dec = jnp.where(incl, jnp.exp(jnp.where(incl, gcol - grow, 0.0)), 0.0)
        egc_scr[rows, :] = jnp.exp(gwide)
        eend = jnp.exp(_seg_last_rows(gwide, seg) - gwide)
        return dec, bcol, brow, jnp.exp(grow), eend

    def local(n, side=None):
        tick = (lambda: None) if side is None else (lambda: next(side, None))
        chunk_rows = [pl.ds(_aligned((n * cpi + i) * c, c), c) for i in range(cpi)]
        forms = [gate_forms(rows) for rows in chunk_rows]
        if geo.long and isinstance(n, int) and n == 0:
            for i in range(cpi):
                conv_chunk(i)
        units = [(i, p) for i in range(cpi) for p in range(n_pair)]
        tick()

        a_pairs = []
        for i, p in units:
            rows = chunk_rows[i]
            dec, bcol, _, _, eend = forms[i]
            qs, ks = [], []
            for hh in (2 * p, 2 * p + 1):
                kq = slice(hh * GDN_DK, (hh + 1) * GDN_DK)
                kk = slice(GDN_KW + hh * GDN_DK, GDN_KW + (hh + 1) * GDN_DK)
                q = qkv_scr[rows, kq]
                k = qkv_scr[rows, kk]
                q = q * lax.rsqrt(jnp.sum(q * q, axis=-1, keepdims=True) + NORM_EPS) * GDN_DK ** -0.5
                k = k * lax.rsqrt(jnp.sum(k * k, axis=-1, keepdims=True) + NORM_EPS)
                qk_scr[rows, kq] = q
                qk_scr[rows, kk] = k
                kend_scr[rows, kq] = k * eend[:, kq]
                qs.append(q)
                ks.append(k)
            kh = _bf(jnp.concatenate(ks, axis=1))
            kbd_h = _bf(_block_rows(kh.astype(F32), mask_cd))
            kkqk = _dot_nt(jnp.concatenate([kh, _bf(jnp.concatenate(qs, axis=1))], axis=0), kbd_h)
            cs = slice(p * pw, (p + 1) * pw)
            a_pairs.append(jnp.where(strict[:, cs], kkqk[0:c] * dec[:, cs] * bcol[:, cs], 0.0))
            att_scr[rows, cs] = kkqk[c:2 * c] * dec[:, cs]
        tick()
        ps = series_inverse(a_pairs, eye_pair, c, seg, mask_cc, tick)
        for t_inv, (i, p) in zip(ps, units):
            rows = chunk_rows[i]
            _, _, brow, egrow, _ = forms[i]
            cs = slice(p * pw, (p + 1) * pw)
            vs = slice(p * 2 * GDN_DV, (p + 1) * 2 * GDN_DV)
            t_b = t_inv * brow[:, cs]
            vh, vl = _split2_bf(qkv_scr[rows, 2 * GDN_KW + p * 2 * GDN_DV:2 * GDN_KW + (p + 1) * 2 * GDN_DV])
            u_scr[rows, vs] = _mm3_split(t_b, _bf(_block_rows(vh.astype(F32), mask_cd)),
                                         _bf(_block_rows(vl.astype(F32), mask_cd)))
            kh, kl = _split2_bf(qk_scr[rows, GDN_KW + p * 2 * GDN_DK:GDN_KW + (p + 1) * 2 * GDN_DK])
            w_scr[rows, vs] = _mm3_split(t_b * egrow[:, cs], _bf(_block_rows(kh.astype(F32), mask_cd)),
                                         _bf(_block_rows(kl.astype(F32), mask_cd)))
        if geo.long:
            for i in range(cpi):
                nxt, cap = (n + 1) * cpi + i, tc // c - cpi + i
                conv_chunk(min(nxt, cap) if isinstance(n, int) else jnp.minimum(nxt, cap))
        if side is not None:
            for _ in side:
                pass

    def recur(n):
        r0 = _aligned(n * c, c)
        rows = pl.ds(r0, c)
        heads = range(GDN_HEADS)
        kqs = [slice(hh * GDN_DK, (hh + 1) * GDN_DK) for hh in heads]
        vn_seg = [[] for _ in heads]
        qs_seg = [[] for _ in heads]
        for s in range(n_seg):
            sq = n * n_seg + s
            sr = pl.ds(_aligned(r0 + s * seg, SUBLANES), seg)
            tail = pl.ds(_aligned(r0 + (s + 1) * seg - SUBLANES, SUBLANES), SUBLANES)
            sts = [get_state(sq, hh) for hh in heads]
            m1s = [_dot(_bf(jnp.concatenate([w_scr[sr, kqs[hh]], qk_scr[sr, kqs[hh]]], axis=0)), _bf(sts[hh]))
                   for hh in heads]
            yield
            vn = [u_scr[sr, kqs[hh]] - m1s[hh][0:seg] for hh in heads]
            upd = [_dot_tn(_bf(kend_scr[sr, kqs[hh]]), _bf(vn[hh])) for hh in heads]
            yield
            for hh in heads:
                e_last = egc_scr[tail, kqs[hh]][SUBLANES - 1:SUBLANES, :]
                put_state(sq, hh, sts[hh] * e_last + upd[hh])
                vn_seg[hh].append(vn[hh])
                qs_seg[hh].append(m1s[hh][seg:2 * seg])
        join = lambda parts: parts[0] if n_seg == 1 else jnp.concatenate(parts, axis=0)
        o_pairs = []
        for p in range(n_pair):
            pair = (2 * p, 2 * p + 1)
            vs = slice(p * 2 * GDN_DV, (p + 1) * 2 * GDN_DV)
            vbd = _bf(_block_rows(jnp.concatenate([join(vn_seg[hh]) for hh in pair], axis=1), mask_cd))
            o_pairs.append(_dot(_bf(att_scr[rows, p * pw:(p + 1) * pw]), vbd)
                           + egc_scr[rows, vs] * jnp.concatenate([join(qs_seg[hh]) for hh in pair], axis=1))
        yield
        for p in range(n_pair):
            for i, hh in enumerate((2 * p, 2 * p + 1)):
                o = o_pairs[p][:, i * GDN_DV:(i + 1) * GDN_DV]
                o = o * lax.rsqrt(jnp.mean(o * o, axis=-1, keepdims=True) + NORM_EPS) * ng
                kz = slice(GDN_CONV_CH + hh * GDN_DV, GDN_CONV_CH + (hh + 1) * GDN_DV)
                og_scr[rows, hh * GDN_DV:(hh + 1) * GDN_DV] = o * _silu(h_ref[rows, kz])
        yield

    def recur_trip(n):
        for i in range(cpi):
            yield from recur(n * cpi + i)

    def drain(gen):
        for _ in gen:
            pass

    n_trip = tc // (c * cpi)
    local(0)

    def trip(n, carry):
        local(n, recur_trip(n - 1))
        return carry

    lax.fori_loop(1, n_trip, trip, 0)
    drain(recur_trip(n_trip - 1))

    if geo.long:
        carry_scr[...] = h_ref[tc - pad:tc, 0:GDN_CONV_CH]
        _out_ln_body(og_scr, x_ref, wout_ref, lng_ref, lnb_ref, og_ref)

        @pl.when(geo.last_time_block())
        def _():
            sout_ref[0, 0] = s_scr[...]


def _gdn_recur(h, ab, w_conv, a_log, dt_bias, norm_g, conv_in, state_in, layer, n_layers, geo, out_ln):
    tail = (GDN_HEADS, GDN_DK, GDN_DV)
    cv_tail = (GDN_CONV - 1, GDN_CONV_CH)
    consts = [w_conv, a_log, dt_bias, norm_g, _gdn_expand_matrix()]
    ins = [h, ab] + consts
    in_specs = [pl.BlockSpec((geo.tc, GDN_MAIN), geo.row_map), pl.BlockSpec((geo.tc, 2 * LANES), geo.row_map)]
    in_specs += [_const_spec(a) for a in consts]
    tc = geo.tc
    if geo.long:
        ins += list(out_ln)
        in_specs += [pl.BlockSpec((tc, D_MODEL), geo.row_map)] + [_const_spec(a) for a in out_ln[1:]]
    else:
        assert tc == CHUNK
        ins += [conv_in, state_in]
        in_specs += [geo.state_spec(layer, cv_tail), geo.state_spec(layer, tail)]
    scratch = [pltpu.VMEM((tc, GDN_CONV_CH), F32),
               pltpu.VMEM((tc, LANES), F32), pltpu.VMEM((tc, LANES), F32),
               pltpu.VMEM((tc, GDN_VW), F32), pltpu.VMEM((tc, GDN_KW), F32),
               pltpu.VMEM((tc, GDN_HEADS * CHUNK), F32), pltpu.VMEM((tc, GDN_KW), F32),
               pltpu.VMEM((tc, GDN_KW), F32), pltpu.VMEM((tc, 2 * GDN_KW), F32)]
    if geo.long:
        scratch += [pltpu.VMEM((SUBLANES, GDN_CONV_CH), F32), pltpu.VMEM((tc, GDN_VW), F32),
                    pltpu.VMEM(tail, F32)]
    return pl.pallas_call(
        functools.partial(_gdn_recur_body, geo=geo),
        grid=geo.grid, in_specs=in_specs,
        out_specs=[pl.BlockSpec((tc, GDN_VW), geo.row_map), geo.state_spec(layer, cv_tail),
                   geo.state_spec(layer, tail)],
        out_shape=[jax.ShapeDtypeStruct((geo.n_b * geo.n_l, GDN_VW), F32),
                   jax.ShapeDtypeStruct((n_layers, geo.n_b) + cv_tail, F32),
                   jax.ShapeDtypeStruct((n_layers, geo.n_b) + tail, F32)],
        scratch_shapes=scratch,
        compiler_params=_params(geo.sem), name="gdn_recur",
    )(*ins)


def _s5_body(*refs, tt, nb, has_init):
    if has_init:
        (x_ref, win_ref, wb_ref, are_ref, aim_ref, wc_ref, d_ref, wglu_ref, bglu_ref, wout_ref,
         lng_ref, lnb_ref, h0re_ref, h0im_ref, y_ref, hre_ref, him_ref,
         x_scr, u_scr, z_scr, bu_scr, y_scr, st_scr) = refs
    else:
        (x_ref, win_ref, wb_ref, are_ref, aim_ref, wc_ref, d_ref, wglu_ref, bglu_ref, wout_ref,
         lng_ref, lnb_ref, y_ref, hre_ref, him_ref,
         x_scr, u_scr, z_scr, bu_scr, y_scr, st_scr) = refs
    tb = pl.program_id(1)
    ns = S5_KT_STATES

    @pl.when(tb == 0)
    def _():
        if not has_init:
            st_scr[...] = jnp.zeros_like(st_scr)
        else:
            for kt in range(S5_KT):
                st_scr[:, 2 * kt * ns:(2 * kt + 1) * ns] = h0re_ref[0, :, kt * ns:(kt + 1) * ns]
                st_scr[:, (2 * kt + 1) * ns:(2 * kt + 2) * ns] = h0im_ref[0, :, kt * ns:(kt + 1) * ns]

    x_scr[...] = jnp.swapaxes(x_ref[...], 0, 1).reshape(tt * nb, D_MODEL)
    h = _dot(_bf(x_scr[...]), win_ref[...])
    u_scr[...] = h[:, 0:S5_WIDTH]
    z_scr[...] = h[:, S5_WIDTH:]

    def input_map(kt):
        cols = slice(kt * S5_KT_W, (kt + 1) * S5_KT_W)
        bu_scr[kt % 2] = _dot(_bf(u_scr[:, cols]), wb_ref[kt])

    def scan(kt):
        bu = bu_scr.at[kt % 2]
        a_re = jnp.broadcast_to(are_ref[kt], (SUBLANES, ns))
        a_im = jnp.broadcast_to(aim_ref[kt], (SUBLANES, ns))
        base = kt * 2 * ns
        for rb in range(nb // SUBLANES):
            st_rows = slice(rb * SUBLANES, (rb + 1) * SUBLANES)
            h_re = st_scr[st_rows, base:base + ns]
            h_im = st_scr[st_rows, base + ns:base + 2 * ns]
            for t in range(tt):
                rows = slice(t * nb + rb * SUBLANES, t * nb + (rb + 1) * SUBLANES)
                h_re, h_im = (a_re * h_re - a_im * h_im + bu[rows, 0:ns],
                              a_re * h_im + a_im * h_re + bu[rows, ns:2 * ns])
                bu[rows, 0:ns] = h_re
                bu[rows, ns:2 * ns] = h_im
            st_scr[st_rows, base:base + ns] = h_re
            st_scr[st_rows, base + ns:base + 2 * ns] = h_im

    def output_map(kt):
        cols = slice(kt * S5_KT_W, (kt + 1) * S5_KT_W)
        y_scr[:, cols] = _dot(_bf(bu_scr[kt % 2]), wc_ref[kt]) + d_ref[:, cols] * u_scr[:, cols]

    input_map(0)
    for kt in range(S5_KT):
        if kt + 1 < S5_KT:
            input_map(kt + 1)
        scan(kt)
        output_map(kt)

    th = tt // 2
    halves = [slice(i * th * nb, (i + 1) * th * nb) for i in range(2)]

    def glu(r):
        return _dot(_bf(_gelu_tanh(y_scr[r, :])), wglu_ref[...]) + bglu_ref[...]

    def gated_out(r, yg):
        y = yg[:, 0:S5_WIDTH] * _sigmoid(yg[:, S5_WIDTH:]) * _silu(z_scr[r, :])
        return _dot(_bf(y), wout_ref[...])

    def finish(i, out):
        y = _layer_norm(ALPHA_RES * x_scr[halves[i], :] + out, lng_ref[...], lnb_ref[...])
        y_ref[:, i * th:(i + 1) * th, :] = jnp.swapaxes(y.reshape(th, nb, D_MODEL), 0, 1)

    yg0 = glu(halves[0])
    yg1 = glu(halves[1])
    out0 = gated_out(halves[0], yg0)
    out1 = gated_out(halves[1], yg1)
    finish(0, out0)
    finish(1, out1)

    @pl.when(tb == pl.num_programs(1) - 1)
    def _():
        for kt in range(S5_KT):
            hre_ref[0, :, kt * ns:(kt + 1) * ns] = st_scr[:, 2 * kt * ns:(2 * kt + 1) * ns]
            him_ref[0, :, kt * ns:(kt + 1) * ns] = st_scr[:, (2 * kt + 1) * ns:(2 * kt + 2) * ns]


def _s5_discretize(lam_re, lam_im, log_dt, b_re, b_im, c_re, c_im):
    dt = jnp.exp(log_dt)[:, None]
    mag = jnp.exp(lam_re * dt)
    ab_re, ab_im = mag * jnp.cos(lam_im * dt), mag * jnp.sin(lam_im * dt)
    den = jnp.square(lam_re) + jnp.square(lam_im)
    num_re = ab_re - 1.0
    coef_re = (num_re * lam_re + ab_im * lam_im) / den
    coef_im = (ab_im * lam_re - num_re * lam_im) / den
    bb_re = coef_re[..., None] * b_re - coef_im[..., None] * b_im
    bb_im = coef_re[..., None] * b_im + coef_im[..., None] * b_re
    gl = S5_GROUPS // S5_KT
    rep = (jnp.arange(S5_KT_STATES)[None, :] % S5_STATE == jnp.arange(S5_STATE)[:, None]).astype(F32)
    same = jnp.arange(S5_KT_W)[:, None] // S5_GROUP == jnp.arange(S5_KT_STATES)[None, :] // S5_STATE
    exact = lax.Precision.HIGHEST

    def block_b(bb):
        t = bb.reshape(S5_KT, gl, S5_STATE, S5_GROUP).transpose(0, 1, 3, 2).reshape(S5_KT, S5_KT_W, S5_STATE)
        return jnp.where(same, jnp.einsum("krp,pn->krn", t, rep, precision=exact), 0.0)

    def block_c(cc):
        t = cc.reshape(S5_KT, gl, S5_GROUP, S5_STATE).transpose(0, 3, 1, 2).reshape(S5_KT, S5_STATE, S5_KT_W)
        return jnp.where(same.T, jnp.einsum("pn,kpr->knr", rep, t, precision=exact), 0.0)

    w_b = jnp.concatenate([block_b(bb_re), block_b(bb_im)], axis=2).astype(BF16)
    w_c = jnp.concatenate([block_c(c_re), -block_c(c_im)], axis=1).astype(BF16)
    a_re = ab_re.reshape(S5_KT, 1, S5_KT_STATES)
    a_im = ab_im.reshape(S5_KT, 1, S5_KT_STATES)
    return w_b, w_c, a_re, a_im


def _s5_layer(x, prep, w_in, d_vec, w_glu, b_glu, w_out, ln_g, ln_b, h0, n_b, n_l):
    w_b, w_c, a_re, a_im = prep
    if n_l >= S5_LONG_STEPS:
        nb, tt = n_b, S5_LONG_STEPS
    else:
        nb, tt = min(S5_SHORT_SEQS, n_b), n_l
    assert nb % SUBLANES == 0 and tt % SUBLANES == 0
    rows = tt * nb
    has_init = h0 is not None
    consts = [w_in, w_b, a_re, a_im, w_c, d_vec, w_glu, b_glu, w_out, ln_g, ln_b]
    x_spec = pl.BlockSpec((nb, tt, D_MODEL), lambda b, t: (b, t, 0))
    st_spec = pl.BlockSpec((1, nb, S5_NSTATE), lambda b, t: (0, b, 0))
    ins = [x.reshape(n_b, n_l, D_MODEL)] + consts
    in_specs = [x_spec] + [_const_spec(a) for a in consts]
    if has_init:
        ins += list(h0)
        in_specs += [st_spec, st_spec]
    y, h_re, h_im = pl.pallas_call(
        functools.partial(_s5_body, tt=tt, nb=nb, has_init=has_init),
        grid=(n_b // nb, n_l // tt), in_specs=in_specs,
        out_specs=[x_spec, st_spec, st_spec],
        out_shape=[jax.ShapeDtypeStruct((n_b, n_l, D_MODEL), F32),
                   jax.ShapeDtypeStruct((1, n_b, S5_NSTATE), F32),
                   jax.ShapeDtypeStruct((1, n_b, S5_NSTATE), F32)],
        scratch_shapes=[pltpu.VMEM((rows, D_MODEL), F32),
                        pltpu.VMEM((rows, S5_WIDTH), F32), pltpu.VMEM((rows, S5_WIDTH), F32),
                        pltpu.VMEM((2, rows, 2 * S5_KT_STATES), F32), pltpu.VMEM((rows, S5_WIDTH), F32),
                        pltpu.VMEM((nb, 2 * S5_NSTATE), F32)],
        compiler_params=_params(("parallel", "arbitrary")), name="s5_layer",
    )(*ins)
    return y.reshape(n_b * n_l, D_MODEL), h_re, h_im


def _row2(v):
    return v.reshape(1, -1).astype(F32)


def _pad_cols(w, n):
    return jnp.concatenate([w, jnp.zeros((w.shape[0], n - w.shape[1]), w.dtype)], axis=1)


def _trunk(x, n_b, n_l, states, wts):
    geo = _Geometry(n_b, n_l)
    n_gla, n_gdn, n_s5 = (DEPTH + 2) // 3, (DEPTH + 1) // 3, DEPTH // 3
    assert n_gdn == 1 and n_s5 == 1
    s_gla = s_gdn = s_conv = s_re = s_im = None
    for i in range(DEPTH):
        j, kind = divmod(i, 3)
        ln_g, ln_b = _row2(wts["ln_g"][i]), _row2(wts["ln_b"][i])
        if kind == 0:
            w_in = wts["gla_w_in"][j]
            w_main = _bf(w_in)
            w_lr = _bf(_pad_cols(w_in[:, GLA_MAIN:], LANES))
            w_a2 = _bf(jnp.concatenate(
                [wts["gla_w_a2"][j], jnp.zeros((LANES - GLA_LOWRANK, GLA_KW), F32)], axis=0))
            h, la = _rowwise_call(_gla_proj_body, "gla_proj", [x],
                                  [w_main, w_lr, w_a2, _row2(wts["gla_b_a"][j])], [GLA_MAIN, GLA_KW])
            w_out = _bf(wts["gla_w_out"][j])
            og, s_gla = _gla_recur(h, la, _row2(wts["gla_norm_g"][j]),
                                   None if states is None else states[0], j, n_gla, s_gla, geo,
                                   (x, w_out, ln_g, ln_b))
        elif kind == 1:
            w_in = wts["gdn_w_in"][j]
            w_main = _bf(w_in)
            w_ab = _bf(jnp.concatenate(
                [_pad_cols(w_in[:, GDN_MAIN:GDN_MAIN + GDN_HEADS], LANES),
                 _pad_cols(w_in[:, GDN_MAIN + GDN_HEADS:], LANES)], axis=1))
            h, ab = _rowwise_call(_gdn_proj_body, "gdn_proj", [x], [w_main, w_ab], [GDN_MAIN, 2 * LANES])
            w_out = _bf(wts["gdn_w_out"][j])
            og, s_conv, s_gdn = _gdn_recur(
                h, ab, wts["gdn_w_conv"][j].astype(F32),
                _pad_cols(_row2(wts["gdn_a_log"][j]), LANES), _pad_cols(_row2(wts["gdn_dt_bias"][j]), LANES),
                _row2(wts["gdn_norm_g"][j]),
                None if states is None else states[2], None if states is None else states[1],
                j, n_gdn, geo, (x, w_out, ln_g, ln_b))
        else:
            prep = _s5_discretize(wts["s5_lam_re"][j].astype(F32), wts["s5_lam_im"][j].astype(F32),
                                  wts["s5_log_dt"][j].astype(F32), wts["s5_b_re"][j].astype(F32),
                                  wts["s5_b_im"][j].astype(F32), wts["s5_c_re"][j].astype(F32),
                                  wts["s5_c_im"][j].astype(F32))
            h0 = None if states is None else (states[3].reshape(n_s5, n_b, S5_NSTATE),
                                              states[4].reshape(n_s5, n_b, S5_NSTATE))
            x, h_re, h_im = _s5_layer(x, prep, _bf(wts["s5_w_in"][j]), _row2(wts["s5_d"][j]),
                                      _bf(wts["s5_w_glu"][j]), _row2(wts["s5_b_glu"][j]),
                                      _bf(wts["s5_w_out"][j]), ln_g, ln_b, h0, n_b, n_l)
            s_re = h_re.reshape(n_s5, n_b, S5_GROUPS, S5_STATE)
            s_im = h_im.reshape(n_s5, n_b, S5_GROUPS, S5_STATE)
            continue
        if geo.long:
            x = og
        else:
            (x,) = _rowwise_call(_out_ln_body, "out_ln", [og, x], [w_out, ln_g, ln_b], [D_MODEL])
    return x, s_gla, s_gdn, s_conv, s_re, s_im


def kernel(x_prompt, x_sample, state_gla, state_gdn, state_gdn_conv, state_s5_re, state_s5_im,
           ln_g, ln_b, gla_w_in, gla_w_a2, gla_b_a, gla_norm_g, gla_w_out,
           gdn_w_in, gdn_w_conv, gdn_a_log, gdn_dt_bias, gdn_norm_g, gdn_w_out,
           s5_w_in, s5_lam_re, s5_lam_im, s5_log_dt, s5_b_re, s5_b_im, s5_c_re, s5_c_im,
           s5_d, s5_w_glu, s5_b_glu, s5_w_out):
    wts = dict(ln_g=ln_g, ln_b=ln_b,
               gla_w_in=gla_w_in, gla_w_a2=gla_w_a2, gla_b_a=gla_b_a, gla_norm_g=gla_norm_g,
               gla_w_out=gla_w_out,
               gdn_w_in=gdn_w_in, gdn_w_conv=gdn_w_conv, gdn_a_log=gdn_a_log, gdn_dt_bias=gdn_dt_bias,
               gdn_norm_g=gdn_norm_g, gdn_w_out=gdn_w_out,
               s5_w_in=s5_w_in, s5_lam_re=s5_lam_re, s5_lam_im=s5_lam_im, s5_log_dt=s5_log_dt,
               s5_b_re=s5_b_re, s5_b_im=s5_b_im, s5_c_re=s5_c_re, s5_c_im=s5_c_im, s5_d=s5_d,
               s5_w_glu=s5_w_glu, s5_b_glu=s5_b_glu, s5_w_out=s5_w_out)

    bp, lp, _ = x_prompt.shape
    yp, p_gla, p_gdn, p_conv, p_re, p_im = _trunk(
        x_prompt.reshape(bp * lp, D_MODEL), bp, lp, None, wts)
    bs, ls, _ = x_sample.shape
    ys, s_gla, s_gdn, s_conv, s_re, s_im = _trunk(
        x_sample.reshape(bs * ls, D_MODEL), bs, ls,
        (state_gla, state_gdn, state_gdn_conv, state_s5_re, state_s5_im), wts)
    return (yp.reshape(bp, lp, D_MODEL), ys.reshape(bs, ls, D_MODEL),
            p_gla, p_gdn, p_conv, p_re, p_im, s_gla, s_gdn, s_conv, s_re, s_im)
```
